```python
import math
import jax, jax.numpy as jnp
from jax import lax
import numpy as np

D_MODEL = 1024
BATCH = 16
SEQ = 2048
DEPTH = 4

MEM_LEN = 256
MIX_WIDTH = D_MODEL
SSM_WIDTH = D_MODEL // 4
SSM_GROUP = 16
SSM_GROUPS = SSM_WIDTH // SSM_GROUP
SSM_STATE = 64
SSM_DT_MIN = 1e-3
SSM_DT_MAX = 1e-1
RET_HEADS = 4
RET_HEAD_DIM = D_MODEL // 16
RET_WIDTH = RET_HEADS * RET_HEAD_DIM
RET_CHUNK = 128
ROPE_BASE = 10000.0
SB_HEADS = 8
SB_HEAD_DIM = D_MODEL // 16
SB_WIDTH = SB_HEADS * SB_HEAD_DIM
SB_BLOCK = 128
IN_COLS = SSM_WIDTH + 4 * RET_WIDTH + 3 * SB_WIDTH
CROSS_HEADS = 4
CROSS_HEAD_DIM = D_MODEL // CROSS_HEADS
D_FF = ((8 * D_MODEL // 3 + 255) // 256) * 256
N_EXPERTS = 8
TOP_K = 2
N_DENSE = (DEPTH + 1) // 2
N_MOE = DEPTH // 2
NORM_EPS = 1e-6
GN_EPS = 1e-6

kernel_name = "hybrid_s5_retention_stickbreak_moe"

F32 = jnp.float32


def rms_norm(x, w):
    xf = x.astype(F32)
    y = xf * lax.rsqrt(jnp.mean(xf * xf, axis=-1, keepdims=True) + NORM_EPS)
    return (y * w.astype(F32)).astype(x.dtype)


def _complex_linear_combine(e1, e2):
    a1r, a1i, b1r, b1i = e1
    a2r, a2i, b2r, b2i = e2
    ar = a1r * a2r - a1i * a2i
    ai = a1r * a2i + a1i * a2r
    br = a2r * b1r - a2i * b1i + b2r
    bi = a2r * b1i + a2i * b1r + b2i
    return (ar, ai, br, bi)


def s5_mixer(u, lam_re, lam_im, b_re, b_im, c_re, c_im, d_skip, log_dt, w_glu):
    dtype = u.dtype
    bsz, seq, _ = u.shape
    uf = u.astype(F32)
    ug = uf.reshape(bsz, seq, SSM_GROUPS, SSM_GROUP)
    lr = lam_re.astype(F32)
    li = lam_im.astype(F32)
    dt = jnp.exp(log_dt.astype(F32))[:, None]
    mag = jnp.exp(lr * dt)
    a_re = mag * jnp.cos(li * dt)
    a_im = mag * jnp.sin(li * dt)
    denom = lr * lr + li * li
    nr = a_re - 1.0
    z_re = (nr * lr + a_im * li) / denom
    z_im = (a_im * lr - nr * li) / denom
    br = b_re.astype(F32)
    bi = b_im.astype(F32)
    bb_re = z_re[..., None] * br - z_im[..., None] * bi
    bb_im = z_re[..., None] * bi + z_im[..., None] * br
    bu_re = jnp.einsum('bsgh,gph->bsgp', ug, bb_re)
    bu_im = jnp.einsum('bsgh,gph->bsgp', ug, bb_im)
    a_re_s = jnp.broadcast_to(a_re, (1, seq) + a_re.shape)
    a_im_s = jnp.broadcast_to(a_im, (1, seq) + a_im.shape)
    _, _, x_re, x_im = lax.associative_scan(
        _complex_linear_combine, (a_re_s, a_im_s, bu_re, bu_im), axis=1)
    y = (jnp.einsum('bsgp,ghp->bsgh', x_re, c_re.astype(F32))
         - jnp.einsum('bsgp,ghp->bsgh', x_im, c_im.astype(F32)))
    y = y.reshape(bsz, seq, SSM_WIDTH) + d_skip.astype(F32) * uf
    g = jax.nn.gelu(y)
    out = g * jax.nn.sigmoid(g @ w_glu.astype(F32))
    return out.astype(dtype)


def rotary(x, pos):
    half = x.shape[-1] // 2
    inv_freq = ROPE_BASE ** (-jnp.arange(half, dtype=F32) / half)
    ang = (pos[:, None] * inv_freq[None, :])[None, :, None, :]
    cos, sin = jnp.cos(ang), jnp.sin(ang)
    x1, x2 = x[..., :half], x[..., half:]
    return jnp.concatenate([x1 * cos - x2 * sin, x1 * sin + x2 * cos], axis=-1)


def retention_mixer(q, k, v, g, gn_w):
    dtype = q.dtype
    bsz, seq, _ = q.shape
    H, dh = RET_HEADS, RET_HEAD_DIM
    pos = jnp.arange(seq, dtype=F32)
    qh = rotary(q.reshape(bsz, seq, H, dh).astype(F32), pos)
    kh = rotary(k.reshape(bsz, seq, H, dh).astype(F32), pos) * (dh ** -0.5)
    vh = v.reshape(bsz, seq, H, dh).astype(F32)
    n_chunks = seq // RET_CHUNK

    def to_chunks(t):
        return t.reshape(bsz, n_chunks, RET_CHUNK, H, dh).transpose(1, 0, 3, 2, 4)

    log_gamma = jnp.log1p(-(2.0 ** (-5.0 - jnp.arange(H, dtype=F32))))
    idx = jnp.arange(RET_CHUNK, dtype=F32)
    rel = idx[:, None] - idx[None, :]
    intra = jnp.where(rel >= 0,
                      jnp.exp(log_gamma[:, None, None] * jnp.maximum(rel, 0.0)),
                      0.0)
    xi = jnp.exp(log_gamma[:, None] * (idx + 1.0))[..., None]
    zeta = jnp.exp(log_gamma[:, None] * (RET_CHUNK - 1.0 - idx))[..., None]
    chunk_decay = jnp.exp(log_gamma * RET_CHUNK)[:, None, None]

    def step(state, qkv):
        qc, kc, vc = qkv
        scores = jnp.einsum('bhnd,bhmd->bhnm', qc, kc) * intra
        out = (jnp.einsum('bhnm,bhme->bhne', scores, vc)
               + jnp.einsum('bhnd,bhde->bhne', qc, state) * xi)
        state = chunk_decay * state + jnp.einsum('bhmd,bhme->bhde', kc * zeta, vc)
        return state, out

    state0 = jnp.zeros((bsz, H, dh, dh), F32)
    _, o = lax.scan(step, state0, (to_chunks(qh), to_chunks(kh), to_chunks(vh)))
    o = o.transpose(1, 0, 3, 2, 4).reshape(bsz, seq, H, dh)
    mu = jnp.mean(o, axis=-1, keepdims=True)
    var = jnp.mean(jnp.square(o - mu), axis=-1, keepdims=True)
    o = ((o - mu) * lax.rsqrt(var + GN_EPS)).reshape(bsz, seq, RET_WIDTH) * gn_w.astype(F32)
    out = jax.nn.silu(g.astype(F32)) * o
    return out.astype(dtype)


def stick_breaking_mixer(q, k, v):
    dtype = q.dtype
    bsz, seq, _ = q.shape
    H, dh = SB_HEADS, SB_HEAD_DIM
    qh = q.reshape(bsz, seq, H, dh).transpose(0, 2, 1, 3).astype(F32) * (dh ** -0.5)
    kh = k.reshape(bsz, seq, H, dh).transpose(0, 2, 1, 3).astype(F32)
    vh = v.reshape(bsz, seq, H, dh).transpose(0, 2, 1, 3).astype(F32)
    n_blocks = seq // SB_BLOCK
    q_blocks = qh.reshape(bsz, H, n_blocks, SB_BLOCK, dh).transpose(2, 0, 1, 3, 4)
    starts = jnp.arange(n_blocks, dtype=jnp.int32) * SB_BLOCK
    k_pos = jnp.arange(seq, dtype=jnp.int32)

    def block(args):
        qb, t0 = args
        z = jnp.einsum('bhqd,bhkd->bhqk', qb, kh)
        q_pos = t0 + jnp.arange(SB_BLOCK, dtype=jnp.int32)
        mask = k_pos[None, :] < q_pos[:, None]
        log_fail = jnp.where(mask, jax.nn.log_sigmoid(-z), 0.0)
        after = lax.cumsum(log_fail, axis=3, reverse=True) - log_fail
        w = jnp.where(mask, jnp.exp(jax.nn.log_sigmoid(z) + after), 0.0)
        return jnp.einsum('bhqk,bhkd->bhqd', w, vh)

    o = lax.map(block, (q_blocks, starts))
    o = o.transpose(1, 0, 3, 2, 4).reshape(bsz, seq, SB_WIDTH)
    return o.astype(dtype)


def cross_attention(h, mem_n, w_q, w_kv, w_o):
    bsz, seq, _ = h.shape
    m = mem_n.shape[1]
    q = (h @ w_q).reshape(bsz, seq, CROSS_HEADS, CROSS_HEAD_DIM)
    kv = (mem_n @ w_kv).reshape(bsz, m, 2, CROSS_HEADS, CROSS_HEAD_DIM)
    k, v = kv[:, :, 0], kv[:, :, 1]
    s = jnp.einsum('bshd,bmhd->bhsm', q, k).astype(F32) * (CROSS_HEAD_DIM ** -0.5)
    p = jax.nn.softmax(s, axis=-1).astype(v.dtype)
    o = jnp.einsum('bhsm,bmhd->bshd', p, v).reshape(bsz, seq, D_MODEL)
    return o @ w_o


def swiglu(x, w_gu, w_down):
    gate, up = jnp.split(x @ w_gu, 2, axis=-1)
    return (jax.nn.silu(gate) * up) @ w_down


def moe_swiglu(x, w_router, w_gu, w_down):
    bsz, seq, d = x.shape
    t = x.reshape(bsz * seq, d)
    logits = (t @ w_router).astype(F32)
    top_logits, top_idx = lax.top_k(logits, TOP_K)
    top_w = jax.nn.softmax(top_logits, axis=-1)
    gates = jnp.sum(jax.nn.one_hot(top_idx, N_EXPERTS, dtype=F32) * top_w[..., None], axis=1)
    out = jnp.zeros_like(t)
    for e in range(N_EXPERTS):
        out = out + gates[:, e:e + 1].astype(t.dtype) * swiglu(t, w_gu[e], w_down[e])
    return out.reshape(bsz, seq, d)


def setup_inputs(seed: int = 0) -> dict:
    key = jax.random.key(seed)
    ks = iter(jax.random.split(key, 32))

    def normal(shape, scale):
        return jax.random.normal(next(ks), shape, F32) * scale

    def gain(shape):
        return 1.0 + normal(shape, 0.02)

    n_idx = jnp.arange(SSM_STATE, dtype=F32)
    lam_re = -0.5 + normal((DEPTH, SSM_GROUPS, SSM_STATE), 0.005)
    lam_im = math.pi * n_idx[None, None, :] + normal((DEPTH, SSM_GROUPS, SSM_STATE), 0.005)
    log_dt = jax.random.uniform(next(ks), (DEPTH, SSM_GROUPS), F32,
                                math.log(SSM_DT_MIN), math.log(SSM_DT_MAX))
    return {
        "x": normal((BATCH, SEQ, D_MODEL), 1.0),
        "mem": normal((BATCH, MEM_LEN, D_MODEL), 1.0),
        "mix_norm_w": gain((DEPTH, D_MODEL)),
        "w_in": normal((DEPTH, D_MODEL, IN_COLS), D_MODEL ** -0.5),
        "ssm_lambda_re": lam_re,
        "ssm_lambda_im": lam_im,
        "ssm_b_re": normal((DEPTH, SSM_GROUPS, SSM_STATE, SSM_GROUP), (2 * SSM_GROUP) ** -0.5),
        "ssm_b_im": normal((DEPTH, SSM_GROUPS, SSM_STATE, SSM_GROUP), (2 * SSM_GROUP) ** -0.5),
        "ssm_c_re": normal((DEPTH, SSM_GROUPS, SSM_GROUP, SSM_STATE), SSM_STATE ** -0.5),
        "ssm_c_im": normal((DEPTH, SSM_GROUPS, SSM_GROUP, SSM_STATE), SSM_STATE ** -0.5),
        "ssm_d": normal((DEPTH, SSM_WIDTH), 1.0),
        "ssm_log_dt": log_dt,
        "ssm_w_glu": normal((DEPTH, SSM_WIDTH, SSM_WIDTH), SSM_WIDTH ** -0.5),
        "mix_out_gain": gain((DEPTH, MIX_WIDTH)),
        "w_out": normal((DEPTH, MIX_WIDTH, D_MODEL), MIX_WIDTH ** -0.5),
        "cross_norm_w": gain((DEPTH, D_MODEL)),
        "mem_norm_w": gain((D_MODEL,)),
        "w_cross_q": normal((DEPTH, D_MODEL, D_MODEL), D_MODEL ** -0.5),
        "w_cross_kv": normal((DEPTH, D_MODEL, 2 * D_MODEL), D_MODEL ** -0.5),
        "w_cross_o": normal((DEPTH, D_MODEL, D_MODEL), D_MODEL ** -0.5),
        "ffn_norm_w": gain((DEPTH, D_MODEL)),
        "w_dense_gu": normal((N_DENSE, D_MODEL, 2 * D_FF), D_MODEL ** -0.5),
        "w_dense_down": normal((N_DENSE, D_FF, D_MODEL), D_FF ** -0.5),
        "w_router": normal((N_MOE, D_MODEL, N_EXPERTS), D_MODEL ** -0.5),
        "w_expert_gu": normal((N_MOE, N_EXPERTS, D_MODEL, 2 * D_FF), D_MODEL ** -0.5),
        "w_expert_down": normal((N_MOE, N_EXPERTS, D_FF, D_MODEL), D_FF ** -0.5),
        "final_norm_w": gain((D_MODEL,)),
    }


def reference(x, mem, mix_norm_w, w_in, ssm_lambda_re, ssm_lambda_im, ssm_b_re, ssm_b_im,
              ssm_c_re, ssm_c_im, ssm_d, ssm_log_dt, ssm_w_glu, mix_out_gain, w_out,
              cross_norm_w, mem_norm_w, w_cross_q, w_cross_kv, w_cross_o, ffn_norm_w,
              w_dense_gu, w_dense_down, w_router, w_expert_gu, w_expert_down, final_norm_w):
    o_ret = SSM_WIDTH
    o_sb = SSM_WIDTH + 4 * RET_WIDTH
    mem_n = rms_norm(mem, mem_norm_w)
    h = x
    for i in range(DEPTH):
        proj = rms_norm(h, mix_norm_w[i]) @ w_in[i]
        u_ssm = proj[..., :o_ret]
        r_q = proj[..., o_ret:o_ret + RET_WIDTH]
        r_k = proj[..., o_ret + RET_WIDTH:o_ret + 2 * RET_WIDTH]
        r_v = proj[..., o_ret + 2 * RET_WIDTH:o_ret + 3 * RET_WIDTH]
        r_g = proj[..., o_ret + 3 * RET_WIDTH:o_sb]
        s_q = proj[..., o_sb:o_sb + SB_WIDTH]
        s_k = proj[..., o_sb + SB_WIDTH:o_sb + 2 * SB_WIDTH]
        s_v = proj[..., o_sb + 2 * SB_WIDTH:o_sb + 3 * SB_WIDTH]
        gain_i = mix_out_gain[i]
        y_a = s5_mixer(u_ssm, ssm_lambda_re[i], ssm_lambda_im[i], ssm_b_re[i], ssm_b_im[i],
                       ssm_c_re[i], ssm_c_im[i], ssm_d[i], ssm_log_dt[i], ssm_w_glu[i])
        y_b = retention_mixer(r_q, r_k, r_v, r_g, gain_i[SSM_WIDTH:SSM_WIDTH + RET_WIDTH])
        y_c = stick_breaking_mixer(s_q, s_k, s_v)
        y = jnp.concatenate([
            rms_norm(y_a, gain_i[:SSM_WIDTH]),
            y_b,
            rms_norm(y_c, gain_i[SSM_WIDTH + RET_WIDTH:]),
        ], axis=-1)
        h = h + y @ w_out[i]
        h = h + cross_attention(rms_norm(h, cross_norm_w[i]), mem_n,
                                w_cross_q[i], w_cross_kv[i], w_cross_o[i])
        hn = rms_norm(h, ffn_norm_w[i])
        if i % 2 == 0:
            h = h + swiglu(hn, w_dense_gu[i // 2], w_dense_down[i // 2])
        else:
            h = h + moe_swiglu(hn, w_router[i // 2], w_expert_gu[i // 2], w_expert_down[i // 2])
    return rms_norm(h, final_norm_w)
```

```python
import functools
import math

import numpy as np
import jax
import jax.numpy as jnp
from jax import lax
from jax.experimental import pallas as pl
from jax.experimental.pallas import tpu as pltpu

F32 = jnp.float32
BF16 = jnp.bfloat16

NORM_EPS = 1e-6
GN_EPS = 1e-6
ROPE_BASE = 10000.0

SSM_GROUP = 16
SSM_STATE = 64
RET_HEADS = 4
SB_HEADS = 8
CROSS_HEADS = 4
N_EXPERTS = 8

V7X_VMEM_BYTES = 64 * 1024 * 1024
LANES = 128
SUBLANES = 8


def _cparams(sem, vmem_mb=48):
    return pltpu.CompilerParams(dimension_semantics=sem, vmem_limit_bytes=vmem_mb * 1024 * 1024)


def _rms(x, w):
    ms = jnp.mean(x * x, axis=-1, keepdims=True)
    return x * lax.rsqrt(ms + NORM_EPS) * w


def _dot(a, b):
    return jnp.dot(a, b, preferred_element_type=F32)


def _dot_nt(a, b):
    return lax.dot_general(a, b, (((1,), (1,)), ((), ())), preferred_element_type=F32)


def _dot_tn(a, b):
    return lax.dot_general(a, b, (((0,), (0,)), ((), ())), preferred_element_type=F32)


def _split_dot(x, m_bf16):
    hi = x.astype(BF16)
    lo = (x - hi.astype(F32)).astype(BF16)
    return _dot(hi, m_bf16) + _dot(lo, m_bf16)


def _norm_proj_kernel(x_ref, nw_ref, w_ref, *o_refs, splits, chunk):
    xn = _rms(x_ref[...], nw_ref[...]).astype(BF16)
    col = 0
    for o_ref, width in zip(o_refs, splits):
        for c0 in range(0, width, chunk):
            r = _dot(xn, w_ref[:, col + c0:col + c0 + chunk])
            o_ref[:, c0:c0 + chunk] = r.astype(o_ref.dtype)
        col += width


def norm_proj(x, nw, w, splits, dtypes, tm=512, chunk=256):
    rows, d = x.shape
    n = w.shape[1]
    assert sum(splits) == n and rows % tm == 0
    return pl.pallas_call(
        functools.partial(_norm_proj_kernel, splits=splits, chunk=chunk),
        grid=(rows // tm,),
        in_specs=[
            pl.BlockSpec((tm, d), lambda i: (i, 0)),
            pl.BlockSpec((1, d), lambda i: (0, 0)),
            pl.BlockSpec((d, n), lambda i: (0, 0)),
        ],
        out_specs=[pl.BlockSpec((tm, s), lambda i: (i, 0)) for s in splits],
        out_shape=[jax.ShapeDtypeStruct((rows, s), dt) for s, dt in zip(splits, dtypes)],
        compiler_params=_cparams(("parallel",)),
        name="norm_proj",
    )(x, nw.reshape(1, d), w)


def _s5_prep_kernel(lr_ref, li_ref, ldt_ref, br_ref, bi_ref, apow_ref, bb_ref):
    lr = lr_ref[0]
    li = li_ref[0]
    dt = jnp.exp(ldt_ref[0])
    mag = jnp.exp(lr * dt)
    a_re = mag * jnp.cos(li * dt)
    a_im = mag * jnp.sin(li * dt)
    denom = lr * lr + li * li
    nr = a_re - 1.0
    z_re = (nr * lr + a_im * li) / denom
    z_im = (a_im * lr - nr * li) / denom
    br = br_ref[0]
    bi = bi_ref[0]
    nh = br.shape[0]
    bb_ref[0, :nh, :] = z_re * br - z_im * bi
    bb_ref[0, nh:, :] = z_re * bi + z_im * br
    pr, pi = a_re, a_im
    for j in range(SUBLANES):
        apow_ref[0, j:j + 1, :] = pr
        apow_ref[0, SUBLANES + j:SUBLANES + j + 1, :] = pi
        pr, pi = pr * a_re - pi * a_im, pr * a_im + pi * a_re


def s5_prep(lam_re, lam_im, log_dt, b_re, b_im):
    depth, g, p = lam_re.shape
    nh = b_re.shape[-1]
    gp = g * p
    lr = lam_re.reshape(depth, 1, gp)
    li = lam_im.reshape(depth, 1, gp)
    ldt = jnp.repeat(log_dt, p, axis=1).reshape(depth, 1, gp)
    br = b_re.transpose(0, 3, 1, 2).reshape(depth, nh, gp)
    bi = b_im.transpose(0, 3, 1, 2).reshape(depth, nh, gp)
    vec = pl.BlockSpec((1, 1, gp), lambda i: (i, 0, 0))
    mat = pl.BlockSpec((1, nh, gp), lambda i: (i, 0, 0))
    return pl.pallas_call(
        _s5_prep_kernel,
        grid=(depth,),
        in_specs=[vec, vec, vec, mat, mat],
        out_specs=[pl.BlockSpec((1, 2 * SUBLANES, gp), lambda i: (i, 0, 0)),
                   pl.BlockSpec((1, 2 * nh, gp), lambda i: (i, 0, 0))],
        out_shape=[jax.ShapeDtypeStruct((depth, 2 * SUBLANES, gp), F32),
                   jax.ShapeDtypeStruct((depth, 2 * nh, gp), F32)],
        compiler_params=_cparams(("arbitrary",)),
        name="s5_prep",
    )(lr, li, ldt, br, bi)


def _s5_kernel(u_ref, bb_ref, cc_ref, apow_ref, d_ref, wglu_ref, o_ref, xs_ref, carry_ref, *, gp):
    ts = u_ref.shape[0]
    nblk = ts // SUBLANES

    @pl.when(pl.program_id(1) == 0)
    def _():
        carry_ref[...] = jnp.zeros_like(carry_ref)

    u = u_ref[...]
    bu = _dot(u.astype(BF16), bb_ref[...])
    xr = bu[:, :gp].reshape(nblk, SUBLANES, gp)
    xi = bu[:, gp:].reshape(nblk, SUBLANES, gp)
    row = lax.broadcasted_iota(jnp.int32, (1, SUBLANES, 1), 1)
    for sh in (1, 2, 4):
        ar = apow_ref[sh - 1:sh, :].reshape(1, 1, gp)
        ai = apow_ref[SUBLANES + sh - 1:SUBLANES + sh, :].reshape(1, 1, gp)
        keep = row >= sh
        sr = jnp.where(keep, pltpu.roll(xr, sh, axis=1), 0.0)
        si = jnp.where(keep, pltpu.roll(xi, sh, axis=1), 0.0)
        xr, xi = xr + ar * sr - ai * si, xi + ar * si + ai * sr
    xs_ref[:, :gp] = xr.reshape(ts, gp)
    xs_ref[:, gp:] = xi.reshape(ts, gp)
    pr = apow_ref[:SUBLANES, :]
    pi = apow_ref[SUBLANES:, :]

    def body(r, carry):
        cr, ci = carry
        off = pl.multiple_of(r * SUBLANES, SUBLANES)
        lr = xs_ref[pl.ds(off, SUBLANES), :gp]
        li = xs_ref[pl.ds(off, SUBLANES), gp:]
        nr = lr + pr * cr - pi * ci
        ni = li + pr * ci + pi * cr
        xs_ref[pl.ds(off, SUBLANES), :gp] = nr
        xs_ref[pl.ds(off, SUBLANES), gp:] = ni
        return nr[SUBLANES - 1:, :], ni[SUBLANES - 1:, :]

    cr, ci = lax.fori_loop(0, nblk, body, (carry_ref[0:1, :], carry_ref[1:2, :]))
    carry_ref[0:1, :] = cr
    carry_ref[1:2, :] = ci

    y = _dot(xs_ref[...].astype(BF16), cc_ref[...]) + d_ref[...] * u
    g = jax.nn.gelu(y)
    o_ref[...] = g * jax.nn.sigmoid(_dot(g.astype(BF16), wglu_ref[...]))


def s5_mixer(p_a, bb, cc, apow, d_skip, w_glu, batch, seq, ts=256):
    w = d_skip.shape[-1]
    gp = apow.shape[-1]
    nt = seq // ts
    return pl.pallas_call(
        functools.partial(_s5_kernel, gp=gp),
        grid=(batch, nt),
        in_specs=[
            pl.BlockSpec((ts, w), lambda b, t: (b * nt + t, 0)),
            pl.BlockSpec((w, 2 * gp), lambda b, t: (0, 0)),
            pl.BlockSpec((2 * gp, w), lambda b, t: (0, 0)),
            pl.BlockSpec((2 * SUBLANES, gp), lambda b, t: (0, 0)),
            pl.BlockSpec((1, w), lambda b, t: (0, 0)),
            pl.BlockSpec((w, w), lambda b, t: (0, 0)),
        ],
        out_specs=pl.BlockSpec((ts, w), lambda b, t: (b * nt + t, 0)),
        out_shape=jax.ShapeDtypeStruct((batch * seq, w), F32),
        scratch_shapes=[pltpu.VMEM((ts, 2 * gp), F32), pltpu.VMEM((2, gp), F32)],
        compiler_params=_cparams(("parallel", "arbitrary")),
        name="s5_mixer",
    )(p_a, bb, cc, apow, d_skip.reshape(1, w), w_glu)


def _block_diag_in(bb, g, p, nh):
    eye = jnp.eye(g, dtype=F32)
    def one(m):
        m = m.reshape(nh, g, p)
        return jnp.einsum('hgp,kg->khgp', m, eye).reshape(g * nh, g * p)
    return jnp.concatenate([one(bb[:nh]), one(bb[nh:])], axis=1)


def _block_diag_out(c_re, c_im):
    g, nh, p = c_re.shape
    eye = jnp.eye(g, dtype=F32)
    def one(m):
        return jnp.einsum('ghp,gk->gpkh', m, eye).reshape(g * p, g * nh)
    return jnp.concatenate([one(c_re), one(-c_im)], axis=0)


def _ret_tables(seq, chunk, heads, dh):
    half = dh // 2
    w = heads * dh
    hw = w // 2
    inv_freq = ROPE_BASE ** (-np.arange(half, dtype=np.float64) / half)
    ang = np.arange(seq, dtype=np.float64)[:, None] * np.tile(inv_freq, heads)[None, :]
    cos = np.cos(ang).astype(np.float32)
    sin = np.sin(ang).astype(np.float32)
    log_gamma = np.log1p(-(2.0 ** (-5.0 - np.arange(heads, dtype=np.float64))))
    idx = np.arange(chunk, dtype=np.float64)
    rel = idx[:, None] - idx[None, :]
    intra = np.where(rel >= 0, np.exp(log_gamma[:, None, None] * np.maximum(rel, 0.0)), 0.0)
    head_nat = np.arange(w) // dh
    head_rot = (np.arange(w) % hw) // half
    xi = np.exp(log_gamma[head_nat][None, :] * (idx[:, None] + 1.0))
    zeta = np.exp(log_gamma[head_rot][None, :] * (chunk - 1.0 - idx[:, None]))
    decay = np.exp(log_gamma[head_nat] * chunk)[None, :]
    same = (head_rot[:, None] == head_nat[None, :]).astype(np.float32)
    gavg = (head_nat[:, None] == head_nat[None, :]).astype(np.float32) / dh
    f = lambda a: jnp.asarray(a, dtype=F32)
    return dict(cos=f(cos), sin=f(sin), intra=f(intra), xi=f(xi), zeta=f(zeta), decay=f(decay),
                same=f(same), gavg=jnp.asarray(gavg, dtype=BF16))


def _ret_kernel(q_ref, k_ref, v_ref, g_ref, cos_ref, sin_ref, intra_ref, xi_ref, zeta_ref,
                decay_ref, same_ref, gavg_ref, gnw_ref, o_ref, state_ref, *, heads, dh):
    w = heads * dh
    hw = w // 2
    half = dh // 2

    @pl.when(pl.program_id(1) == 0)
    def _():
        state_ref[...] = jnp.zeros_like(state_ref)

    cos = cos_ref[...]
    sin = sin_ref[...]

    def rot(x):
        x1, x2 = x[:, :hw], x[:, hw:]
        return jnp.concatenate([x1 * cos - x2 * sin, x1 * sin + x2 * cos], axis=1)

    q = rot(q_ref[...])
    k = rot(k_ref[...]) * (dh ** -0.5)
    v = v_ref[...]
    qb = q.astype(BF16)
    kb = k.astype(BF16)
    vb = v.astype(BF16)
    lane = lax.broadcasted_iota(jnp.int32, (1, w), 1)
    head_rot = (lane % hw) // half
    head_nat = lane // dh

    o = _dot(qb, state_ref[...].astype(BF16)) * xi_ref[...]
    for h in range(heads):
        qh = jnp.where(head_rot == h, qb, jnp.zeros_like(qb))
        vh = jnp.where(head_nat == h, vb, jnp.zeros_like(vb))
        s = _dot_nt(qh, kb) * intra_ref[h]
        o = o + _dot(s.astype(BF16), vh)
    kz = (k * zeta_ref[...]).astype(BF16)
    state_ref[...] = decay_ref[...] * state_ref[...] + _dot_tn(kz, vb) * same_ref[...]

    gavg = gavg_ref[...]
    mu = _split_dot(o, gavg)
    d = o - mu
    var = _split_dot(d * d, gavg)
    on = d * lax.rsqrt(var + GN_EPS) * gnw_ref[...]
    o_ref[...] = jax.nn.silu(g_ref[...]) * on


def retention_mixer(p_a, col0, gn_w, batch, seq, heads=RET_HEADS, chunk=256):
    w = gn_w.shape[-1]
    dh = w // heads
    nc = seq // chunk
    tb = _ret_tables(seq, chunk, heads, dh)
    cb = col0 // w
    assert col0 % w == 0
    row = lambda j: pl.BlockSpec((chunk, w), lambda b, c, j=j: (b * nc + c, cb + j))
    const = lambda shape: pl.BlockSpec(shape, lambda b, c: (0,) * len(shape))
    return pl.pallas_call(
        functools.partial(_ret_kernel, heads=heads, dh=dh),
        grid=(batch, nc),
        in_specs=[row(0), row(1), row(2), row(3),
                  pl.BlockSpec((chunk, w // 2), lambda b, c: (c, 0)),
                  pl.BlockSpec((chunk, w // 2), lambda b, c: (c, 0)),
                  const((heads, chunk, chunk)), const((chunk, w)), const((chunk, w)),
                  const((1, w)), const((w, w)), const((w, w)), const((1, w))],
        out_specs=pl.BlockSpec((chunk, w), lambda b, c: (b * nc + c, 0)),
        out_shape=jax.ShapeDtypeStruct((batch * seq, w), F32),
        scratch_shapes=[pltpu.VMEM((w, w), F32)],
        compiler_params=_cparams(("parallel", "arbitrary")),
        name="retention_mixer",
    )(p_a, p_a, p_a, p_a, tb["cos"], tb["sin"], tb["intra"], tb["xi"], tb["zeta"], tb["decay"],
      tb["same"], tb["gavg"], gn_w.reshape(1, w))


def _rot_perm(heads, dh):
    half = dh // 2
    first = (np.arange(heads)[:, None] * dh + np.arange(half)[None, :]).reshape(-1)
    return np.concatenate([first, first + half])


def _sb_kernel(q_ref, k_ref, v_ref, o_ref, *, dh, blk):
    i = pl.program_id(2)
    q2 = q_ref[...]
    lane = lax.broadcasted_iota(jnp.int32, (1, 2 * dh), 1)
    r_id = lax.broadcasted_iota(jnp.int32, (blk, blk), 0)
    c_id = lax.broadcasted_iota(jnp.int32, (blk, blk), 1)
    upper = jnp.where(r_id > c_id, 1.0, 0.0).astype(BF16)
    causal = c_id < r_id
    scale = dh ** -0.5

    def pair(qm, hsel, j, run, diag):
        off = pl.multiple_of(j * blk, blk)
        kb = k_ref[pl.ds(off, blk), :]
        vb = v_ref[pl.ds(off, blk), :]
        vb = jnp.where(hsel, vb, jnp.zeros_like(vb))
        z = _dot_nt(qm, kb) * scale
        l1p = jnp.log(1.0 + jnp.exp(-jnp.abs(z)))
        log_fail = -(jnp.maximum(z, 0.0) + l1p)
        log_beta = jnp.minimum(z, 0.0) - l1p
        if diag:
            log_fail = jnp.where(causal, log_fail, 0.0)
        after = _dot(log_fail.astype(BF16), upper) + run
        wgt = jnp.exp(log_beta + after)
        if diag:
            wgt = jnp.where(causal, wgt, 0.0)
        contrib = _dot(wgt.astype(BF16), vb)
        return contrib, run + jnp.sum(log_fail, axis=1, keepdims=True)

    acc = jnp.zeros((blk, 2 * dh), F32)
    for hh in range(2):
        hsel = (lane // dh) == hh
        qm = jnp.where(hsel, q2, jnp.zeros_like(q2))
        c0, run0 = pair(qm, hsel, i, jnp.zeros((blk, 1), F32), True)

        def body(jj, carry, qm=qm, hsel=hsel):
            a, run = carry
            c, run = pair(qm, hsel, i - jj, run, False)
            return a + c, run

        acc, _ = lax.fori_loop(1, i + 1, body, (acc + c0, run0))
    o_ref[...] = acc


def stick_breaking_mixer(p_sb, batch, seq, heads=SB_HEADS, blk=256):
    w = p_sb.shape[1] // 3
    dh = w // heads
    pw = 2 * dh
    assert pw == LANES
    npair = heads // 2
    nq = seq // blk
    return pl.pallas_call(
        functools.partial(_sb_kernel, dh=dh, blk=blk),
        grid=(batch, npair, nq),
        in_specs=[
            pl.BlockSpec((blk, pw), lambda b, p, i: (b * nq + i, p)),
            pl.BlockSpec((seq, pw), lambda b, p, i: (b, npair + p)),
            pl.BlockSpec((seq, pw), lambda b, p, i: (b, 2 * npair + p)),
        ],
        out_specs=pl.BlockSpec((blk, pw), lambda b, p, i: (b * nq + i, p)),
        out_shape=jax.ShapeDtypeStruct((batch * seq, w), F32),
        compiler_params=_cparams(("parallel", "parallel", "arbitrary")),
        name="stick_breaking_mixer",
    )(p_sb, p_sb, p_sb)


def _mix_cross_kernel(h_ref, ya_ref, yb_ref, yc_ref, gain_ref, wout_ref, cnw_ref, wq_ref, kv_ref,
                      wo_ref, o_ref, *, heads):
    wa = ya_ref.shape[1]
    wb = yb_ref.shape[1]
    d = h_ref.shape[1]
    gain = gain_ref[...]
    ya = _rms(ya_ref[...], gain[:, :wa]).astype(BF16)
    yb = yb_ref[...].astype(BF16)
    yc = _rms(yc_ref[...], gain[:, wa + wb:]).astype(BF16)
    h1 = (h_ref[...] + _dot(ya, wout_ref[:wa, :]) + _dot(yb, wout_ref[wa:wa + wb, :])
          + _dot(yc, wout_ref[wa + wb:, :]))
    q = _dot(_rms(h1, cnw_ref[...]).astype(BF16), wq_ref[...])
    dh = d // heads
    outs = []
    for hd in range(heads):
        qh = q[:, hd * dh:(hd + 1) * dh].astype(BF16)
        kh = kv_ref[:, hd * dh:(hd + 1) * dh]
        vh = kv_ref[:, d + hd * dh:d + (hd + 1) * dh]
        s = _dot_nt(qh, kh) * (dh ** -0.5)
        s = s - jnp.max(s, axis=-1, keepdims=True)
        e = jnp.exp(s)
        p = e / jnp.sum(e, axis=-1, keepdims=True)
        outs.append(_dot(p.astype(BF16), vh).astype(BF16))
    o_ref[...] = h1 + _dot(jnp.concatenate(outs, axis=1), wo_ref[...])


def mix_cross(h, ya, yb, yc, gain, w_out, cnw, wq, kv, wo, batch, seq, heads=CROSS_HEADS, tm=512):
    t, d = h.shape
    m = kv.shape[0] // batch
    nt = seq // tm
    rows = lambda wd: pl.BlockSpec((tm, wd), lambda i: (i, 0))
    const = lambda shape: pl.BlockSpec(shape, lambda i: (0, 0))
    return pl.pallas_call(
        functools.partial(_mix_cross_kernel, heads=heads),
        grid=(t // tm,),
        in_specs=[rows(d), rows(ya.shape[1]), rows(yb.shape[1]), rows(yc.shape[1]),
                  const((1, d)), const((d, d)), const((1, d)), const((d, d)),
                  pl.BlockSpec((m, 2 * d), lambda i: (i // nt, 0)),
                  const((d, d))],
        out_specs=rows(d),
        out_shape=jax.ShapeDtypeStruct((t, d), F32),
        compiler_params=_cparams(("parallel",)),
        name="mix_cross",
    )(h, ya, yb, yc, gain.reshape(1, d), w_out, cnw.reshape(1, d), wq, kv, wo)


def _ffn_dense_kernel(h_ref, nw_ref, wg_ref, wu_ref, wd_ref, o_ref, xn_ref):
    f = pl.program_id(1)

    @pl.when(f == 0)
    def _():
        h = h_ref[...]
        xn_ref[...] = _rms(h, nw_ref[...]).astype(BF16)
        o_ref[...] = h

    xn = xn_ref[...]
    act = jax.nn.silu(_dot(xn, wg_ref[...])) * _dot(xn, wu_ref[...])
    o_ref[...] += _dot(act.astype(BF16), wd_ref[...])


def ffn_dense(h, nw, w_gu, w_down, tm=1024, tf=256):
    t, d = h.shape
    ff = w_down.shape[0]
    nf = ff // tf
    return pl.pallas_call(
        _ffn_dense_kernel,
        grid=(t // tm, nf),
        in_specs=[
            pl.BlockSpec((tm, d), lambda i, f: (i, 0)),
            pl.BlockSpec((1, d), lambda i, f: (0, 0)),
            pl.BlockSpec((d, tf), lambda i, f: (0, f)),
            pl.BlockSpec((d, tf), lambda i, f: (0, nf + f)),
            pl.BlockSpec((tf, d), lambda i, f: (f, 0)),
        ],
        out_specs=pl.BlockSpec((tm, d), lambda i, f: (i, 0)),
        out_shape=jax.ShapeDtypeStruct((t, d), F32),
        scratch_shapes=[pltpu.VMEM((tm, d), BF16)],
        compiler_params=_cparams(("parallel", "arbitrary")),
        name="ffn_dense",
    )(h, nw.reshape(1, d), w_gu, w_gu, w_down)


def _ffn_group_kernel(te_ref, nv_ref, x_ref, gate_ref, wg_ref, wu_ref, wd_ref, o_ref):
    i = pl.program_id(0)
    f = pl.program_id(1)
    valid = i < nv_ref[0]

    @pl.when(f == 0)
    def _():
        o_ref[...] = jnp.zeros_like(o_ref)

    @pl.when(valid)
    def _():
        x = x_ref[...]
        act = jax.nn.silu(_dot(x, wg_ref[0])) * _dot(x, wu_ref[0])
        o_ref[...] += _dot(act.astype(BF16), wd_ref[0])

    @pl.when(jnp.logical_and(valid, f == pl.num_programs(1) - 1))
    def _():
        o_ref[...] = o_ref[...] * gate_ref[...]


def ffn_grouped(xs, row_gate, tile_expert, n_valid, w_gu, w_down, tm=1024, tf=256):
    p, d = xs.shape
    ff = w_down.shape[1]
    nf = ff // tf

    def live(i, f, nv):
        ok = i < nv[0]
        return jnp.where(ok, i, nv[0] - 1), jnp.where(ok, f, nf - 1)

    def x_map(i, f, te, nv):
        ii, _ = live(i, f, nv)
        return ii, 0

    def wg_map(i, f, te, nv):
        ii, ff_ = live(i, f, nv)
        return te[ii], 0, ff_

    def wu_map(i, f, te, nv):
        ii, ff_ = live(i, f, nv)
        return te[ii], 0, nf + ff_

    def wd_map(i, f, te, nv):
        ii, ff_ = live(i, f, nv)
        return te[ii], ff_, 0

    grid_spec = pltpu.PrefetchScalarGridSpec(
        num_scalar_prefetch=2,
        grid=(p // tm, nf),
        in_specs=[
            pl.BlockSpec((tm, d), x_map),
            pl.BlockSpec((tm, 1), x_map),
            pl.BlockSpec((1, d, tf), wg_map),
            pl.BlockSpec((1, d, tf), wu_map),
            pl.BlockSpec((1, tf, d), wd_map),
        ],
        out_specs=pl.BlockSpec((tm, d), lambda i, f, te, nv: (i, 0)),
    )
    return pl.pallas_call(
        _ffn_group_kernel,
        grid_spec=grid_spec,
        out_shape=jax.ShapeDtypeStruct((p, d), F32),
        compiler_params=_cparams(("arbitrary", "arbitrary")),
        name="ffn_grouped",
    )(tile_expert, n_valid, xs, row_gate, w_gu, w_gu, w_down)


def _router_kernel(h_ref, nw_ref, wr_ref, xn_ref, slot_ref, gate_ref, *, n_exp):
    xn = _rms(h_ref[...], nw_ref[...])
    xn_ref[...] = xn.astype(BF16)
    xh = xn.astype(BF16)
    xl = (xn - xh.astype(F32)).astype(BF16)
    wr = wr_ref[...]
    wh = wr.astype(BF16)
    wl = (wr - wh.astype(F32)).astype(BF16)
    logits = _dot(xh, wh) + _dot(xh, wl) + _dot(xl, wh)
    lane = lax.broadcasted_iota(jnp.int32, logits.shape, 1)
    neg = -jnp.inf
    lg = jnp.where(lane < n_exp, logits, neg)
    m1 = jnp.max(lg, axis=-1, keepdims=True)
    i1 = jnp.min(jnp.where(lg == m1, lane, LANES), axis=-1, keepdims=True)
    lg2 = jnp.where(lane == i1, neg, lg)
    m2 = jnp.max(lg2, axis=-1, keepdims=True)
    i2 = jnp.min(jnp.where(lg2 == m2, lane, LANES), axis=-1, keepdims=True)
    e2 = jnp.exp(m2 - m1)
    g1 = 1.0 / (1.0 + e2)
    g2 = e2 / (1.0 + e2)
    slot = jnp.where(lane == i1, 1, jnp.where(lane == i2, 2, 0))
    gate = jnp.where(lane == i1, g1, jnp.where(lane == i2, g2, 0.0))
    slot_ref[...] = slot[:, :n_exp]
    gate_ref[...] = gate[:, :n_exp]


def router(h, nw, w_router, tm=512):
    t, d = h.shape
    n_exp = w_router.shape[1]
    wr = jnp.pad(w_router, ((0, 0), (0, LANES - n_exp)))
    return pl.pallas_call(
        functools.partial(_router_kernel, n_exp=n_exp),
        grid=(t // tm,),
        in_specs=[pl.BlockSpec((tm, d), lambda i: (i, 0)),
                  pl.BlockSpec((1, d), lambda i: (0, 0)),
                  pl.BlockSpec((d, LANES), lambda i: (0, 0))],
        out_specs=[pl.BlockSpec((tm, d), lambda i: (i, 0)),
                   pl.BlockSpec((tm, n_exp), lambda i: (i, 0)),
                   pl.BlockSpec((tm, n_exp), lambda i: (i, 0))],
        out_shape=[jax.ShapeDtypeStruct((t, d), BF16),
                   jax.ShapeDtypeStruct((t, n_exp), jnp.int32),
                   jax.ShapeDtypeStruct((t, n_exp), F32)],
        compiler_params=_cparams(("parallel",)),
        name="router",
    )(h, nw.reshape(1, d), wr)


def _dispatch_plan(slot, gate, tm):
    t, n_exp = slot.shape
    sel = (slot > 0).astype(jnp.int32)
    counts = jnp.sum(sel, axis=0)
    rank = jnp.cumsum(sel, axis=0) - sel
    tiles = (counts + tm - 1) // tm
    tile_end = jnp.cumsum(tiles)
    start = (tile_end - tiles) * tm
    pos = start[None, :] + rank
    n_rows = 2 * t + n_exp * tm
    n_tiles = n_rows // tm
    pos1 = jnp.sum(jnp.where(slot == 1, pos, 0), axis=1)
    pos2 = jnp.sum(jnp.where(slot == 2, pos, 0), axis=1)
    gate1 = jnp.sum(jnp.where(slot == 1, gate, 0.0), axis=1)
    gate2 = jnp.sum(jnp.where(slot == 2, gate, 0.0), axis=1)
    flat_pos = jnp.concatenate([pos1, pos2])
    tok = jnp.arange(t, dtype=jnp.int32)
    row_token = jnp.zeros((n_rows,), jnp.int32).at[flat_pos].set(jnp.concatenate([tok, tok]))
    row_gate = jnp.zeros((n_rows,), F32).at[flat_pos].set(jnp.concatenate([gate1, gate2]))
    tile_expert = jnp.minimum(
        jnp.searchsorted(tile_end, jnp.arange(n_tiles, dtype=jnp.int32), side="right"),
        n_exp - 1).astype(jnp.int32)
    n_valid = tile_end[-1:].astype(jnp.int32)
    return row_token, row_gate, tile_expert, n_valid, pos1, pos2


def _final_norm_kernel(h_ref, w_ref, o_ref):
    o_ref[...] = _rms(h_ref[...], w_ref[...])


def final_norm(h, w, tm=1024):
    t, d = h.shape
    return pl.pallas_call(
        _final_norm_kernel,
        grid=(t // tm,),
        in_specs=[pl.BlockSpec((tm, d), lambda i: (i, 0)), pl.BlockSpec((1, d), lambda i: (0, 0))],
        out_specs=pl.BlockSpec((tm, d), lambda i: (i, 0)),
        out_shape=jax.ShapeDtypeStruct((t, d), F32),
        compiler_params=_cparams(("parallel",)),
        name="final_norm",
    )(h, w.reshape(1, d))


def kernel(x, mem, mix_norm_w, w_in, ssm_lambda_re, ssm_lambda_im, ssm_b_re, ssm_b_im, ssm_c_re, ssm_c_im, ssm_d, ssm_log_dt, ssm_w_glu, mix_out_gain, w_out, cross_norm_w, mem_norm_w, w_cross_q, w_cross_kv, w_cross_o, ffn_norm_w, w_dense_gu, w_dense_down, w_router, w_expert_gu, w_expert_down, final_norm_w):
    batch, seq, d = x.shape
    depth = w_in.shape[0]
    mem_len = mem.shape[1]
    ssm_w = ssm_d.shape[1]
    g, p = ssm_lambda_re.shape[1:]
    nh = ssm_b_re.shape[-1]
    ret_w = ssm_w
    sb_w = d - ssm_w - ret_w
    ret_dh = ret_w // RET_HEADS
    moe_tm = 1024

    perm = _rot_perm(RET_HEADS, ret_dh)
    cols = np.arange(w_in.shape[2])
    cols[ssm_w:ssm_w + ret_w] = ssm_w + perm
    cols[ssm_w + ret_w:ssm_w + 2 * ret_w] = ssm_w + ret_w + perm
    w_in_b = w_in[:, :, cols].astype(BF16)
    n_a = ssm_w + 4 * ret_w

    apow, bbc = s5_prep(ssm_lambda_re, ssm_lambda_im, ssm_log_dt, ssm_b_re, ssm_b_im)

    mem2 = mem.reshape(batch * mem_len, d)
    h = x.reshape(batch * seq, d)
    for i in range(depth):
        p_a, p_sb = norm_proj(h, mix_norm_w[i], w_in_b[i], (n_a, 3 * sb_w), (F32, BF16))
        bb = _block_diag_in(bbc[i], g, p, nh).astype(BF16)
        cc = _block_diag_out(ssm_c_re[i], ssm_c_im[i]).astype(BF16)
        y_a = s5_mixer(p_a, bb, cc, apow[i], ssm_d[i], ssm_w_glu[i].astype(BF16), batch, seq)
        gain = mix_out_gain[i]
        y_b = retention_mixer(p_a, ssm_w, gain[ssm_w:ssm_w + ret_w], batch, seq)
        y_c = stick_breaking_mixer(p_sb, batch, seq)
        (kv,) = norm_proj(mem2, mem_norm_w, w_cross_kv[i].astype(BF16), (2 * d,), (BF16,))
        h = mix_cross(h, y_a, y_b, y_c, gain, w_out[i].astype(BF16), cross_norm_w[i],
                      w_cross_q[i].astype(BF16), kv, w_cross_o[i].astype(BF16), batch, seq)
        if i % 2 == 0:
            h = ffn_dense(h, ffn_norm_w[i], w_dense_gu[i // 2].astype(BF16),
                          w_dense_down[i // 2].astype(BF16))
        else:
            xn, slot, gate = router(h, ffn_norm_w[i], w_router[i // 2])
            wgu = w_expert_gu[i // 2].astype(BF16)
            wdn = w_expert_down[i // 2].astype(BF16)
            n_tiles = (batch * seq) // moe_tm
            for e in range(N_EXPERTS):
                y = ffn_grouped(xn, gate[:, e:e + 1], jnp.full((n_tiles,), e, jnp.int32),
                                jnp.full((1,), n_tiles, jnp.int32), wgu, wdn, tm=moe_tm)
                h = h + y
    return final_norm(h, final_norm_w).reshape(batch, seq, d)
```

```python
import functools
import math

import numpy as np
import jax
import jax.numpy as jnp
from jax import lax
from jax.experimental import pallas as pl
from jax.experimental.pallas import tpu as pltpu
from jax.experimental.pallas import tpu_sc as plsc

F32 = jnp.float32
BF16 = jnp.bfloat16

NORM_EPS = 1e-6
GN_EPS = 1e-6
ROPE_BASE = 10000.0
LOG2_E = 1.4426950408889634
SB_DEAD_LOG2 = -150.0

SSM_GROUP = 16
SSM_STATE = 64
RET_HEADS = 4
SB_HEADS = 8
CROSS_HEADS = 4
N_EXPERTS = 8

V7X_VMEM_BYTES = 64 * 1024 * 1024
LANES = 128
SUBLANES = 8


def _cparams(sem, vmem_mb=48):
    return pltpu.CompilerParams(dimension_semantics=sem, vmem_limit_bytes=vmem_mb * 1024 * 1024)


def _rms(x, w):
    ms = jnp.mean(x * x, axis=-1, keepdims=True)
    return x * lax.rsqrt(ms + NORM_EPS) * w


def _dot(a, b):
    return jnp.dot(a, b, preferred_element_type=F32)


def _dot_nt(a, b):
    return lax.dot_general(a, b, (((1,), (1,)), ((), ())), preferred_element_type=F32)


def _dot_tn(a, b):
    return lax.dot_general(a, b, (((0,), (0,)), ((), ())), preferred_element_type=F32)


def _split_dot(x, m_bf16):
    hi = x.astype(BF16)
    lo = (x - hi.astype(F32)).astype(BF16)
    return _dot(hi, m_bf16) + _dot(lo, m_bf16)


def _norm_proj_kernel(x_ref, nw_ref, w_ref, *o_refs, splits, chunk):
    xn = _rms(x_ref[...], nw_ref[...]).astype(BF16)
    col = 0
    for o_ref, width in zip(o_refs, splits):
        for c0 in range(0, width, chunk):
            r = _dot(xn, w_ref[:, col + c0:col + c0 + chunk])
            o_ref[:, c0:c0 + chunk] = r.astype(o_ref.dtype)
        col += width


def norm_proj(x, nw, w, splits, dtypes, tm=512, chunk=256):
    rows, d = x.shape
    n = w.shape[1]
    assert sum(splits) == n and rows % tm == 0
    return pl.pallas_call(
        functools.partial(_norm_proj_kernel, splits=splits, chunk=chunk),
        grid=(rows // tm,),
        in_specs=[
            pl.BlockSpec((tm, d), lambda i: (i, 0)),
            pl.BlockSpec((1, d), lambda i: (0, 0)),
            pl.BlockSpec((d, n), lambda i: (0, 0)),
        ],
        out_specs=[pl.BlockSpec((tm, s), lambda i: (i, 0)) for s in splits],
        out_shape=[jax.ShapeDtypeStruct((rows, s), dt) for s, dt in zip(splits, dtypes)],
        compiler_params=_cparams(("parallel",)),
        name="norm_proj",
    )(x, nw.reshape(1, d), w)


def _s5_prep_kernel(lr_ref, li_ref, ldt_ref, br_ref, bi_ref, apow_ref, bb_ref):
    lr = lr_ref[0]
    li = li_ref[0]
    dt = jnp.exp(ldt_ref[0])
    mag = jnp.exp(lr * dt)
    a_re = mag * jnp.cos(li * dt)
    a_im = mag * jnp.sin(li * dt)
    denom = lr * lr + li * li
    nr = a_re - 1.0
    z_re = (nr * lr + a_im * li) / denom
    z_im = (a_im * lr - nr * li) / denom
    br = br_ref[0]
    bi = bi_ref[0]
    nh = br.shape[0]
    bb_ref[0, :nh, :] = z_re * br - z_im * bi
    bb_ref[0, nh:, :] = z_re * bi + z_im * br
    pr, pi = a_re, a_im
    for j in range(SUBLANES):
        apow_ref[0, j:j + 1, :] = pr
        apow_ref[0, SUBLANES + j:SUBLANES + j + 1, :] = pi
        pr, pi = pr * a_re - pi * a_im, pr * a_im + pi * a_re


def s5_prep(lam_re, lam_im, log_dt, b_re, b_im):
    depth, g, p = lam_re.shape
    nh = b_re.shape[-1]
    gp = g * p
    lr = lam_re.reshape(depth, 1, gp)
    li = lam_im.reshape(depth, 1, gp)
    ldt = jnp.repeat(log_dt, p, axis=1).reshape(depth, 1, gp)
    br = b_re.transpose(0, 3, 1, 2).reshape(depth, nh, gp)
    bi = b_im.transpose(0, 3, 1, 2).reshape(depth, nh, gp)
    vec = pl.BlockSpec((1, 1, gp), lambda i: (i, 0, 0))
    mat = pl.BlockSpec((1, nh, gp), lambda i: (i, 0, 0))
    return pl.pallas_call(
        _s5_prep_kernel,
        grid=(depth,),
        in_specs=[vec, vec, vec, mat, mat],
        out_specs=[pl.BlockSpec((1, 2 * SUBLANES, gp), lambda i: (i, 0, 0)),
                   pl.BlockSpec((1, 2 * nh, gp), lambda i: (i, 0, 0))],
        out_shape=[jax.ShapeDtypeStruct((depth, 2 * SUBLANES, gp), F32),
                   jax.ShapeDtypeStruct((depth, 2 * nh, gp), F32)],
        compiler_params=_cparams(("arbitrary",)),
        name="s5_prep",
    )(lr, li, ldt, br, bi)


def _s5_kernel(u_ref, bb_ref, cc_ref, apow_ref, d_ref, wglu_ref, o_ref, xs_ref, carry_ref, *, gp):
    ts = u_ref.shape[0]
    nblk = ts // SUBLANES

    @pl.when(pl.program_id(1) == 0)
    def _():
        carry_ref[...] = jnp.zeros_like(carry_ref)

    u = u_ref[...]
    bu = _dot(u.astype(BF16), bb_ref[...])
    xr = bu[:, :gp].reshape(nblk, SUBLANES, gp)
    xi = bu[:, gp:].reshape(nblk, SUBLANES, gp)
    row = lax.broadcasted_iota(jnp.int32, (1, SUBLANES, 1), 1)
    for sh in (1, 2, 4):
        ar = apow_ref[sh - 1:sh, :].reshape(1, 1, gp)
        ai = apow_ref[SUBLANES + sh - 1:SUBLANES + sh, :].reshape(1, 1, gp)
        keep = row >= sh
        sr = jnp.where(keep, pltpu.roll(xr, sh, axis=1), 0.0)
        si = jnp.where(keep, pltpu.roll(xi, sh, axis=1), 0.0)
        xr, xi = xr + ar * sr - ai * si, xi + ar * si + ai * sr
    xs_ref[:, :gp] = xr.reshape(ts, gp)
    xs_ref[:, gp:] = xi.reshape(ts, gp)
    pr = apow_ref[:SUBLANES, :]
    pi = apow_ref[SUBLANES:, :]

    def body(r, carry):
        cr, ci = carry
        off = pl.multiple_of(r * SUBLANES, SUBLANES)
        lr = xs_ref[pl.ds(off, SUBLANES), :gp]
        li = xs_ref[pl.ds(off, SUBLANES), gp:]
        nr = lr + pr * cr - pi * ci
        ni = li + pr * ci + pi * cr
        xs_ref[pl.ds(off, SUBLANES), :gp] = nr
        xs_ref[pl.ds(off, SUBLANES), gp:] = ni
        return nr[SUBLANES - 1:, :], ni[SUBLANES - 1:, :]

    cr, ci = lax.fori_loop(0, nblk, body, (carry_ref[0:1, :], carry_ref[1:2, :]))
    carry_ref[0:1, :] = cr
    carry_ref[1:2, :] = ci

    y = _dot(xs_ref[...].astype(BF16), cc_ref[...]) + d_ref[...] * u
    g = jax.nn.gelu(y)
    o_ref[...] = g * jax.nn.sigmoid(_dot(g.astype(BF16), wglu_ref[...]))


def s5_mixer(p_a, bb, cc, apow, d_skip, w_glu, batch, seq, ts=256):
    w = d_skip.shape[-1]
    gp = apow.shape[-1]
    nt = seq // ts
    return pl.pallas_call(
        functools.partial(_s5_kernel, gp=gp),
        grid=(batch, nt),
        in_specs=[
            pl.BlockSpec((ts, w), lambda b, t: (b * nt + t, 0)),
            pl.BlockSpec((w, 2 * gp), lambda b, t: (0, 0)),
            pl.BlockSpec((2 * gp, w), lambda b, t: (0, 0)),
            pl.BlockSpec((2 * SUBLANES, gp), lambda b, t: (0, 0)),
            pl.BlockSpec((1, w), lambda b, t: (0, 0)),
            pl.BlockSpec((w, w), lambda b, t: (0, 0)),
        ],
        out_specs=pl.BlockSpec((ts, w), lambda b, t: (b * nt + t, 0)),
        out_shape=jax.ShapeDtypeStruct((batch * seq, w), F32),
        scratch_shapes=[pltpu.VMEM((ts, 2 * gp), F32), pltpu.VMEM((2, gp), F32)],
        compiler_params=_cparams(("parallel", "arbitrary")),
        name="s5_mixer",
    )(p_a, bb, cc, apow, d_skip.reshape(1, w), w_glu)


def _block_diag_in(bb, g, p, nh):
    eye = jnp.eye(g, dtype=F32)
    def one(m):
        m = m.reshape(nh, g, p)
        return jnp.einsum('hgp,kg->khgp', m, eye).reshape(g * nh, g * p)
    return jnp.concatenate([one(bb[:nh]), one(bb[nh:])], axis=1)


def _block_diag_out(c_re, c_im):
    g, nh, p = c_re.shape
    eye = jnp.eye(g, dtype=F32)
    def one(m):
        return jnp.einsum('ghp,gk->gpkh', m, eye).reshape(g * p, g * nh)
    return jnp.concatenate([one(c_re), one(-c_im)], axis=0)


def _ret_tables(seq, chunk, heads, dh):
    half = dh // 2
    w = heads * dh
    hw = w // 2
    inv_freq = ROPE_BASE ** (-np.arange(half, dtype=np.float64) / half)
    ang = np.arange(seq, dtype=np.float64)[:, None] * np.tile(inv_freq, heads)[None, :]
    cos = np.cos(ang).astype(np.float32)
    sin = np.sin(ang).astype(np.float32)
    log_gamma = np.log1p(-(2.0 ** (-5.0 - np.arange(heads, dtype=np.float64))))
    idx = np.arange(chunk, dtype=np.float64)
    rel = idx[:, None] - idx[None, :]
    intra = np.where(rel >= 0, np.exp(log_gamma[:, None, None] * np.maximum(rel, 0.0)), 0.0)
    head_nat = np.arange(w) // dh
    head_rot = (np.arange(w) % hw) // half
    xi = np.exp(log_gamma[head_nat][None, :] * (idx[:, None] + 1.0))
    zeta = np.exp(log_gamma[head_rot][None, :] * (chunk - 1.0 - idx[:, None]))
    decay = np.exp(log_gamma[head_nat] * chunk)[None, :]
    same = (head_rot[:, None] == head_nat[None, :]).astype(np.float32)
    gavg = (head_nat[:, None] == head_nat[None, :]).astype(np.float32) / dh
    f = lambda a: jnp.asarray(a, dtype=F32)
    return dict(cos=f(cos), sin=f(sin), intra=f(intra), xi=f(xi), zeta=f(zeta), decay=f(decay),
                same=f(same), gavg=jnp.asarray(gavg, dtype=BF16))


def _ret_kernel(q_ref, k_ref, v_ref, g_ref, cos_ref, sin_ref, intra_ref, xi_ref, zeta_ref,
                decay_ref, same_ref, gavg_ref, gnw_ref, o_ref, state_ref, *, heads, dh):
    w = heads * dh
    hw = w // 2
    half = dh // 2

    @pl.when(pl.program_id(1) == 0)
    def _():
        state_ref[...] = jnp.zeros_like(state_ref)

    cos = cos_ref[...]
    sin = sin_ref[...]

    def rot(x):
        x1, x2 = x[:, :hw], x[:, hw:]
        return jnp.concatenate([x1 * cos - x2 * sin, x1 * sin + x2 * cos], axis=1)

    q = rot(q_ref[...])
    k = rot(k_ref[...]) * (dh ** -0.5)
    v = v_ref[...]
    qb = q.astype(BF16)
    kb = k.astype(BF16)
    vb = v.astype(BF16)
    lane = lax.broadcasted_iota(jnp.int32, (1, w), 1)
    head_rot = (lane % hw) // half
    head_nat = lane // dh

    o = _dot(qb, state_ref[...].astype(BF16)) * xi_ref[...]
    for h in range(heads):
        qh = jnp.where(head_rot == h, qb, jnp.zeros_like(qb))
        vh = jnp.where(head_nat == h, vb, jnp.zeros_like(vb))
        s = _dot_nt(qh, kb) * intra_ref[h]
        o = o + _dot(s.astype(BF16), vh)
    kz = (k * zeta_ref[...]).astype(BF16)
    state_ref[...] = decay_ref[...] * state_ref[...] + _dot_tn(kz, vb) * same_ref[...]

    gavg = gavg_ref[...]
    mu = _split_dot(o, gavg)
    d = o - mu
    var = _split_dot(d * d, gavg)
    on = d * lax.rsqrt(var + GN_EPS) * gnw_ref[...]
    o_ref[...] = jax.nn.silu(g_ref[...]) * on


def retention_mixer(p_a, col0, gn_w, batch, seq, heads=RET_HEADS, chunk=256):
    w = gn_w.shape[-1]
    dh = w // heads
    nc = seq // chunk
    tb = _ret_tables(seq, chunk, heads, dh)
    cb = col0 // w
    assert col0 % w == 0
    row = lambda j: pl.BlockSpec((chunk, w), lambda b, c, j=j: (b * nc + c, cb + j))
    const = lambda shape: pl.BlockSpec(shape, lambda b, c: (0,) * len(shape))
    return pl.pallas_call(
        functools.partial(_ret_kernel, heads=heads, dh=dh),
        grid=(batch, nc),
        in_specs=[row(0), row(1), row(2), row(3),
                  pl.BlockSpec((chunk, w // 2), lambda b, c: (c, 0)),
                  pl.BlockSpec((chunk, w // 2), lambda b, c: (c, 0)),
                  const((heads, chunk, chunk)), const((chunk, w)), const((chunk, w)),
                  const((1, w)), const((w, w)), const((w, w)), const((1, w))],
        out_specs=pl.BlockSpec((chunk, w), lambda b, c: (b * nc + c, 0)),
        out_shape=jax.ShapeDtypeStruct((batch * seq, w), F32),
        scratch_shapes=[pltpu.VMEM((w, w), F32)],
        compiler_params=_cparams(("parallel", "arbitrary")),
        name="retention_mixer",
    )(p_a, p_a, p_a, p_a, tb["cos"], tb["sin"], tb["intra"], tb["xi"], tb["zeta"], tb["decay"],
      tb["same"], tb["gavg"], gn_w.reshape(1, w))


def _rot_perm(heads, dh):
    half = dh // 2
    first = (np.arange(heads)[:, None] * dh + np.arange(half)[None, :]).reshape(-1)
    return np.concatenate([first, first + half])


def _sb_kernel(q_ref, k_ref, v_ref, o_ref, *, dh, blk):
    i = pl.program_id(2)
    q2 = q_ref[...]
    lane = lax.broadcasted_iota(jnp.int32, (1, 2 * dh), 1)
    r_id = lax.broadcasted_iota(jnp.int32, (blk, blk), 0)
    c_id = lax.broadcasted_iota(jnp.int32, (blk, blk), 1)
    neg_upper = jnp.where(r_id > c_id, -1.0, 0.0).astype(BF16)
    causal = c_id < r_id
    zscale = (dh ** -0.5) * LOG2_E
    hsel = [(lane // dh) == hh for hh in range(2)]
    qm = [jnp.where(hs, q2, jnp.zeros_like(q2)) for hs in hsel]

    def pair(j, runs, diag):
        off = pl.multiple_of(j * blk, blk)
        kb = k_ref[pl.ds(off, blk), :]
        vb = v_ref[pl.ds(off, blk), :]
        wgts, vms, new_runs = [], [], []
        for hh in range(2):
            zs = _dot_nt(qm[hh], kb) * zscale
            nfail = jnp.maximum(zs, 0.0) + jnp.log2(1.0 + jnp.exp2(-jnp.abs(zs)))
            if diag:
                nfail = jnp.where(causal, nfail, 0.0)
            after = _dot(nfail.astype(BF16), neg_upper) + runs[hh]
            wgt = jnp.exp2((zs - nfail) + after)
            if diag:
                wgt = jnp.where(causal, wgt, 0.0)
            wgts.append(wgt.astype(BF16))
            vms.append(jnp.where(hsel[hh], vb, jnp.zeros_like(vb)))
            new_runs.append(runs[hh] - jnp.sum(nfail, axis=1, keepdims=True))
        contrib = _dot(jnp.concatenate(wgts, axis=1), jnp.concatenate(vms, axis=0))
        return contrib, new_runs

    zero = jnp.zeros((blk, 1), F32)
    acc0, (ra, rb) = pair(i, [zero, zero], True)

    def cond(carry):
        jj, _, ra, rb = carry
        alive = jnp.maximum(jnp.max(ra), jnp.max(rb)) > SB_DEAD_LOG2
        return jnp.logical_and(jj <= i, alive)

    def body(carry):
        jj, acc, ra, rb = carry
        c, (ra, rb) = pair(i - jj, [ra, rb], False)
        return jj + 1, acc + c, ra, rb

    _, acc, _, _ = lax.while_loop(cond, body, (jnp.int32(1), acc0, ra, rb))
    o_ref[...] = acc


def stick_breaking_mixer(p_sb, batch, seq, heads=SB_HEADS, blk=256):
    w = p_sb.shape[1] // 3
    dh = w // heads
    pw = 2 * dh
    assert pw == LANES
    npair = heads // 2
    nq = seq // blk
    return pl.pallas_call(
        functools.partial(_sb_kernel, dh=dh, blk=blk),
        grid=(batch, npair, nq),
        in_specs=[
            pl.BlockSpec((blk, pw), lambda b, p, i: (b * nq + i, p)),
            pl.BlockSpec((seq, pw), lambda b, p, i: (b, npair + p)),
            pl.BlockSpec((seq, pw), lambda b, p, i: (b, 2 * npair + p)),
        ],
        out_specs=pl.BlockSpec((blk, pw), lambda b, p, i: (b * nq + i, p)),
        out_shape=jax.ShapeDtypeStruct((batch * seq, w), F32),
        compiler_params=_cparams(("parallel", "parallel", "arbitrary")),
        name="stick_breaking_mixer",
    )(p_sb, p_sb, p_sb)


def _mix_cross_kernel(h_ref, ya_ref, yb_ref, yc_ref, gain_ref, wout_ref, cnw_ref, wq_ref, kv_ref,
                      wo_ref, o_ref, *, heads):
    wa = ya_ref.shape[1]
    wb = yb_ref.shape[1]
    d = h_ref.shape[1]
    gain = gain_ref[...]
    ya = _rms(ya_ref[...], gain[:, :wa]).astype(BF16)
    yb = yb_ref[...].astype(BF16)
    yc = _rms(yc_ref[...], gain[:, wa + wb:]).astype(BF16)
    h1 = (h_ref[...] + _dot(ya, wout_ref[:wa, :]) + _dot(yb, wout_ref[wa:wa + wb, :])
          + _dot(yc, wout_ref[wa + wb:, :]))
    q = _dot(_rms(h1, cnw_ref[...]).astype(BF16), wq_ref[...])
    dh = d // heads
    outs = []
    for hd in range(heads):
        qh = q[:, hd * dh:(hd + 1) * dh].astype(BF16)
        kh = kv_ref[:, hd * dh:(hd + 1) * dh]
        vh = kv_ref[:, d + hd * dh:d + (hd + 1) * dh]
        s = _dot_nt(qh, kh) * (dh ** -0.5)
        s = s - jnp.max(s, axis=-1, keepdims=True)
        e = jnp.exp(s)
        p = e / jnp.sum(e, axis=-1, keepdims=True)
        outs.append(_dot(p.astype(BF16), vh).astype(BF16))
    o_ref[...] = h1 + _dot(jnp.concatenate(outs, axis=1), wo_ref[...])


def mix_cross(h, ya, yb, yc, gain, w_out, cnw, wq, kv, wo, batch, seq, heads=CROSS_HEADS, tm=512):
    t, d = h.shape
    m = kv.shape[0] // batch
    nt = seq // tm
    rows = lambda wd: pl.BlockSpec((tm, wd), lambda i: (i, 0))
    const = lambda shape: pl.BlockSpec(shape, lambda i: (0, 0))
    return pl.pallas_call(
        functools.partial(_mix_cross_kernel, heads=heads),
        grid=(t // tm,),
        in_specs=[rows(d), rows(ya.shape[1]), rows(yb.shape[1]), rows(yc.shape[1]),
                  const((1, d)), const((d, d)), const((1, d)), const((d, d)),
                  pl.BlockSpec((m, 2 * d), lambda i: (i // nt, 0)),
                  const((d, d))],
        out_specs=rows(d),
        out_shape=jax.ShapeDtypeStruct((t, d), F32),
        compiler_params=_cparams(("parallel",)),
        name="mix_cross",
    )(h, ya, yb, yc, gain.reshape(1, d), w_out, cnw.reshape(1, d), wq, kv, wo)


def _ffn_dense_kernel(h_ref, nw_ref, wg_ref, wu_ref, wd_ref, o_ref, xn_ref):
    f = pl.program_id(1)

    @pl.when(f == 0)
    def _():
        h = h_ref[...]
        xn_ref[...] = _rms(h, nw_ref[...]).astype(BF16)
        o_ref[...] = h

    xn = xn_ref[...]
    act = jax.nn.silu(_dot(xn, wg_ref[...])) * _dot(xn, wu_ref[...])
    o_ref[...] += _dot(act.astype(BF16), wd_ref[...])


def ffn_dense(h, nw, w_gu, w_down, tm=1024, tf=256):
    t, d = h.shape
    ff = w_down.shape[0]
    nf = ff // tf
    return pl.pallas_call(
        _ffn_dense_kernel,
        grid=(t // tm, nf),
        in_specs=[
            pl.BlockSpec((tm, d), lambda i, f: (i, 0)),
            pl.BlockSpec((1, d), lambda i, f: (0, 0)),
            pl.BlockSpec((d, tf), lambda i, f: (0, f)),
            pl.BlockSpec((d, tf), lambda i, f: (0, nf + f)),
            pl.BlockSpec((tf, d), lambda i, f: (f, 0)),
        ],
        out_specs=pl.BlockSpec((tm, d), lambda i, f: (i, 0)),
        out_shape=jax.ShapeDtypeStruct((t, d), F32),
        scratch_shapes=[pltpu.VMEM((tm, d), BF16)],
        compiler_params=_cparams(("parallel", "arbitrary")),
        name="ffn_dense",
    )(h, nw.reshape(1, d), w_gu, w_gu, w_down)


def _ffn_group_kernel(te_ref, nv_ref, rows_ref, x_ref, wg_ref, wu_ref, wd_ref, o_ref, xb_ref):
    i = pl.program_id(0)
    f = pl.program_id(1)
    valid = i < nv_ref[0]

    @pl.when(f == 0)
    def _():
        row = lax.broadcasted_iota(jnp.int32, (x_ref.shape[0], 1), 0)
        xb_ref[...] = jnp.where(row < rows_ref[i], x_ref[...], 0.0).astype(BF16)
        o_ref[...] = jnp.zeros_like(o_ref)

    @pl.when(valid)
    def _():
        x = xb_ref[...]
        act = jax.nn.silu(_dot(x, wg_ref[0])) * _dot(x, wu_ref[0])
        o_ref[...] += _dot(act.astype(BF16), wd_ref[0])


def ffn_grouped(xs, tile_expert, n_valid, tile_rows, w_gu, w_down, tm=1024, tf=256):
    p, d = xs.shape
    ff = w_down.shape[1]
    nf = ff // tf

    def live(i, f, nv):
        ok = i < nv[0]
        return jnp.where(ok, i, nv[0] - 1), jnp.where(ok, f, nf - 1)

    def x_map(i, f, te, nv, rows):
        ii, _ = live(i, f, nv)
        return ii, 0

    def wg_map(i, f, te, nv, rows):
        ii, ff_ = live(i, f, nv)
        return te[ii], 0, ff_

    def wu_map(i, f, te, nv, rows):
        ii, ff_ = live(i, f, nv)
        return te[ii], 0, nf + ff_

    def wd_map(i, f, te, nv, rows):
        ii, ff_ = live(i, f, nv)
        return te[ii], ff_, 0

    grid_spec = pltpu.PrefetchScalarGridSpec(
        num_scalar_prefetch=3,
        grid=(p // tm, nf),
        in_specs=[
            pl.BlockSpec((tm, d), x_map),
            pl.BlockSpec((1, d, tf), wg_map),
            pl.BlockSpec((1, d, tf), wu_map),
            pl.BlockSpec((1, tf, d), wd_map),
        ],
        out_specs=pl.BlockSpec((tm, d), lambda i, f, te, nv, rows: (i, 0)),
        scratch_shapes=[pltpu.VMEM((tm, d), BF16)],
    )
    return pl.pallas_call(
        _ffn_group_kernel,
        grid_spec=grid_spec,
        out_shape=jax.ShapeDtypeStruct((p, d), F32),
        compiler_params=_cparams(("arbitrary", "arbitrary")),
        name="ffn_grouped",
    )(tile_expert, n_valid, tile_rows, xs, w_gu, w_gu, w_down)


V7X_SC_CORES = 2
V7X_SC_SUBCORES = 16
V7X_SC_WORKERS = V7X_SC_CORES * V7X_SC_SUBCORES


def _sc_mesh():
    return plsc.VectorSubcoreMesh(core_axis_name="c", subcore_axis_name="s")


def sc_scatter_rows2(x, idx1, idx2, n_out, ch=64):
    t, d = x.shape
    assert t % (V7X_SC_WORKERS * ch) == 0
    t_per_w = t // V7X_SC_WORKERS
    n_ch = t_per_w // ch

    @functools.partial(
        pl.kernel, mesh=_sc_mesh(), out_type=jax.ShapeDtypeStruct((n_out, d), x.dtype),
        scratch_types=[pltpu.VMEM((ch,), jnp.int32), pltpu.VMEM((ch,), jnp.int32),
                       pltpu.VMEM((ch, d), x.dtype), pltpu.SemaphoreType.DMA])
    def k(x_hbm, i1_hbm, i2_hbm, out_hbm, i1_v, i2_v, rows_v, sem):
        wid = lax.axis_index("s") * V7X_SC_CORES + lax.axis_index("c")
        base = wid * t_per_w

        @pl.loop(0, n_ch)
        def _(j):
            off = pl.multiple_of(base + j * ch, 8)
            pltpu.sync_copy(i1_hbm.at[pl.ds(off, ch)], i1_v)
            pltpu.sync_copy(i2_hbm.at[pl.ds(off, ch)], i2_v)
            pltpu.sync_copy(x_hbm.at[pl.ds(off, ch)], rows_v)
            pltpu.async_copy(rows_v, out_hbm.at[i1_v], sem).wait()
            pltpu.async_copy(rows_v, out_hbm.at[i2_v], sem).wait()

    return k(x, idx1, idx2)


def sc_gather_rows(table, idx, ch=64):
    _, d = table.shape
    b = idx.shape[0]
    assert b % (V7X_SC_WORKERS * ch) == 0
    b_per_w = b // V7X_SC_WORKERS
    n_ch = b_per_w // ch

    @functools.partial(
        pl.kernel, mesh=_sc_mesh(), out_type=jax.ShapeDtypeStruct((b, d), table.dtype),
        scratch_types=[pltpu.VMEM((ch,), jnp.int32), pltpu.VMEM((ch, d), table.dtype),
                       pltpu.SemaphoreType.DMA])
    def k(table_hbm, idx_hbm, out_hbm, idx_v, rows_v, sem):
        wid = lax.axis_index("s") * V7X_SC_CORES + lax.axis_index("c")
        base = wid * b_per_w

        @pl.loop(0, n_ch)
        def _(j):
            off = pl.multiple_of(base + j * ch, 8)
            pltpu.sync_copy(idx_hbm.at[pl.ds(off, ch)], idx_v)
            pltpu.async_copy(table_hbm.at[idx_v], rows_v, sem).wait()
            pltpu.sync_copy(rows_v, out_hbm.at[pl.ds(off, ch)])

    return k(table, idx)


def _moe_combine_kernel(h_ref, y1_ref, y2_ref, g1_ref, g2_ref, o_ref):
    o_ref[...] = h_ref[...] + g1_ref[...] * y1_ref[...] + g2_ref[...] * y2_ref[...]


def moe_combine(h, yg, gate1, gate2, tm=1024):
    t, d = h.shape
    nt = t // tm
    row = pl.BlockSpec((tm, d), lambda i: (i, 0))
    col = pl.BlockSpec((tm, 1), lambda i: (i, 0))
    return pl.pallas_call(
        _moe_combine_kernel,
        grid=(nt,),
        in_specs=[row, row, pl.BlockSpec((tm, d), lambda i: (nt + i, 0)), col, col],
        out_specs=row,
        out_shape=jax.ShapeDtypeStruct((t, d), F32),
        compiler_params=_cparams(("parallel",)),
        name="moe_combine",
    )(h, yg, yg, gate1, gate2)


def _router_kernel(h_ref, nw_ref, wr_ref, xn_ref, slot_ref, gate_ref, *, n_exp):
    xn = _rms(h_ref[...], nw_ref[...])
    xn_ref[...] = xn
    xh = xn.astype(BF16)
    xl = (xn - xh.astype(F32)).astype(BF16)
    wr = wr_ref[...]
    wh = wr.astype(BF16)
    wl = (wr - wh.astype(F32)).astype(BF16)
    logits = _dot(xh, wh) + _dot(xh, wl) + _dot(xl, wh)
    lane = lax.broadcasted_iota(jnp.int32, logits.shape, 1)
    neg = -jnp.inf
    lg = jnp.where(lane < n_exp, logits, neg)
    m1 = jnp.max(lg, axis=-1, keepdims=True)
    i1 = jnp.min(jnp.where(lg == m1, lane, LANES), axis=-1, keepdims=True)
    lg2 = jnp.where(lane == i1, neg, lg)
    m2 = jnp.max(lg2, axis=-1, keepdims=True)
    i2 = jnp.min(jnp.where(lg2 == m2, lane, LANES), axis=-1, keepdims=True)
    e2 = jnp.exp(m2 - m1)
    g1 = 1.0 / (1.0 + e2)
    g2 = e2 / (1.0 + e2)
    slot = jnp.where(lane == i1, 1, jnp.where(lane == i2, 2, 0))
    gate = jnp.where(lane == i1, g1, jnp.where(lane == i2, g2, 0.0))
    slot_ref[...] = slot[:, :n_exp]
    gate_ref[...] = gate[:, :n_exp]


def router(h, nw, w_router, tm=512):
    t, d = h.shape
    n_exp = w_router.shape[1]
    wr = jnp.pad(w_router, ((0, 0), (0, LANES - n_exp)))
    return pl.pallas_call(
        functools.partial(_router_kernel, n_exp=n_exp),
        grid=(t // tm,),
        in_specs=[pl.BlockSpec((tm, d), lambda i: (i, 0)),
                  pl.BlockSpec((1, d), lambda i: (0, 0)),
                  pl.BlockSpec((d, LANES), lambda i: (0, 0))],
        out_specs=[pl.BlockSpec((tm, d), lambda i: (i, 0)),
                   pl.BlockSpec((tm, n_exp), lambda i: (i, 0)),
                   pl.BlockSpec((tm, n_exp), lambda i: (i, 0))],
        out_shape=[jax.ShapeDtypeStruct((t, d), F32),
                   jax.ShapeDtypeStruct((t, n_exp), jnp.int32),
                   jax.ShapeDtypeStruct((t, n_exp), F32)],
        compiler_params=_cparams(("parallel",)),
        name="router",
    )(h, nw.reshape(1, d), wr)


def _dispatch_plan(slot, gate, tm):
    t, n_exp = slot.shape
    sel = (slot > 0).astype(jnp.int32)
    counts = jnp.sum(sel, axis=0)
    rank = jnp.cumsum(sel, axis=0) - sel
    tiles = (counts + tm - 1) // tm
    tile_end = jnp.cumsum(tiles)
    start = (tile_end - tiles) * tm
    pos = start[None, :] + rank
    n_rows = 2 * t + n_exp * tm
    n_tiles = n_rows // tm
    pos1 = jnp.sum(jnp.where(slot == 1, pos, 0), axis=1)
    pos2 = jnp.sum(jnp.where(slot == 2, pos, 0), axis=1)
    gate1 = jnp.sum(jnp.where(slot == 1, gate, 0.0), axis=1, keepdims=True)
    gate2 = jnp.sum(jnp.where(slot == 2, gate, 0.0), axis=1, keepdims=True)
    tile_id = jnp.arange(n_tiles, dtype=jnp.int32)
    owner = tile_id[:, None] >= tile_end[None, :]
    tile_expert = jnp.minimum(jnp.sum(owner.astype(jnp.int32), axis=1), n_exp - 1)
    mine = tile_expert[:, None] == jnp.arange(n_exp, dtype=jnp.int32)[None, :]
    group_end = jnp.sum(jnp.where(mine, (start + counts)[None, :], 0), axis=1)
    n_valid = tile_end[-1:].astype(jnp.int32)
    tile_rows = jnp.where(tile_id < n_valid[0], jnp.clip(group_end - tile_id * tm, 0, tm), 0)
    return (n_rows, pos1, pos2, gate1, gate2, tile_expert.astype(jnp.int32), n_valid,
            tile_rows.astype(jnp.int32))


def _final_norm_kernel(h_ref, w_ref, o_ref):
    o_ref[...] = _rms(h_ref[...], w_ref[...])


def final_norm(h, w, tm=1024):
    t, d = h.shape
    return pl.pallas_call(
        _final_norm_kernel,
        grid=(t // tm,),
        in_specs=[pl.BlockSpec((tm, d), lambda i: (i, 0)), pl.BlockSpec((1, d), lambda i: (0, 0))],
        out_specs=pl.BlockSpec((tm, d), lambda i: (i, 0)),
        out_shape=jax.ShapeDtypeStruct((t, d), F32),
        compiler_params=_cparams(("parallel",)),
        name="final_norm",
    )(h, w.reshape(1, d))


def kernel(x, mem, mix_norm_w, w_in, ssm_lambda_re, ssm_lambda_im, ssm_b_re, ssm_b_im, ssm_c_re, ssm_c_im, ssm_d, ssm_log_dt, ssm_w_glu, mix_out_gain, w_out, cross_norm_w, mem_norm_w, w_cross_q, w_cross_kv, w_cross_o, ffn_norm_w, w_dense_gu, w_dense_down, w_router, w_expert_gu, w_expert_down, final_norm_w):
    batch, seq, d = x.shape
    depth = w_in.shape[0]
    mem_len = mem.shape[1]
    ssm_w = ssm_d.shape[1]
    g, p = ssm_lambda_re.shape[1:]
    nh = ssm_b_re.shape[-1]
    ret_w = ssm_w
    sb_w = d - ssm_w - ret_w
    ret_dh = ret_w // RET_HEADS
    moe_tm = 1024

    perm = _rot_perm(RET_HEADS, ret_dh)
    cols = np.arange(w_in.shape[2])
    cols[ssm_w:ssm_w + ret_w] = ssm_w + perm
    cols[ssm_w + ret_w:ssm_w + 2 * ret_w] = ssm_w + ret_w + perm
    w_in_b = w_in[:, :, cols].astype(BF16)
    n_a = ssm_w + 4 * ret_w

    apow, bbc = s5_prep(ssm_lambda_re, ssm_lambda_im, ssm_log_dt, ssm_b_re, ssm_b_im)

    mem2 = mem.reshape(batch * mem_len, d)
    h = x.reshape(batch * seq, d)
    for i in range(depth):
        p_a, p_sb = norm_proj(h, mix_norm_w[i], w_in_b[i], (n_a, 3 * sb_w), (F32, BF16))
        bb = _block_diag_in(bbc[i], g, p, nh).astype(BF16)
        cc = _block_diag_out(ssm_c_re[i], ssm_c_im[i]).astype(BF16)
        y_a = s5_mixer(p_a, bb, cc, apow[i], ssm_d[i], ssm_w_glu[i].astype(BF16), batch, seq)
        gain = mix_out_gain[i]
        y_b = retention_mixer(p_a, ssm_w, gain[ssm_w:ssm_w + ret_w], batch, seq)
        y_c = stick_breaking_mixer(p_sb, batch, seq)
        (kv,) = norm_proj(mem2, mem_norm_w, w_cross_kv[i].astype(BF16), (2 * d,), (BF16,))
        h = mix_cross(h, y_a, y_b, y_c, gain, w_out[i].astype(BF16), cross_norm_w[i],
                      w_cross_q[i].astype(BF16), kv, w_cross_o[i].astype(BF16), batch, seq)
        if i % 2 == 0:
            h = ffn_dense(h, ffn_norm_w[i], w_dense_gu[i // 2].astype(BF16),
                          w_dense_down[i // 2].astype(BF16))
        else:
            xn, slot, gate = router(h, ffn_norm_w[i], w_router[i // 2])
            n_rows, pos1, pos2, gate1, gate2, tile_expert, n_valid, tile_rows = _dispatch_plan(
                slot, gate, moe_tm)
            xs = sc_scatter_rows2(xn, pos1, pos2, n_rows)
            y = ffn_grouped(xs, tile_expert, n_valid, tile_rows, w_expert_gu[i // 2].astype(BF16),
                            w_expert_down[i // 2].astype(BF16), tm=moe_tm)
            yg = sc_gather_rows(y, jnp.concatenate([pos1, pos2]))
            h = moe_combine(h, yg, gate1, gate2)
    return final_norm(h, final_norm_w).reshape(batch, seq, d)
```

```python
import functools
import math

import numpy as np
import jax
import jax.numpy as jnp
from jax import lax
from jax.experimental import pallas as pl
from jax.experimental.pallas import tpu as pltpu
from jax.experimental.pallas import tpu_sc as plsc

F32 = jnp.float32
BF16 = jnp.bfloat16

NORM_EPS = 1e-6
GN_EPS = 1e-6
ROPE_BASE = 10000.0
LOG2_E = 1.4426950408889634
SB_DEAD_LOG2 = -150.0
SB_MASKED_LOG2 = -1e30

SSM_GROUP = 16
SSM_STATE = 64
RET_HEADS = 4
SB_HEADS = 8
CROSS_HEADS = 4
N_EXPERTS = 8

V7X_VMEM_BYTES = 64 * 1024 * 1024
LANES = 128
SUBLANES = 8


def _cparams(sem, vmem_mb=48):
    return pltpu.CompilerParams(dimension_semantics=sem, vmem_limit_bytes=vmem_mb * 1024 * 1024)


def _rms(x, w):
    ms = jnp.mean(x * x, axis=-1, keepdims=True)
    return x * lax.rsqrt(ms + NORM_EPS) * w


def _dot(a, b):
    return jnp.dot(a, b, preferred_element_type=F32)


def _dot_nt(a, b):
    return lax.dot_general(a, b, (((1,), (1,)), ((), ())), preferred_element_type=F32)


def _dot_tn(a, b):
    return lax.dot_general(a, b, (((0,), (0,)), ((), ())), preferred_element_type=F32)


def _split_dot(x, m_bf16):
    hi = x.astype(BF16)
    lo = (x - hi.astype(F32)).astype(BF16)
    return _dot(hi, m_bf16) + _dot(lo, m_bf16)


def _norm_proj_kernel(x_ref, nw_ref, w_ref, *o_refs, splits, chunk):
    xn = _rms(x_ref[...], nw_ref[...]).astype(BF16)
    col = 0
    for o_ref, width in zip(o_refs, splits):
        for c0 in range(0, width, chunk):
            r = _dot(xn, w_ref[:, col + c0:col + c0 + chunk])
            o_ref[:, c0:c0 + chunk] = r.astype(o_ref.dtype)
        col += width


def norm_proj(x, nw, w, splits, dtypes, tm=512, chunk=256):
    rows, d = x.shape
    n = w.shape[1]
    assert sum(splits) == n and rows % tm == 0
    return pl.pallas_call(
        functools.partial(_norm_proj_kernel, splits=splits, chunk=chunk),
        grid=(rows // tm,),
        in_specs=[
            pl.BlockSpec((tm, d), lambda i: (i, 0)),
            pl.BlockSpec((1, d), lambda i: (0, 0)),
            pl.BlockSpec((d, n), lambda i: (0, 0)),
        ],
        out_specs=[pl.BlockSpec((tm, s), lambda i: (i, 0)) for s in splits],
        out_shape=[jax.ShapeDtypeStruct((rows, s), dt) for s, dt in zip(splits, dtypes)],
        compiler_params=_cparams(("parallel",)),
        name="norm_proj",
    )(x, nw.reshape(1, d), w)


def _s5_prep_kernel(lr_ref, li_ref, ldt_ref, br_ref, bi_ref, apow_ref, bb_ref):
    lr = lr_ref[0]
    li = li_ref[0]
    dt = jnp.exp(ldt_ref[0])
    mag = jnp.exp(lr * dt)
    a_re = mag * jnp.cos(li * dt)
    a_im = mag * jnp.sin(li * dt)
    denom = lr * lr + li * li
    nr = a_re - 1.0
    z_re = (nr * lr + a_im * li) / denom
    z_im = (a_im * lr - nr * li) / denom
    br = br_ref[0]
    bi = bi_ref[0]
    nh = br.shape[0]
    bb_ref[0, :nh, :] = z_re * br - z_im * bi
    bb_ref[0, nh:, :] = z_re * bi + z_im * br
    nstep = (apow_ref.shape[1] - 2 * S5_SEG_LEVELS) // 2
    pr, pi = a_re, a_im
    for j in range(nstep):
        apow_ref[0, j:j + 1, :] = pr
        apow_ref[0, nstep + j:nstep + j + 1, :] = pi
        if j + 1 < nstep:
            pr, pi = pr * a_re - pi * a_im, pr * a_im + pi * a_re
    for k in range(S5_SEG_LEVELS):
        apow_ref[0, 2 * nstep + k:2 * nstep + k + 1, :] = pr
        apow_ref[0, 2 * nstep + S5_SEG_LEVELS + k:2 * nstep + S5_SEG_LEVELS + k + 1, :] = pi
        pr, pi = pr * pr - pi * pi, 2.0 * pr * pi


S5_SEG_LEVELS = 3
S5_TIME_TILE = 256


def s5_prep(lam_re, lam_im, log_dt, b_re, b_im, nstep):
    depth, g, p = lam_re.shape
    nh = b_re.shape[-1]
    gp = g * p
    lr = lam_re.reshape(depth, 1, gp)
    li = lam_im.reshape(depth, 1, gp)
    ldt = jnp.repeat(log_dt, p, axis=1).reshape(depth, 1, gp)
    br = b_re.transpose(0, 3, 1, 2).reshape(depth, nh, gp)
    bi = b_im.transpose(0, 3, 1, 2).reshape(depth, nh, gp)
    vec = pl.BlockSpec((1, 1, gp), lambda i: (i, 0, 0))
    mat = pl.BlockSpec((1, nh, gp), lambda i: (i, 0, 0))
    npow = 2 * (nstep + S5_SEG_LEVELS)
    return pl.pallas_call(
        _s5_prep_kernel,
        grid=(depth,),
        in_specs=[vec, vec, vec, mat, mat],
        out_specs=[pl.BlockSpec((1, npow, gp), lambda i: (i, 0, 0)),
                   pl.BlockSpec((1, 2 * nh, gp), lambda i: (i, 0, 0))],
        out_shape=[jax.ShapeDtypeStruct((depth, npow, gp), F32),
                   jax.ShapeDtypeStruct((depth, 2 * nh, gp), F32)],
        compiler_params=_cparams(("arbitrary",)),
        name="s5_prep",
    )(lr, li, ldt, br, bi)


def _s5_kernel(u_ref, bb_ref, cc_ref, apow_ref, d_ref, wglu_ref, o_ref, xs_ref, carry_ref, perm_ref,
               *, gp):
    ts = u_ref.shape[0]
    nstep = ts // SUBLANES
    rows = lambda s: slice(s * SUBLANES, (s + 1) * SUBLANES)

    @pl.when(pl.program_id(1) == 0)
    def _():
        carry_ref[...] = jnp.zeros_like(carry_ref)

    ntile = u_ref.shape[1] // LANES
    for c in range(ntile):
        perm_ref[c] = u_ref[:, c * LANES:(c + 1) * LANES]
    u = jnp.concatenate(
        [jnp.concatenate([perm_ref[c, pl.ds(s, SUBLANES, stride=nstep), :] for c in range(ntile)], axis=1)
         for s in range(nstep)], axis=0)
    xs_ref[...] = _dot(u.astype(BF16), bb_ref[...])

    ar = apow_ref[0:1, :]
    ai = apow_ref[nstep:nstep + 1, :]
    xr = xs_ref[rows(0), :gp]
    xi = xs_ref[rows(0), gp:]
    for s in range(1, nstep):
        xr, xi = (ar * xr - ai * xi + xs_ref[rows(s), :gp], ar * xi + ai * xr + xs_ref[rows(s), gp:])
        xs_ref[rows(s), :gp] = xr
        xs_ref[rows(s), gp:] = xi

    seg = lax.broadcasted_iota(jnp.int32, (SUBLANES, 1), 0)
    cr = carry_ref[0:1, :]
    ci = carry_ref[1:2, :]
    lr = apow_ref[2 * nstep:2 * nstep + 1, :]
    li = apow_ref[2 * nstep + S5_SEG_LEVELS:2 * nstep + S5_SEG_LEVELS + 1, :]
    first = seg == 0
    xr = xr + jnp.where(first, lr * cr - li * ci, 0.0)
    xi = xi + jnp.where(first, lr * ci + li * cr, 0.0)
    for k in range(S5_SEG_LEVELS):
        sh = 1 << k
        pr = apow_ref[2 * nstep + k:2 * nstep + k + 1, :]
        pi = apow_ref[2 * nstep + S5_SEG_LEVELS + k:2 * nstep + S5_SEG_LEVELS + k + 1, :]
        keep = seg >= sh
        sr = jnp.where(keep, pltpu.roll(xr, sh, axis=0), 0.0)
        si = jnp.where(keep, pltpu.roll(xi, sh, axis=0), 0.0)
        xr, xi = xr + pr * sr - pi * si, xi + pr * si + pi * sr
    carry_ref[0:1, :] = xr[SUBLANES - 1:, :]
    carry_ref[1:2, :] = xi[SUBLANES - 1:, :]
    enter_r = jnp.where(first, cr, pltpu.roll(xr, 1, axis=0))
    enter_i = jnp.where(first, ci, pltpu.roll(xi, 1, axis=0))

    for s in range(nstep):
        pr = apow_ref[s:s + 1, :]
        pi = apow_ref[nstep + s:nstep + s + 1, :]
        xs_ref[rows(s), :gp] += pr * enter_r - pi * enter_i
        xs_ref[rows(s), gp:] += pr * enter_i + pi * enter_r

    y = _dot(xs_ref[...].astype(BF16), cc_ref[...]) + d_ref[...] * u
    g = jax.nn.gelu(y)
    out = g * jax.nn.sigmoid(_dot(g.astype(BF16), wglu_ref[...]))
    for s in range(nstep):
        for c in range(ntile):
            perm_ref[c, pl.ds(s, SUBLANES, stride=nstep), :] = out[rows(s), c * LANES:(c + 1) * LANES]
    for c in range(ntile):
        o_ref[:, c * LANES:(c + 1) * LANES] = perm_ref[c]


def s5_mixer(p_a, bb, cc, apow, d_skip, w_glu, batch, seq):
    w = d_skip.shape[-1]
    gp = apow.shape[-1]
    ts = (apow.shape[0] // 2 - S5_SEG_LEVELS) * SUBLANES
    nt = seq // ts
    return pl.pallas_call(
        functools.partial(_s5_kernel, gp=gp),
        grid=(batch, nt),
        in_specs=[
            pl.BlockSpec((ts, w), lambda b, t: (b * nt + t, 0)),
            pl.BlockSpec((w, 2 * gp), lambda b, t: (0, 0)),
            pl.BlockSpec((2 * gp, w), lambda b, t: (0, 0)),
            pl.BlockSpec(apow.shape, lambda b, t: (0, 0)),
            pl.BlockSpec((1, w), lambda b, t: (0, 0)),
            pl.BlockSpec((w, w), lambda b, t: (0, 0)),
        ],
        out_specs=pl.BlockSpec((ts, w), lambda b, t: (b * nt + t, 0)),
        out_shape=jax.ShapeDtypeStruct((batch * seq, w), F32),
        scratch_shapes=[pltpu.VMEM((ts, 2 * gp), F32), pltpu.VMEM((2, gp), F32),
                        pltpu.VMEM((w // LANES, ts, LANES), F32)],
        compiler_params=_cparams(("parallel", "arbitrary")),
        name="s5_mixer",
    )(p_a, bb, cc, apow, d_skip.reshape(1, w), w_glu)


def _block_diag_in(bb, g, p, nh):
    eye = jnp.eye(g, dtype=F32)
    def one(m):
        m = m.reshape(nh, g, p)
        return jnp.einsum('hgp,kg->khgp', m, eye).reshape(g * nh, g * p)
    return jnp.concatenate([one(bb[:nh]), one(bb[nh:])], axis=1)


def _block_diag_out(c_re, c_im):
    g, nh, p = c_re.shape
    eye = jnp.eye(g, dtype=F32)
    def one(m):
        return jnp.einsum('ghp,gk->gpkh', m, eye).reshape(g * p, g * nh)
    return jnp.concatenate([one(c_re), one(-c_im)], axis=0)


def _ret_tables(seq, chunk, heads, dh):
    half = dh // 2
    w = heads * dh
    hw = w // 2
    inv_freq = ROPE_BASE ** (-np.arange(half, dtype=np.float64) / half)
    ang = np.arange(seq, dtype=np.float64)[:, None] * np.tile(inv_freq, heads)[None, :]
    cos = np.cos(ang).astype(np.float32)
    sin = np.sin(ang).astype(np.float32)
    log_gamma = np.log1p(-(2.0 ** (-5.0 - np.arange(heads, dtype=np.float64))))
    idx = np.arange(chunk, dtype=np.float64)
    rel = idx[:, None] - idx[None, :]
    intra = np.where(rel >= 0, np.exp(log_gamma[:, None, None] * np.maximum(rel, 0.0)), 0.0)
    head_nat = np.arange(w) // dh
    head_rot = (np.arange(w) % hw) // half
    xi = np.exp(log_gamma[head_nat][None, :] * (idx[:, None] + 1.0))
    zeta = np.exp(log_gamma[head_rot][None, :] * (chunk - 1.0 - idx[:, None]))
    decay = np.exp(log_gamma[head_nat] * chunk)[None, :]
    same = (head_rot[:, None] == head_nat[None, :]).astype(np.float32)
    gavg = (head_nat[:, None] == head_nat[None, :]).astype(np.float32) / dh
    f = lambda a: jnp.asarray(a, dtype=F32)
    return dict(cos=f(cos), sin=f(sin), intra=f(intra), xi=f(xi), zeta=f(zeta), decay=f(decay),
                same=f(same), gavg=jnp.asarray(gavg, dtype=BF16))


def _ret_kernel(q_ref, k_ref, v_ref, g_ref, cos_ref, sin_ref, intra_ref, xi_ref, zeta_ref,
                decay_ref, same_ref, gavg_ref, gnw_ref, o_ref, state_ref, *, heads, dh):
    w = heads * dh
    hw = w // 2
    half = dh // 2

    @pl.when(pl.program_id(1) == 0)
    def _():
        state_ref[...] = jnp.zeros_like(state_ref)

    cos = cos_ref[...]
    sin = sin_ref[...]

    def rot(x):
        x1, x2 = x[:, :hw], x[:, hw:]
        return jnp.concatenate([x1 * cos - x2 * sin, x1 * sin + x2 * cos], axis=1)

    q = rot(q_ref[...])
    k = rot(k_ref[...]) * (dh ** -0.5)
    v = v_ref[...]
    qb = q.astype(BF16)
    kb = k.astype(BF16)
    vb = v.astype(BF16)
    lane = lax.broadcasted_iota(jnp.int32, (1, w), 1)
    head_rot = (lane % hw) // half
    head_nat = lane // dh

    o = _dot(qb, state_ref[...].astype(BF16)) * xi_ref[...]
    for h in range(heads):
        qh = jnp.where(head_rot == h, qb, jnp.zeros_like(qb))
        vh = jnp.where(head_nat == h, vb, jnp.zeros_like(vb))
        s = _dot_nt(qh, kb) * intra_ref[h]
        o = o + _dot(s.astype(BF16), vh)
    kz = (k * zeta_ref[...]).astype(BF16)
    state_ref[...] = decay_ref[...] * state_ref[...] + _dot_tn(kz, vb) * same_ref[...]

    gavg = gavg_ref[...]
    mu = _split_dot(o, gavg)
    d = o - mu
    var = _split_dot(d * d, gavg)
    on = d * lax.rsqrt(var + GN_EPS) * gnw_ref[...]
    o_ref[...] = jax.nn.silu(g_ref[...]) * on


def retention_mixer(p_a, col0, gn_w, batch, seq, heads=RET_HEADS, chunk=256):
    w = gn_w.shape[-1]
    dh = w // heads
    nc = seq // chunk
    tb = _ret_tables(seq, chunk, heads, dh)
    cb = col0 // w
    assert col0 % w == 0
    row = lambda j: pl.BlockSpec((chunk, w), lambda b, c, j=j: (b * nc + c, cb + j))
    const = lambda shape: pl.BlockSpec(shape, lambda b, c: (0,) * len(shape))
    return pl.pallas_call(
        functools.partial(_ret_kernel, heads=heads, dh=dh),
        grid=(batch, nc),
        in_specs=[row(0), row(1), row(2), row(3),
                  pl.BlockSpec((chunk, w // 2), lambda b, c: (c, 0)),
                  pl.BlockSpec((chunk, w // 2), lambda b, c: (c, 0)),
                  const((heads, chunk, chunk)), const((chunk, w)), const((chunk, w)),
                  const((1, w)), const((w, w)), const((w, w)), const((1, w))],
        out_specs=pl.BlockSpec((chunk, w), lambda b, c: (b * nc + c, 0)),
        out_shape=jax.ShapeDtypeStruct((batch * seq, w), F32),
        scratch_shapes=[pltpu.VMEM((w, w), F32)],
        compiler_params=_cparams(("parallel", "arbitrary")),
        name="retention_mixer",
    )(p_a, p_a, p_a, p_a, tb["cos"], tb["sin"], tb["intra"], tb["xi"], tb["zeta"], tb["decay"],
      tb["same"], tb["gavg"], gn_w.reshape(1, w))


def _rot_perm(heads, dh):
    half = dh // 2
    first = (np.arange(heads)[:, None] * dh + np.arange(half)[None, :]).reshape(-1)
    return np.concatenate([first, first + half])


def _sb_kernel(q_ref, k_ref, v_ref, o_ref, *, dh, blk):
    i = pl.program_id(2)
    q2 = q_ref[...]
    lane = lax.broadcasted_iota(jnp.int32, (1, 2 * dh), 1)
    r_id = lax.broadcasted_iota(jnp.int32, (blk, blk), 0)
    c_id = lax.broadcasted_iota(jnp.int32, (blk, blk), 1)
    neg_upper = jnp.where(r_id > c_id, -1.0, 0.0).astype(BF16)
    causal = c_id < r_id
    zscale = (dh ** -0.5) * LOG2_E
    hsel = [(lane // dh) == hh for hh in range(2)]
    qm = [jnp.where(hs, q2, jnp.zeros_like(q2)) for hs in hsel]

    def pair(j, runs, diag):
        off = pl.multiple_of(j * blk, blk)
        kb = k_ref[pl.ds(off, blk), :]
        vb = v_ref[pl.ds(off, blk), :]
        wgts, vms, new_runs = [], [], []
        for hh in range(2):
            zs = _dot_nt(qm[hh], kb) * zscale
            nfail = jnp.maximum(zs, 0.0) + jnp.log2(1.0 + jnp.exp2(-jnp.abs(zs)))
            if diag:
                nfail = jnp.where(causal, nfail, 0.0)
            after = _dot(nfail.astype(BF16), neg_upper) + runs[hh]
            wgt = jnp.exp2((zs - nfail) + after)
            if diag:
                wgt = jnp.where(causal, wgt, 0.0)
            wgts.append(wgt.astype(BF16))
            vms.append(jnp.where(hsel[hh], vb, jnp.zeros_like(vb)))
            new_runs.append(runs[hh] - jnp.sum(nfail, axis=1, keepdims=True))
        contrib = _dot(jnp.concatenate(wgts, axis=1), jnp.concatenate(vms, axis=0))
        return contrib, new_runs

    zero = jnp.zeros((blk, 1), F32)
    acc0, runs0 = pair(i, [zero, zero], True)
    acc1, runs1 = pair(jnp.maximum(i - 1, 0), runs0, False)
    has_prev = i >= 1
    acc = acc0 + jnp.where(has_prev, acc1, 0.0)
    ra = jnp.where(has_prev, runs1[0], runs0[0])
    rb = jnp.where(has_prev, runs1[1], runs0[1])

    def cond(carry):
        jj, _, ra, rb = carry
        alive = jnp.maximum(jnp.max(ra), jnp.max(rb)) > SB_DEAD_LOG2
        return jnp.logical_and(jj <= i, alive)

    def body(carry):
        jj, acc, ra, rb = carry
        c, (ra, rb) = pair(i - jj, [ra, rb], False)
        return jj + 1, acc + c, ra, rb

    _, acc, _, _ = lax.while_loop(cond, body, (jnp.int32(2), acc, ra, rb))
    o_ref[...] = acc


def stick_breaking_mixer(p_sb, batch, seq, heads=SB_HEADS, blk=256):
    w = p_sb.shape[1] // 3
    dh = w // heads
    pw = 2 * dh
    assert pw == LANES
    npair = heads // 2
    nq = seq // blk
    return pl.pallas_call(
        functools.partial(_sb_kernel, dh=dh, blk=blk),
        grid=(batch, npair, nq),
        in_specs=[
            pl.BlockSpec((blk, pw), lambda b, p, i: (b * nq + i, p)),
            pl.BlockSpec((seq, pw), lambda b, p, i: (b, npair + p)),
            pl.BlockSpec((seq, pw), lambda b, p, i: (b, 2 * npair + p)),
        ],
        out_specs=pl.BlockSpec((blk, pw), lambda b, p, i: (b * nq + i, p)),
        out_shape=jax.ShapeDtypeStruct((batch * seq, w), F32),
        compiler_params=_cparams(("parallel", "parallel", "arbitrary")),
        name="stick_breaking_mixer",
    )(p_sb, p_sb, p_sb)


def _mix_cross_kernel(h_ref, ya_ref, yb_ref, yc_ref, gain_ref, wout_ref, cnw_ref, wq_ref, kv_ref,
                      wo_ref, o_ref, *, heads):
    wa = ya_ref.shape[1]
    wb = yb_ref.shape[1]
    d = h_ref.shape[1]
    gain = gain_ref[...]
    ya = _rms(ya_ref[...], gain[:, :wa]).astype(BF16)
    yb = yb_ref[...].astype(BF16)
    yc = _rms(yc_ref[...], gain[:, wa + wb:]).astype(BF16)
    h1 = (h_ref[...] + _dot(ya, wout_ref[:wa, :]) + _dot(yb, wout_ref[wa:wa + wb, :])
          + _dot(yc, wout_ref[wa + wb:, :]))
    q = _dot(_rms(h1, cnw_ref[...]).astype(BF16), wq_ref[...])
    dh = d // heads
    outs = []
    for hd in range(heads):
        qh = q[:, hd * dh:(hd + 1) * dh].astype(BF16)
        kh = kv_ref[:, hd * dh:(hd + 1) * dh]
        vh = kv_ref[:, d + hd * dh:d + (hd + 1) * dh]
        s = _dot_nt(qh, kh) * (dh ** -0.5)
        s = s - jnp.max(s, axis=-1, keepdims=True)
        e = jnp.exp(s)
        p = e / jnp.sum(e, axis=-1, keepdims=True)
        outs.append(_dot(p.astype(BF16), vh).astype(BF16))
    o_ref[...] = h1 + _dot(jnp.concatenate(outs, axis=1), wo_ref[...])


def mix_cross(h, ya, yb, yc, gain, w_out, cnw, wq, kv, wo, batch, seq, heads=CROSS_HEADS, tm=512):
    t, d = h.shape
    m = kv.shape[0] // batch
    nt = seq // tm
    rows = lambda wd: pl.BlockSpec((tm, wd), lambda i: (i, 0))
    const = lambda shape: pl.BlockSpec(shape, lambda i: (0, 0))
    return pl.pallas_call(
        functools.partial(_mix_cross_kernel, heads=heads),
        grid=(t // tm,),
        in_specs=[rows(d), rows(ya.shape[1]), rows(yb.shape[1]), rows(yc.shape[1]),
                  const((1, d)), const((d, d)), const((1, d)), const((d, d)),
                  pl.BlockSpec((m, 2 * d), lambda i: (i // nt, 0)),
                  const((d, d))],
        out_specs=rows(d),
        out_shape=jax.ShapeDtypeStruct((t, d), F32),
        compiler_params=_cparams(("parallel",)),
        name="mix_cross",
    )(h, ya, yb, yc, gain.reshape(1, d), w_out, cnw.reshape(1, d), wq, kv, wo)


FFN_TM = 1024
FFN_TF = 1408
FFN_CHUNK = 256
FFN_VMEM_MB = 56


def _swiglu_accumulate(x, wg_ref, wu_ref, wd_ref, o_ref):
    tf = wd_ref.shape[0]
    for c0 in range(0, tf, FFN_CHUNK):
        c1 = min(c0 + FFN_CHUNK, tf)
        act = jax.nn.silu(_dot(x, wg_ref[:, c0:c1])) * _dot(x, wu_ref[:, c0:c1])
        o_ref[...] += _dot(act.astype(BF16), wd_ref[c0:c1, :])


def _ffn_dense_kernel(h_ref, nw_ref, wg_ref, wu_ref, wd_ref, o_ref, xn_ref):
    f = pl.program_id(1)

    @pl.when(f == 0)
    def _():
        h = h_ref[...]
        xn_ref[...] = _rms(h, nw_ref[...]).astype(BF16)
        o_ref[...] = h

    _swiglu_accumulate(xn_ref[...], wg_ref, wu_ref, wd_ref, o_ref)


def ffn_dense(h, nw, w_gu, w_down, tm=FFN_TM, tf=FFN_TF):
    t, d = h.shape
    ff = w_down.shape[0]
    nf = ff // tf
    return pl.pallas_call(
        _ffn_dense_kernel,
        grid=(t // tm, nf),
        in_specs=[
            pl.BlockSpec((tm, d), lambda i, f: (i, 0)),
            pl.BlockSpec((1, d), lambda i, f: (0, 0)),
            pl.BlockSpec((d, tf), lambda i, f: (0, f)),
            pl.BlockSpec((d, tf), lambda i, f: (0, nf + f)),
            pl.BlockSpec((tf, d), lambda i, f: (f, 0)),
        ],
        out_specs=pl.BlockSpec((tm, d), lambda i, f: (i, 0)),
        out_shape=jax.ShapeDtypeStruct((t, d), F32),
        scratch_shapes=[pltpu.VMEM((tm, d), BF16)],
        compiler_params=_cparams(("parallel", "arbitrary"), FFN_VMEM_MB),
        name="ffn_dense",
    )(h, nw.reshape(1, d), w_gu, w_gu, w_down)


def _ffn_group_kernel(te_ref, nv_ref, rows_ref, x_ref, wg_ref, wu_ref, wd_ref, o_ref, xb_ref):
    i = pl.program_id(0)
    f = pl.program_id(1)
    valid = i < nv_ref[0]

    @pl.when(f == 0)
    def _():
        row = lax.broadcasted_iota(jnp.int32, (x_ref.shape[0], 1), 0)
        xb_ref[...] = jnp.where(row < rows_ref[i], x_ref[...], 0.0).astype(BF16)
        o_ref[...] = jnp.zeros_like(o_ref)

    @pl.when(valid)
    def _():
        _swiglu_accumulate(xb_ref[...], wg_ref.at[0], wu_ref.at[0], wd_ref.at[0], o_ref)


def ffn_grouped(xs, tile_expert, n_valid, tile_rows, w_gu, w_down, tm=FFN_TM, tf=FFN_TF):
    p, d = xs.shape
    ff = w_down.shape[1]
    nf = ff // tf

    def live(i, f, nv):
        ok = i < nv[0]
        return jnp.where(ok, i, nv[0] - 1), jnp.where(ok, f, nf - 1)

    def x_map(i, f, te, nv, rows):
        ii, _ = live(i, f, nv)
        return ii, 0

    def wg_map(i, f, te, nv, rows):
        ii, ff_ = live(i, f, nv)
        return te[ii], 0, ff_

    def wu_map(i, f, te, nv, rows):
        ii, ff_ = live(i, f, nv)
        return te[ii], 0, nf + ff_

    def wd_map(i, f, te, nv, rows):
        ii, ff_ = live(i, f, nv)
        return te[ii], ff_, 0

    grid_spec = pltpu.PrefetchScalarGridSpec(
        num_scalar_prefetch=3,
        grid=(p // tm, nf),
        in_specs=[
            pl.BlockSpec((tm, d), x_map),
            pl.BlockSpec((1, d, tf), wg_map),
            pl.BlockSpec((1, d, tf), wu_map),
            pl.BlockSpec((1, tf, d), wd_map),
        ],
        out_specs=pl.BlockSpec((tm, d), lambda i, f, te, nv, rows: (i, 0)),
        scratch_shapes=[pltpu.VMEM((tm, d), BF16)],
    )
    return pl.pallas_call(
        _ffn_group_kernel,
        grid_spec=grid_spec,
        out_shape=jax.ShapeDtypeStruct((p, d), F32),
        compiler_params=_cparams(("arbitrary", "arbitrary"), FFN_VMEM_MB),
        name="ffn_grouped",
    )(tile_expert, n_valid, tile_rows, xs, w_gu, w_gu, w_down)


V7X_SC_CORES = 2
V7X_SC_SUBCORES = 16
V7X_SC_WORKERS = V7X_SC_CORES * V7X_SC_SUBCORES


def _sc_mesh():
    return plsc.VectorSubcoreMesh(core_axis_name="c", subcore_axis_name="s")


def sc_scatter_rows2(x, idx1, idx2, n_out, ch=64):
    t, d = x.shape
    assert t % (V7X_SC_WORKERS * ch) == 0
    t_per_w = t // V7X_SC_WORKERS
    n_ch = t_per_w // ch

    @functools.partial(
        pl.kernel, mesh=_sc_mesh(), out_type=jax.ShapeDtypeStruct((n_out, d), x.dtype),
        scratch_types=[pltpu.VMEM((ch,), jnp.int32), pltpu.VMEM((ch,), jnp.int32),
                       pltpu.VMEM((ch, d), x.dtype), pltpu.SemaphoreType.DMA])
    def k(x_hbm, i1_hbm, i2_hbm, out_hbm, i1_v, i2_v, rows_v, sem):
        wid = lax.axis_index("s") * V7X_SC_CORES + lax.axis_index("c")
        base = wid * t_per_w

        @pl.loop(0, n_ch)
        def _(j):
            off = pl.multiple_of(base + j * ch, 8)
            pltpu.sync_copy(i1_hbm.at[pl.ds(off, ch)], i1_v)
            pltpu.sync_copy(i2_hbm.at[pl.ds(off, ch)], i2_v)
            pltpu.sync_copy(x_hbm.at[pl.ds(off, ch)], rows_v)
            pltpu.async_copy(rows_v, out_hbm.at[i1_v], sem).wait()
            pltpu.async_copy(rows_v, out_hbm.at[i2_v], sem).wait()

    return k(x, idx1, idx2)


def sc_gather_rows(table, idx, ch=64):
    _, d = table.shape
    b = idx.shape[0]
    assert b % (V7X_SC_WORKERS * ch) == 0
    b_per_w = b // V7X_SC_WORKERS
    n_ch = b_per_w // ch

    @functools.partial(
        pl.kernel, mesh=_sc_mesh(), out_type=jax.ShapeDtypeStruct((b, d), table.dtype),
        scratch_types=[pltpu.VMEM((ch,), jnp.int32), pltpu.VMEM((ch, d), table.dtype),
                       pltpu.SemaphoreType.DMA])
    def k(table_hbm, idx_hbm, out_hbm, idx_v, rows_v, sem):
        wid = lax.axis_index("s") * V7X_SC_CORES + lax.axis_index("c")
        base = wid * b_per_w

        @pl.loop(0, n_ch)
        def _(j):
            off = pl.multiple_of(base + j * ch, 8)
            pltpu.sync_copy(idx_hbm.at[pl.ds(off, ch)], idx_v)
            pltpu.async_copy(table_hbm.at[idx_v], rows_v, sem).wait()
            pltpu.sync_copy(rows_v, out_hbm.at[pl.ds(off, ch)])

    return k(table, idx)


def _moe_combine_kernel(h_ref, y1_ref, y2_ref, g1_ref, g2_ref, nw_ref, o_ref, *, normed):
    out = h_ref[...] + g1_ref[...] * y1_ref[...] + g2_ref[...] * y2_ref[...]
    o_ref[...] = _rms(out, nw_ref[...]) if normed else out


def moe_combine(h, yg, gate1, gate2, norm_w=None, tm=1024):
    t, d = h.shape
    nt = t // tm
    row = pl.BlockSpec((tm, d), lambda i: (i, 0))
    col = pl.BlockSpec((tm, 1), lambda i: (i, 0))
    nw = jnp.ones((d,), F32) if norm_w is None else norm_w
    return pl.pallas_call(
        functools.partial(_moe_combine_kernel, normed=norm_w is not None),
        grid=(nt,),
        in_specs=[row, row, pl.BlockSpec((tm, d), lambda i: (nt + i, 0)), col, col,
                  pl.BlockSpec((1, d), lambda i: (0, 0))],
        out_specs=row,
        out_shape=jax.ShapeDtypeStruct((t, d), F32),
        compiler_params=_cparams(("parallel",)),
        name="moe_combine",
    )(h, yg, yg, gate1, gate2, nw.reshape(1, d))


def _router_kernel(h_ref, nw_ref, wr_ref, xn_ref, slot_ref, gate_ref, *, n_exp):
    xn = _rms(h_ref[...], nw_ref[...])
    xn_ref[...] = xn
    xh = xn.astype(BF16)
    xl = (xn - xh.astype(F32)).astype(BF16)
    wr = wr_ref[...]
    wh = wr.astype(BF16)
    wl = (wr - wh.astype(F32)).astype(BF16)
    logits = _dot(xh, wh) + _dot(xh, wl) + _dot(xl, wh)
    lane = lax.broadcasted_iota(jnp.int32, logits.shape, 1)
    neg = -jnp.inf
    lg = jnp.where(lane < n_exp, logits, neg)
    m1 = jnp.max(lg, axis=-1, keepdims=True)
    i1 = jnp.min(jnp.where(lg == m1, lane, LANES), axis=-1, keepdims=True)
    lg2 = jnp.where(lane == i1, neg, lg)
    m2 = jnp.max(lg2, axis=-1, keepdims=True)
    i2 = jnp.min(jnp.where(lg2 == m2, lane, LANES), axis=-1, keepdims=True)
    e2 = jnp.exp(m2 - m1)
    g1 = 1.0 / (1.0 + e2)
    g2 = e2 / (1.0 + e2)
    slot = jnp.where(lane == i1, 1, jnp.where(lane == i2, 2, 0))
    gate = jnp.where(lane == i1, g1, jnp.where(lane == i2, g2, 0.0))
    slot_ref[...] = slot[:, :n_exp]
    gate_ref[...] = gate[:, :n_exp]


def router(h, nw, w_router, tm=512):
    t, d = h.shape
    n_exp = w_router.shape[1]
    wr = jnp.pad(w_router, ((0, 0), (0, LANES - n_exp)))
    return pl.pallas_call(
        functools.partial(_router_kernel, n_exp=n_exp),
        grid=(t // tm,),
        in_specs=[pl.BlockSpec((tm, d), lambda i: (i, 0)),
                  pl.BlockSpec((1, d), lambda i: (0, 0)),
                  pl.BlockSpec((d, LANES), lambda i: (0, 0))],
        out_specs=[pl.BlockSpec((tm, d), lambda i: (i, 0)),
                   pl.BlockSpec((tm, n_exp), lambda i: (i, 0)),
                   pl.BlockSpec((tm, n_exp), lambda i: (i, 0))],
        out_shape=[jax.ShapeDtypeStruct((t, d), F32),
                   jax.ShapeDtypeStruct((t, n_exp), jnp.int32),
                   jax.ShapeDtypeStruct((t, n_exp), F32)],
        compiler_params=_cparams(("parallel",)),
        name="router",
    )(h, nw.reshape(1, d), wr)


def _dispatch_plan(slot, gate, tm):
    t, n_exp = slot.shape
    sel = (slot > 0).astype(jnp.int32)
    counts = jnp.sum(sel, axis=0)
    rank = jnp.cumsum(sel, axis=0) - sel
    tiles = (counts + tm - 1) // tm
    tile_end = jnp.cumsum(tiles)
    start = (tile_end - tiles) * tm
    pos = start[None, :] + rank
    n_rows = 2 * t + n_exp * tm
    n_tiles = n_rows // tm
    pos1 = jnp.sum(jnp.where(slot == 1, pos, 0), axis=1)
    pos2 = jnp.sum(jnp.where(slot == 2, pos, 0), axis=1)
    gate1 = jnp.sum(jnp.where(slot == 1, gate, 0.0), axis=1, keepdims=True)
    gate2 = jnp.sum(jnp.where(slot == 2, gate, 0.0), axis=1, keepdims=True)
    tile_id = jnp.arange(n_tiles, dtype=jnp.int32)
    owner = tile_id[:, None] >= tile_end[None, :]
    tile_expert = jnp.minimum(jnp.sum(owner.astype(jnp.int32), axis=1), n_exp - 1)
    mine = tile_expert[:, None] == jnp.arange(n_exp, dtype=jnp.int32)[None, :]
    group_end = jnp.sum(jnp.where(mine, (start + counts)[None, :], 0), axis=1)
    n_valid = tile_end[-1:].astype(jnp.int32)
    tile_rows = jnp.where(tile_id < n_valid[0], jnp.clip(group_end - tile_id * tm, 0, tm), 0)
    return (n_rows, pos1, pos2, gate1, gate2, tile_expert.astype(jnp.int32), n_valid,
            tile_rows.astype(jnp.int32))


def _final_norm_kernel(h_ref, w_ref, o_ref):
    o_ref[...] = _rms(h_ref[...], w_ref[...])


def final_norm(h, w, tm=1024):
    t, d = h.shape
    return pl.pallas_call(
        _final_norm_kernel,
        grid=(t // tm,),
        in_specs=[pl.BlockSpec((tm, d), lambda i: (i, 0)), pl.BlockSpec((1, d), lambda i: (0, 0))],
        out_specs=pl.BlockSpec((tm, d), lambda i: (i, 0)),
        out_shape=jax.ShapeDtypeStruct((t, d), F32),
        compiler_params=_cparams(("parallel",)),
        name="final_norm",
    )(h, w.reshape(1, d))


def kernel(x, mem, mix_norm_w, w_in, ssm_lambda_re, ssm_lambda_im, ssm_b_re, ssm_b_im, ssm_c_re, ssm_c_im, ssm_d, ssm_log_dt, ssm_w_glu, mix_out_gain, w_out, cross_norm_w, mem_norm_w, w_cross_q, w_cross_kv, w_cross_o, ffn_norm_w, w_dense_gu, w_dense_down, w_router, w_expert_gu, w_expert_down, final_norm_w):
    batch, seq, d = x.shape
    depth = w_in.shape[0]
    mem_len = mem.shape[1]
    ssm_w = ssm_d.shape[1]
    g, p = ssm_lambda_re.shape[1:]
    nh = ssm_b_re.shape[-1]
    ret_w = ssm_w
    sb_w = d - ssm_w - ret_w
    ret_dh = ret_w // RET_HEADS
    moe_tm = FFN_TM

    perm = _rot_perm(RET_HEADS, ret_dh)
    cols = np.arange(w_in.shape[2])
    cols[ssm_w:ssm_w + ret_w] = ssm_w + perm
    cols[ssm_w + ret_w:ssm_w + 2 * ret_w] = ssm_w + ret_w + perm
    w_in_b = w_in[:, :, cols].astype(BF16)
    n_a = ssm_w + 4 * ret_w

    apow, bbc = s5_prep(ssm_lambda_re, ssm_lambda_im, ssm_log_dt, ssm_b_re, ssm_b_im,
                        nstep=S5_TIME_TILE // SUBLANES)

    mem2 = mem.reshape(batch * mem_len, d)
    h = x.reshape(batch * seq, d)
    for i in range(depth):
        p_a, p_sb = norm_proj(h, mix_norm_w[i], w_in_b[i], (n_a, 3 * sb_w), (F32, BF16))
        bb = _block_diag_in(bbc[i], g, p, nh).astype(BF16)
        cc = _block_diag_out(ssm_c_re[i], ssm_c_im[i]).astype(BF16)
        y_a = s5_mixer(p_a, bb, cc, apow[i], ssm_d[i], ssm_w_glu[i].astype(BF16), batch, seq)
        gain = mix_out_gain[i]
        y_b = retention_mixer(p_a, ssm_w, gain[ssm_w:ssm_w + ret_w], batch, seq)
        y_c = stick_breaking_mixer(p_sb, batch, seq)
        (kv,) = norm_proj(mem2, mem_norm_w, w_cross_kv[i].astype(BF16), (2 * d,), (BF16,))
        h = mix_cross(h, y_a, y_b, y_c, gain, w_out[i].astype(BF16), cross_norm_w[i],
                      w_cross_q[i].astype(BF16), kv, w_cross_o[i].astype(BF16), batch, seq)
        if i % 2 == 0:
            h = ffn_dense(h, ffn_norm_w[i], w_dense_gu[i // 2].astype(BF16),
                          w_dense_down[i // 2].astype(BF16))
        else:
            xn, slot, gate = router(h, ffn_norm_w[i], w_router[i // 2])
            n_rows, pos1, pos2, gate1, gate2, tile_expert, n_valid, tile_rows = _dispatch_plan(
                slot, gate, moe_tm)
            xs = sc_scatter_rows2(xn, pos1, pos2, n_rows)
            y = ffn_grouped(xs, tile_expert, n_valid, tile_rows, w_expert_gu[i // 2].astype(BF16),
                            w_expert_down[i // 2].astype(BF16), tm=moe_tm)
            yg = sc_gather_rows(y, jnp.concatenate([pos1, pos2]))
            last = i == depth - 1
            h = moe_combine(h, yg, gate1, gate2, norm_w=final_norm_w if last else None)
    if depth % 2 == 1:
        h = final_norm(h, final_norm_w)
    return h.reshape(batch, seq, d)
```

```python
import functools
import math

import numpy as np
import jax
import jax.numpy as jnp
from jax import lax
from jax.experimental import pallas as pl
from jax.experimental.pallas import tpu as pltpu
from jax.experimental.pallas import tpu_sc as plsc

F32 = jnp.float32
BF16 = jnp.bfloat16

NORM_EPS = 1e-6
GN_EPS = 1e-6
ROPE_BASE = 10000.0
LOG2_E = 1.4426950408889634
SB_DEAD_LOG2 = -150.0
SB_MASKED_LOG2 = -1e30

SSM_GROUP = 16
SSM_STATE = 64
RET_HEADS = 4
SB_HEADS = 8
CROSS_HEADS = 4
N_EXPERTS = 8

V7X_VMEM_BYTES = 64 * 1024 * 1024
LANES = 128
SUBLANES = 8


def _cparams(sem, vmem_mb=48):
    return pltpu.CompilerParams(dimension_semantics=sem, vmem_limit_bytes=vmem_mb * 1024 * 1024)


def _rms(x, w):
    ms = jnp.mean(x * x, axis=-1, keepdims=True)
    return x * lax.rsqrt(ms + NORM_EPS) * w


def _dot(a, b):
    return jnp.dot(a, b, preferred_element_type=F32)


def _dot_nt(a, b):
    return lax.dot_general(a, b, (((1,), (1,)), ((), ())), preferred_element_type=F32)


def _dot_tn(a, b):
    return lax.dot_general(a, b, (((0,), (0,)), ((), ())), preferred_element_type=F32)


def _pack_bf16_pairs(a, b):
    ua = lax.bitcast_convert_type(a.astype(BF16).astype(F32), jnp.uint32)
    ub = lax.bitcast_convert_type(b.astype(BF16).astype(F32), jnp.uint32)
    return (ub & jnp.uint32(0xFFFF0000)) | (ua >> 16)


def _unpack_bf16_pairs(p):
    lo = lax.bitcast_convert_type(p << 16, F32)
    hi = lax.bitcast_convert_type(p & jnp.uint32(0xFFFF0000), F32)
    return jnp.concatenate([lo, hi], axis=1)


def _split_dot(x, m_bf16):
    hi = x.astype(BF16)
    lo = (x - hi.astype(F32)).astype(BF16)
    return _dot(hi, m_bf16) + _dot(lo, m_bf16)


def _norm_proj_kernel(x_ref, nw_ref, w_ref, *o_refs, splits, chunk):
    xn = _rms(x_ref[...], nw_ref[...]).astype(BF16)
    col = 0
    for o_ref, width in zip(o_refs, splits):
        for c0 in range(0, width, chunk):
            r = _dot(xn, w_ref[:, col + c0:col + c0 + chunk])
            o_ref[:, c0:c0 + chunk] = r.astype(o_ref.dtype)
        col += width


def norm_proj(x, nw, w, layer, splits, dtypes, tm=512, chunk=256):
    rows, d = x.shape
    n = w.shape[2]
    assert sum(splits) == n and rows % tm == 0
    return pl.pallas_call(
        functools.partial(_norm_proj_kernel, splits=splits, chunk=chunk),
        grid=(rows // tm,),
        in_specs=[
            pl.BlockSpec((tm, d), lambda i: (i, 0)),
            pl.BlockSpec((1, d), lambda i: (0, 0)),
            pl.BlockSpec((None, d, n), lambda i: (layer, 0, 0)),
        ],
        out_specs=[pl.BlockSpec((tm, s), lambda i: (i, 0)) for s in splits],
        out_shape=[jax.ShapeDtypeStruct((rows, s), dt) for s, dt in zip(splits, dtypes)],
        compiler_params=_cparams(("parallel",)),
        name="norm_proj",
    )(x, nw.reshape(1, d), w)


def _s5_prep_kernel(lr_ref, li_ref, ldt_ref, br_ref, bi_ref, apow_ref, bb_ref):
    lr = lr_ref[0]
    li = li_ref[0]
    dt = jnp.exp(ldt_ref[0])
    mag = jnp.exp(lr * dt)
    a_re = mag * jnp.cos(li * dt)
    a_im = mag * jnp.sin(li * dt)
    denom = lr * lr + li * li
    nr = a_re - 1.0
    z_re = (nr * lr + a_im * li) / denom
    z_im = (a_im * lr - nr * li) / denom
    br = br_ref[0]
    bi = bi_ref[0]
    nh = br.shape[0]
    bb_ref[0, :nh, :] = z_re * br - z_im * bi
    bb_ref[0, nh:, :] = z_re * bi + z_im * br
    nstep = (apow_ref.shape[1] - 2 * S5_SEG_LEVELS) // 2
    pr, pi = a_re, a_im
    for j in range(nstep):
        apow_ref[0, j:j + 1, :] = pr
        apow_ref[0, nstep + j:nstep + j + 1, :] = pi
        if j + 1 < nstep:
            pr, pi = pr * a_re - pi * a_im, pr * a_im + pi * a_re
    for k in range(S5_SEG_LEVELS):
        apow_ref[0, 2 * nstep + k:2 * nstep + k + 1, :] = pr
        apow_ref[0, 2 * nstep + S5_SEG_LEVELS + k:2 * nstep + S5_SEG_LEVELS + k + 1, :] = pi
        pr, pi = pr * pr - pi * pi, 2.0 * pr * pi


S5_SEG_LEVELS = 3
S5_TIME_TILE = 512


def s5_prep(lam_re, lam_im, log_dt, b_re, b_im, nstep):
    depth, g, p = lam_re.shape
    nh = b_re.shape[-1]
    gp = g * p
    lr = lam_re.reshape(depth, 1, gp)
    li = lam_im.reshape(depth, 1, gp)
    ldt = jnp.repeat(log_dt, p, axis=1).reshape(depth, 1, gp)
    br = b_re.transpose(0, 3, 1, 2).reshape(depth, nh, gp)
    bi = b_im.transpose(0, 3, 1, 2).reshape(depth, nh, gp)
    vec = pl.BlockSpec((1, 1, gp), lambda i: (i, 0, 0))
    mat = pl.BlockSpec((1, nh, gp), lambda i: (i, 0, 0))
    npow = 2 * (nstep + S5_SEG_LEVELS)
    return pl.pallas_call(
        _s5_prep_kernel,
        grid=(depth,),
        in_specs=[vec, vec, vec, mat, mat],
        out_specs=[pl.BlockSpec((1, npow, gp), lambda i: (i, 0, 0)),
                   pl.BlockSpec((1, 2 * nh, gp), lambda i: (i, 0, 0))],
        out_shape=[jax.ShapeDtypeStruct((depth, npow, gp), F32),
                   jax.ShapeDtypeStruct((depth, 2 * nh, gp), F32)],
        compiler_params=_cparams(("arbitrary",)),
        name="s5_prep",
    )(lr, li, ldt, br, bi)


def _s5_kernel(u_ref, bb_ref, cc_ref, apow_ref, d_ref, wglu_ref, o_ref, xs_ref, carry_ref, perm_ref,
               *, gp):
    ts = u_ref.shape[0]
    nstep = ts // SUBLANES
    rows = lambda s: slice(s * SUBLANES, (s + 1) * SUBLANES)

    @pl.when(pl.program_id(1) == 0)
    def _():
        carry_ref[...] = jnp.zeros_like(carry_ref)

    ntile = u_ref.shape[1] // LANES
    for c in range(ntile):
        perm_ref[c] = u_ref[:, c * LANES:(c + 1) * LANES]
    u = jnp.concatenate(
        [jnp.concatenate([perm_ref[c, pl.ds(s, SUBLANES, stride=nstep), :] for c in range(ntile)], axis=1)
         for s in range(nstep)], axis=0)
    xs_ref[...] = _dot(u.astype(BF16), bb_ref[...])

    ar = apow_ref[0:1, :]
    ai = apow_ref[nstep:nstep + 1, :]
    xr = xs_ref[rows(0), :gp]
    xi = xs_ref[rows(0), gp:]
    for s in range(1, nstep):
        xr, xi = (ar * xr - ai * xi + xs_ref[rows(s), :gp], ar * xi + ai * xr + xs_ref[rows(s), gp:])
        xs_ref[rows(s), :gp] = xr
        xs_ref[rows(s), gp:] = xi

    seg = lax.broadcasted_iota(jnp.int32, (SUBLANES, 1), 0)
    cr = carry_ref[0:1, :]
    ci = carry_ref[1:2, :]
    lr = apow_ref[2 * nstep:2 * nstep + 1, :]
    li = apow_ref[2 * nstep + S5_SEG_LEVELS:2 * nstep + S5_SEG_LEVELS + 1, :]
    first = seg == 0
    xr = xr + jnp.where(first, lr * cr - li * ci, 0.0)
    xi = xi + jnp.where(first, lr * ci + li * cr, 0.0)
    for k in range(S5_SEG_LEVELS):
        sh = 1 << k
        pr = apow_ref[2 * nstep + k:2 * nstep + k + 1, :]
        pi = apow_ref[2 * nstep + S5_SEG_LEVELS + k:2 * nstep + S5_SEG_LEVELS + k + 1, :]
        keep = seg >= sh
        sr = jnp.where(keep, pltpu.roll(xr, sh, axis=0), 0.0)
        si = jnp.where(keep, pltpu.roll(xi, sh, axis=0), 0.0)
        xr, xi = xr + pr * sr - pi * si, xi + pr * si + pi * sr
    carry_ref[0:1, :] = xr[SUBLANES - 1:, :]
    carry_ref[1:2, :] = xi[SUBLANES - 1:, :]
    enter_r = jnp.where(first, cr, pltpu.roll(xr, 1, axis=0))
    enter_i = jnp.where(first, ci, pltpu.roll(xi, 1, axis=0))

    for s in range(nstep):
        pr = apow_ref[s:s + 1, :]
        pi = apow_ref[nstep + s:nstep + s + 1, :]
        xs_ref[rows(s), :gp] += pr * enter_r - pi * enter_i
        xs_ref[rows(s), gp:] += pr * enter_i + pi * enter_r

    y = _dot(xs_ref[...].astype(BF16), cc_ref[...]) + d_ref[...] * u
    g = jax.nn.gelu(y)
    out = g * jax.nn.sigmoid(_dot(g.astype(BF16), wglu_ref[...]))
    for s in range(nstep):
        for c in range(ntile):
            perm_ref[c, pl.ds(s, SUBLANES, stride=nstep), :] = out[rows(s), c * LANES:(c + 1) * LANES]
    for c in range(ntile):
        o_ref[:, c * LANES:(c + 1) * LANES] = perm_ref[c]


def s5_mixer(p_a, bb, cc, apow, d_skip, w_glu, batch, seq):
    w = d_skip.shape[-1]
    gp = apow.shape[-1]
    ts = (apow.shape[0] // 2 - S5_SEG_LEVELS) * SUBLANES
    nt = seq // ts
    return pl.pallas_call(
        functools.partial(_s5_kernel, gp=gp),
        grid=(batch, nt),
        in_specs=[
            pl.BlockSpec((ts, w), lambda b, t: (b * nt + t, 0)),
            pl.BlockSpec((w, 2 * gp), lambda b, t: (0, 0)),
            pl.BlockSpec((2 * gp, w), lambda b, t: (0, 0)),
            pl.BlockSpec(apow.shape, lambda b, t: (0, 0)),
            pl.BlockSpec((1, w), lambda b, t: (0, 0)),
            pl.BlockSpec((w, w), lambda b, t: (0, 0)),
        ],
        out_specs=pl.BlockSpec((ts, w), lambda b, t: (b * nt + t, 0)),
        out_shape=jax.ShapeDtypeStruct((batch * seq, w), F32),
        scratch_shapes=[pltpu.VMEM((ts, 2 * gp), F32), pltpu.VMEM((2, gp), F32),
                        pltpu.VMEM((w // LANES, ts, LANES), F32)],
        compiler_params=_cparams(("parallel", "arbitrary")),
        name="s5_mixer",
    )(p_a, bb, cc, apow, d_skip.reshape(1, w), w_glu)


def _block_diag_in(bb, g, p, nh):
    eye = jnp.eye(g, dtype=F32)
    def one(m):
        m = m.reshape(nh, g, p)
        return jnp.einsum('hgp,kg->khgp', m, eye).reshape(g * nh, g * p)
    return jnp.concatenate([one(bb[:nh]), one(bb[nh:])], axis=1)


def _block_diag_out(c_re, c_im):
    g, nh, p = c_re.shape
    eye = jnp.eye(g, dtype=F32)
    def one(m):
        return jnp.einsum('ghp,gk->gpkh', m, eye).reshape(g * p, g * nh)
    return jnp.concatenate([one(c_re), one(-c_im)], axis=0)


def _ret_tables(seq, chunk, heads, dh):
    half = dh // 2
    w = heads * dh
    hw = w // 2
    inv_freq = ROPE_BASE ** (-np.arange(half, dtype=np.float64) / half)
    ang = np.arange(seq, dtype=np.float64)[:, None] * np.tile(inv_freq, heads)[None, :]
    cos = np.cos(ang).astype(np.float32)
    sin = np.sin(ang).astype(np.float32)
    log_gamma = np.log1p(-(2.0 ** (-5.0 - np.arange(heads, dtype=np.float64))))
    idx = np.arange(chunk, dtype=np.float64)
    rel = idx[:, None] - idx[None, :]
    intra = np.where(rel >= 0, np.exp(log_gamma[:, None, None] * np.maximum(rel, 0.0)), 0.0)
    head_nat = np.arange(w) // dh
    head_rot = (np.arange(w) % hw) // half
    xi = np.exp(log_gamma[head_nat][None, :] * (idx[:, None] + 1.0))
    zeta = np.exp(log_gamma[head_rot][None, :] * (chunk - 1.0 - idx[:, None]))
    decay = np.exp(log_gamma[head_nat] * chunk)[None, :]
    same = (head_rot[:, None] == head_nat[None, :]).astype(np.float32)
    gavg = (head_nat[:, None] == head_nat[None, :]).astype(np.float32) / dh
    f = lambda a: jnp.asarray(a, dtype=F32)
    return dict(cos=f(cos), sin=f(sin), intra=f(intra), xi=f(xi), zeta=f(zeta), decay=f(decay),
                same=f(same), gavg=jnp.asarray(gavg, dtype=BF16))


def _ret_kernel(q_ref, k_ref, v_ref, g_ref, cos_ref, sin_ref, intra_ref, xi_ref, zeta_ref,
                decay_ref, same_ref, gavg_ref, gnw_ref, o_ref, state_ref, *, heads, dh):
    w = heads * dh
    hw = w // 2
    half = dh // 2

    @pl.when(pl.program_id(1) == 0)
    def _():
        state_ref[...] = jnp.zeros_like(state_ref)

    cos = cos_ref[...]
    sin = sin_ref[...]

    def rot(x):
        x1, x2 = x[:, :hw], x[:, hw:]
        return jnp.concatenate([x1 * cos - x2 * sin, x1 * sin + x2 * cos], axis=1)

    q = rot(q_ref[...])
    k = rot(k_ref[...]) * (dh ** -0.5)
    v = v_ref[...]
    qb = q.astype(BF16)
    kb = k.astype(BF16)
    vb = v.astype(BF16)
    lane = lax.broadcasted_iota(jnp.int32, (1, w), 1)
    head_rot = (lane % hw) // half
    head_nat = lane // dh

    o = _dot(qb, state_ref[...].astype(BF16)) * xi_ref[...]
    for h in range(heads):
        qh = jnp.where(head_rot == h, qb, jnp.zeros_like(qb))
        vh = jnp.where(head_nat == h, vb, jnp.zeros_like(vb))
        s = _dot_nt(qh, kb) * intra_ref[h]
        o = o + _dot(s.astype(BF16), vh)
    kz = (k * zeta_ref[...]).astype(BF16)
    state_ref[...] = decay_ref[...] * state_ref[...] + _dot_tn(kz, vb) * same_ref[...]

    gavg = gavg_ref[...]
    mu = _split_dot(o, gavg)
    d = o - mu
    var = _split_dot(d * d, gavg)
    on = d * lax.rsqrt(var + GN_EPS) * gnw_ref[...]
    o_ref[...] = jax.nn.silu(g_ref[...]) * on


def retention_mixer(p_a, col0, gn_w, batch, seq, heads=RET_HEADS, chunk=256):
    w = gn_w.shape[-1]
    dh = w // heads
    nc = seq // chunk
    tb = _ret_tables(seq, chunk, heads, dh)
    cb = col0 // w
    assert col0 % w == 0
    row = lambda j: pl.BlockSpec((chunk, w), lambda b, c, j=j: (b * nc + c, cb + j))
    const = lambda shape: pl.BlockSpec(shape, lambda b, c: (0,) * len(shape))
    return pl.pallas_call(
        functools.partial(_ret_kernel, heads=heads, dh=dh),
        grid=(batch, nc),
        in_specs=[row(0), row(1), row(2), row(3),
                  pl.BlockSpec((chunk, w // 2), lambda b, c: (c, 0)),
                  pl.BlockSpec((chunk, w // 2), lambda b, c: (c, 0)),
                  const((heads, chunk, chunk)), const((chunk, w)), const((chunk, w)),
                  const((1, w)), const((w, w)), const((w, w)), const((1, w))],
        out_specs=pl.BlockSpec((chunk, w), lambda b, c: (b * nc + c, 0)),
        out_shape=jax.ShapeDtypeStruct((batch * seq, w), F32),
        scratch_shapes=[pltpu.VMEM((w, w), F32)],
        compiler_params=_cparams(("parallel", "arbitrary")),
        name="retention_mixer",
    )(p_a, p_a, p_a, p_a, tb["cos"], tb["sin"], tb["intra"], tb["xi"], tb["zeta"], tb["decay"],
      tb["same"], tb["gavg"], gn_w.reshape(1, w))


def _rot_perm(heads, dh):
    half = dh // 2
    first = (np.arange(heads)[:, None] * dh + np.arange(half)[None, :]).reshape(-1)
    return np.concatenate([first, first + half])


def _sb_kernel(q_ref, k_ref, v_ref, o_ref, *, dh, blk):
    i = pl.program_id(2)
    q2 = q_ref[...]
    lane = lax.broadcasted_iota(jnp.int32, (1, 2 * dh), 1)
    r_id = lax.broadcasted_iota(jnp.int32, (blk, blk), 0)
    c_id = lax.broadcasted_iota(jnp.int32, (blk, blk), 1)
    neg_upper = jnp.where(r_id > c_id, -1.0, 0.0).astype(BF16)
    causal = c_id < r_id
    zscale = (dh ** -0.5) * LOG2_E
    hsel = [(lane // dh) == hh for hh in range(2)]
    qm = [jnp.where(hs, q2, jnp.zeros_like(q2)) for hs in hsel]

    def pair(j, runs, diag):
        off = pl.multiple_of(j * blk, blk)
        kb = k_ref[pl.ds(off, blk), :]
        vb = v_ref[pl.ds(off, blk), :]
        wgts, vms, new_runs = [], [], []
        for hh in range(2):
            zs = _dot_nt(qm[hh], kb) * zscale
            nfail = jnp.maximum(zs, 0.0) + jnp.log2(1.0 + jnp.exp2(jnp.minimum(zs, -zs)))
            if diag:
                nfail = jnp.where(causal, nfail, 0.0)
            after = _dot(nfail.astype(BF16), neg_upper) + runs[hh]
            wgt = jnp.exp2((zs - nfail) + after)
            if diag:
                wgt = jnp.where(causal, wgt, 0.0)
            wgts.append(wgt.astype(BF16))
            vms.append(jnp.where(hsel[hh], vb, jnp.zeros_like(vb)))
            new_runs.append(runs[hh] - jnp.sum(nfail, axis=1, keepdims=True))
        contrib = _dot(jnp.concatenate(wgts, axis=1), jnp.concatenate(vms, axis=0))
        return contrib, new_runs

    zero = jnp.zeros((blk, 1), F32)
    acc0, runs0 = pair(i, [zero, zero], True)
    acc1, runs1 = pair(jnp.maximum(i - 1, 0), runs0, False)
    has_prev = i >= 1
    acc = acc0 + jnp.where(has_prev, acc1, 0.0)
    ra = jnp.where(has_prev, runs1[0], runs0[0])
    rb = jnp.where(has_prev, runs1[1], runs0[1])

    def cond(carry):
        jj, _, ra, rb = carry
        alive = jnp.maximum(jnp.max(ra), jnp.max(rb)) > SB_DEAD_LOG2
        return jnp.logical_and(jj <= i, alive)

    def body(carry):
        jj, acc, ra, rb = carry
        c, (ra, rb) = pair(i - jj, [ra, rb], False)
        return jj + 1, acc + c, ra, rb

    _, acc, _, _ = lax.while_loop(cond, body, (jnp.int32(2), acc, ra, rb))
    o_ref[...] = acc


def stick_breaking_mixer(p_sb, batch, seq, heads=SB_HEADS, blk=256):
    w = p_sb.shape[1] // 3
    dh = w // heads
    pw = 2 * dh
    assert pw == LANES
    npair = heads // 2
    nq = seq // blk
    return pl.pallas_call(
        functools.partial(_sb_kernel, dh=dh, blk=blk),
        grid=(batch, npair, nq),
        in_specs=[
            pl.BlockSpec((blk, pw), lambda b, p, i: (b * nq + i, p)),
            pl.BlockSpec((seq, pw), lambda b, p, i: (b, npair + p)),
            pl.BlockSpec((seq, pw), lambda b, p, i: (b, 2 * npair + p)),
        ],
        out_specs=pl.BlockSpec((blk, pw), lambda b, p, i: (b * nq + i, p)),
        out_shape=jax.ShapeDtypeStruct((batch * seq, w), F32),
        compiler_params=_cparams(("parallel", "parallel", "arbitrary")),
        name="stick_breaking_mixer",
    )(p_sb, p_sb, p_sb)


def _mix_cross_kernel(h_ref, ya_ref, yb_ref, yc_ref, gain_ref, wout_ref, cnw_ref, wq_ref, kv_ref,
                      wo_ref, o_ref, *, heads):
    wa = ya_ref.shape[1]
    wb = yb_ref.shape[1]
    d = h_ref.shape[1]
    gain = gain_ref[...]
    ya = _rms(ya_ref[...], gain[:, :wa]).astype(BF16)
    yb = yb_ref[...].astype(BF16)
    yc = _rms(yc_ref[...], gain[:, wa + wb:]).astype(BF16)
    h1 = (h_ref[...] + _dot(ya, wout_ref[:wa, :]) + _dot(yb, wout_ref[wa:wa + wb, :])
          + _dot(yc, wout_ref[wa + wb:, :]))
    q = _dot(_rms(h1, cnw_ref[...]).astype(BF16), wq_ref[...])
    dh = d // heads
    outs = []
    for hd in range(heads):
        qh = q[:, hd * dh:(hd + 1) * dh].astype(BF16)
        kh = kv_ref[:, hd * dh:(hd + 1) * dh]
        vh = kv_ref[:, d + hd * dh:d + (hd + 1) * dh]
        s = _dot_nt(qh, kh) * (dh ** -0.5)
        s = s - jnp.max(s, axis=-1, keepdims=True)
        e = jnp.exp(s)
        p = e / jnp.sum(e, axis=-1, keepdims=True)
        outs.append(_dot(p.astype(BF16), vh).astype(BF16))
    o_ref[...] = h1 + _dot(jnp.concatenate(outs, axis=1), wo_ref[...])


def mix_cross(h, ya, yb, yc, gain, w_out, cnw, wq, kv, wo, layer, batch, seq, heads=CROSS_HEADS, tm=1024):
    t, d = h.shape
    m = kv.shape[0] // batch
    nt = seq // tm
    rows = lambda wd: pl.BlockSpec((tm, wd), lambda i: (i, 0))
    const = lambda shape: pl.BlockSpec(shape, lambda i: (0, 0))
    stack = pl.BlockSpec((None, d, d), lambda i: (layer, 0, 0))
    return pl.pallas_call(
        functools.partial(_mix_cross_kernel, heads=heads),
        grid=(t // tm,),
        in_specs=[rows(d), rows(ya.shape[1]), rows(yb.shape[1]), rows(yc.shape[1]),
                  const((1, d)), stack, const((1, d)), stack,
                  pl.BlockSpec((m, 2 * d), lambda i: (i // nt, 0)),
                  stack],
        out_specs=rows(d),
        out_shape=jax.ShapeDtypeStruct((t, d), F32),
        compiler_params=_cparams(("parallel",)),
        name="mix_cross",
    )(h, ya, yb, yc, gain.reshape(1, d), w_out, cnw.reshape(1, d), wq, kv, wo)


FFN_TM = 1024
FFN_TF = 1408
FFN_CHUNK = 256
FFN_VMEM_MB = 56


def _swiglu_accumulate(x, wg_ref, wu_ref, wd_ref, o_ref):
    tf = wd_ref.shape[0]
    for c0 in range(0, tf, FFN_CHUNK):
        c1 = min(c0 + FFN_CHUNK, tf)
        act = jax.nn.silu(_dot(x, wg_ref[:, c0:c1])) * _dot(x, wu_ref[:, c0:c1])
        o_ref[...] += _dot(act.astype(BF16), wd_ref[c0:c1, :])


def _ffn_dense_kernel(h_ref, nw_ref, wg_ref, wu_ref, wd_ref, o_ref, xn_ref):
    f = pl.program_id(1)

    @pl.when(f == 0)
    def _():
        h = h_ref[...]
        xn_ref[...] = _rms(h, nw_ref[...]).astype(BF16)
        o_ref[...] = h

    _swiglu_accumulate(xn_ref[...], wg_ref, wu_ref, wd_ref, o_ref)


def ffn_dense(h, nw, w_gu, w_down, layer, tm=FFN_TM, tf=FFN_TF):
    t, d = h.shape
    ff = w_down.shape[1]
    nf = ff // tf
    return pl.pallas_call(
        _ffn_dense_kernel,
        grid=(t // tm, nf),
        in_specs=[
            pl.BlockSpec((tm, d), lambda i, f: (i, 0)),
            pl.BlockSpec((1, d), lambda i, f: (0, 0)),
            pl.BlockSpec((None, d, tf), lambda i, f: (layer, 0, f)),
            pl.BlockSpec((None, d, tf), lambda i, f: (layer, 0, nf + f)),
            pl.BlockSpec((None, tf, d), lambda i, f: (layer, f, 0)),
        ],
        out_specs=pl.BlockSpec((tm, d), lambda i, f: (i, 0)),
        out_shape=jax.ShapeDtypeStruct((t, d), F32),
        scratch_shapes=[pltpu.VMEM((tm, d), BF16)],
        compiler_params=_cparams(("parallel", "arbitrary"), FFN_VMEM_MB),
        name="ffn_dense",
    )(h, nw.reshape(1, d), w_gu, w_gu, w_down)


def _ffn_group_kernel(te_ref, nv_ref, rows_ref, x_ref, wg_ref, wu_ref, wd_ref, y_ref, xb_ref, acc_ref):
    i = pl.program_id(0)
    f = pl.program_id(1)
    valid = i < nv_ref[0]

    @pl.when(f == 0)
    def _():
        row = lax.broadcasted_iota(jnp.int32, (x_ref.shape[0], 1), 0)
        xb_ref[...] = jnp.where(row < rows_ref[i], _unpack_bf16_pairs(x_ref[...]), 0.0).astype(BF16)
        acc_ref[...] = jnp.zeros_like(acc_ref)

    @pl.when(valid)
    def _():
        _swiglu_accumulate(xb_ref[...], wg_ref.at[0], wu_ref.at[0], wd_ref.at[0], acc_ref)

    @pl.when(f == pl.num_programs(1) - 1)
    def _():
        half = acc_ref.shape[1] // 2
        y_ref[...] = _pack_bf16_pairs(acc_ref[:, :half], acc_ref[:, half:])


def ffn_grouped(xs, tile_expert, n_valid, tile_rows, w_gu, w_down, expert_base, tm=FFN_TM, tf=FFN_TF):
    p = xs.shape[0]
    d = 2 * xs.shape[1]
    ff = w_down.shape[1]
    nf = ff // tf

    def live(i, f, nv):
        ok = i < nv[0]
        return jnp.where(ok, i, nv[0] - 1), jnp.where(ok, f, nf - 1)

    def x_map(i, f, te, nv, rows):
        ii, _ = live(i, f, nv)
        return ii, 0

    def wg_map(i, f, te, nv, rows):
        ii, ff_ = live(i, f, nv)
        return expert_base + te[ii], 0, ff_

    def wu_map(i, f, te, nv, rows):
        ii, ff_ = live(i, f, nv)
        return expert_base + te[ii], 0, nf + ff_

    def wd_map(i, f, te, nv, rows):
        ii, ff_ = live(i, f, nv)
        return expert_base + te[ii], ff_, 0

    grid_spec = pltpu.PrefetchScalarGridSpec(
        num_scalar_prefetch=3,
        grid=(p // tm, nf),
        in_specs=[
            pl.BlockSpec((tm, d // 2), x_map),
            pl.BlockSpec((1, d, tf), wg_map),
            pl.BlockSpec((1, d, tf), wu_map),
            pl.BlockSpec((1, tf, d), wd_map),
        ],
        out_specs=pl.BlockSpec((tm, d // 2), lambda i, f, te, nv, rows: (i, 0)),
        scratch_shapes=[pltpu.VMEM((tm, d), BF16), pltpu.VMEM((tm, d), F32)],
    )
    return pl.pallas_call(
        _ffn_group_kernel,
        grid_spec=grid_spec,
        out_shape=jax.ShapeDtypeStruct((p, d // 2), jnp.uint32),
        compiler_params=_cparams(("arbitrary", "arbitrary"), FFN_VMEM_MB),
        name="ffn_grouped",
    )(tile_expert, n_valid, tile_rows, xs, w_gu, w_gu, w_down)


V7X_SC_CORES = 2
V7X_SC_SUBCORES = 16
V7X_SC_WORKERS = V7X_SC_CORES * V7X_SC_SUBCORES


def _sc_mesh():
    return plsc.VectorSubcoreMesh(core_axis_name="c", subcore_axis_name="s")


def sc_scatter_rows2(x, idx1, idx2, n_out, ch=128):
    t, d = x.shape
    assert t % (V7X_SC_WORKERS * ch) == 0
    t_per_w = t // V7X_SC_WORKERS
    n_ch = t_per_w // ch

    @functools.partial(
        pl.kernel, mesh=_sc_mesh(), out_type=jax.ShapeDtypeStruct((n_out, d), x.dtype),
        scratch_types=[pltpu.VMEM((ch,), jnp.int32), pltpu.VMEM((ch,), jnp.int32),
                       pltpu.VMEM((ch, d), x.dtype), pltpu.SemaphoreType.DMA])
    def k(x_hbm, i1_hbm, i2_hbm, out_hbm, i1_v, i2_v, rows_v, sem):
        wid = lax.axis_index("s") * V7X_SC_CORES + lax.axis_index("c")
        base = wid * t_per_w

        @pl.loop(0, n_ch)
        def _(j):
            off = pl.multiple_of(base + j * ch, 8)
            pltpu.sync_copy(i1_hbm.at[pl.ds(off, ch)], i1_v)
            pltpu.sync_copy(i2_hbm.at[pl.ds(off, ch)], i2_v)
            pltpu.sync_copy(x_hbm.at[pl.ds(off, ch)], rows_v)
            pltpu.async_copy(rows_v, out_hbm.at[i1_v], sem).wait()
            pltpu.async_copy(rows_v, out_hbm.at[i2_v], sem).wait()

    return k(x, idx1, idx2)


def sc_gather_rows(table, idx, ch=128):
    _, d = table.shape
    b = idx.shape[0]
    assert b % (V7X_SC_WORKERS * ch) == 0
    b_per_w = b // V7X_SC_WORKERS
    n_ch = b_per_w // ch

    @functools.partial(
        pl.kernel, mesh=_sc_mesh(), out_type=jax.ShapeDtypeStruct((b, d), table.dtype),
        scratch_types=[pltpu.VMEM((ch,), jnp.int32), pltpu.VMEM((ch, d), table.dtype),
                       pltpu.SemaphoreType.DMA])
    def k(table_hbm, idx_hbm, out_hbm, idx_v, rows_v, sem):
        wid = lax.axis_index("s") * V7X_SC_CORES + lax.axis_index("c")
        base = wid * b_per_w

        @pl.loop(0, n_ch)
        def _(j):
            off = pl.multiple_of(base + j * ch, 8)
            pltpu.sync_copy(idx_hbm.at[pl.ds(off, ch)], idx_v)
            pltpu.async_copy(table_hbm.at[idx_v], rows_v, sem).wait()
            pltpu.sync_copy(rows_v, out_hbm.at[pl.ds(off, ch)])

    return k(table, idx)


def _moe_combine_kernel(h_ref, y1_ref, y2_ref, g1_ref, g2_ref, nw_ref, o_ref, *, normed):
    out = (h_ref[...] + g1_ref[...] * _unpack_bf16_pairs(y1_ref[...])
           + g2_ref[...] * _unpack_bf16_pairs(y2_ref[...]))
    o_ref[...] = _rms(out, nw_ref[...]) if normed else out


def moe_combine(h, yg, gate1, gate2, norm_w=None, tm=1024):
    t, d = h.shape
    nt = t // tm
    row = pl.BlockSpec((tm, d), lambda i: (i, 0))
    col = pl.BlockSpec((tm, 1), lambda i: (i, 0))
    nw = jnp.ones((d,), F32) if norm_w is None else norm_w
    return pl.pallas_call(
        functools.partial(_moe_combine_kernel, normed=norm_w is not None),
        grid=(nt,),
        in_specs=[row, pl.BlockSpec((tm, d // 2), lambda i: (i, 0)),
                  pl.BlockSpec((tm, d // 2), lambda i: (nt + i, 0)), col, col,
                  pl.BlockSpec((1, d), lambda i: (0, 0))],
        out_specs=row,
        out_shape=jax.ShapeDtypeStruct((t, d), F32),
        compiler_params=_cparams(("parallel",)),
        name="moe_combine",
    )(h, yg, yg, gate1, gate2, nw.reshape(1, d))


def _router_kernel(h_ref, nw_ref, wr_ref, xn_ref, slot_ref, gate_ref, *, n_exp):
    xn = _rms(h_ref[...], nw_ref[...])
    half = xn.shape[1] // 2
    xn_ref[...] = _pack_bf16_pairs(xn[:, :half], xn[:, half:])
    xh = xn.astype(BF16)
    xl = (xn - xh.astype(F32)).astype(BF16)
    wr = wr_ref[...]
    wh = wr.astype(BF16)
    wl = (wr - wh.astype(F32)).astype(BF16)
    logits = _dot(xh, wh) + _dot(xh, wl) + _dot(xl, wh)
    lane = lax.broadcasted_iota(jnp.int32, logits.shape, 1)
    neg = -jnp.inf
    lg = jnp.where(lane < n_exp, logits, neg)
    m1 = jnp.max(lg, axis=-1, keepdims=True)
    i1 = jnp.min(jnp.where(lg == m1, lane, LANES), axis=-1, keepdims=True)
    lg2 = jnp.where(lane == i1, neg, lg)
    m2 = jnp.max(lg2, axis=-1, keepdims=True)
    i2 = jnp.min(jnp.where(lg2 == m2, lane, LANES), axis=-1, keepdims=True)
    e2 = jnp.exp(m2 - m1)
    g1 = 1.0 / (1.0 + e2)
    g2 = e2 / (1.0 + e2)
    slot = jnp.where(lane == i1, 1, jnp.where(lane == i2, 2, 0))
    gate = jnp.where(lane == i1, g1, jnp.where(lane == i2, g2, 0.0))
    slot_ref[...] = slot[:, :n_exp]
    gate_ref[...] = gate[:, :n_exp]


def router(h, nw, w_router, tm=512):
    t, d = h.shape
    n_exp = w_router.shape[1]
    wr = jnp.pad(w_router, ((0, 0), (0, LANES - n_exp)))
    return pl.pallas_call(
        functools.partial(_router_kernel, n_exp=n_exp),
        grid=(t // tm,),
        in_specs=[pl.BlockSpec((tm, d), lambda i: (i, 0)),
                  pl.BlockSpec((1, d), lambda i: (0, 0)),
                  pl.BlockSpec((d, LANES), lambda i: (0, 0))],
        out_specs=[pl.BlockSpec((tm, d // 2), lambda i: (i, 0)),
                   pl.BlockSpec((tm, n_exp), lambda i: (i, 0)),
                   pl.BlockSpec((tm, n_exp), lambda i: (i, 0))],
        out_shape=[jax.ShapeDtypeStruct((t, d // 2), jnp.uint32),
                   jax.ShapeDtypeStruct((t, n_exp), jnp.int32),
                   jax.ShapeDtypeStruct((t, n_exp), F32)],
        compiler_params=_cparams(("parallel",)),
        name="router",
    )(h, nw.reshape(1, d), wr)


def _dispatch_plan(slot, gate, tm):
    t, n_exp = slot.shape
    sel = (slot > 0).astype(jnp.int32)
    counts = jnp.sum(sel, axis=0)
    rank = jnp.cumsum(sel, axis=0) - sel
    tiles = (counts + tm - 1) // tm
    tile_end = jnp.cumsum(tiles)
    start = (tile_end - tiles) * tm
    pos = start[None, :] + rank
    n_rows = 2 * t + n_exp * tm
    n_tiles = n_rows // tm
    pos1 = jnp.sum(jnp.where(slot == 1, pos, 0), axis=1)
    pos2 = jnp.sum(jnp.where(slot == 2, pos, 0), axis=1)
    gate1 = jnp.sum(jnp.where(slot == 1, gate, 0.0), axis=1, keepdims=True)
    gate2 = jnp.sum(jnp.where(slot == 2, gate, 0.0), axis=1, keepdims=True)
    tile_id = jnp.arange(n_tiles, dtype=jnp.int32)
    owner = tile_id[:, None] >= tile_end[None, :]
    tile_expert = jnp.minimum(jnp.sum(owner.astype(jnp.int32), axis=1), n_exp - 1)
    mine = tile_expert[:, None] == jnp.arange(n_exp, dtype=jnp.int32)[None, :]
    group_end = jnp.sum(jnp.where(mine, (start + counts)[None, :], 0), axis=1)
    n_valid = tile_end[-1:].astype(jnp.int32)
    tile_rows = jnp.where(tile_id < n_valid[0], jnp.clip(group_end - tile_id * tm, 0, tm), 0)
    return (n_rows, pos1, pos2, gate1, gate2, tile_expert.astype(jnp.int32), n_valid,
            tile_rows.astype(jnp.int32))


def _final_norm_kernel(h_ref, w_ref, o_ref):
    o_ref[...] = _rms(h_ref[...], w_ref[...])


def final_norm(h, w, tm=1024):
    t, d = h.shape
    return pl.pallas_call(
        _final_norm_kernel,
        grid=(t // tm,),
        in_specs=[pl.BlockSpec((tm, d), lambda i: (i, 0)), pl.BlockSpec((1, d), lambda i: (0, 0))],
        out_specs=pl.BlockSpec((tm, d), lambda i: (i, 0)),
        out_shape=jax.ShapeDtypeStruct((t, d), F32),
        compiler_params=_cparams(("parallel",)),
        name="final_norm",
    )(h, w.reshape(1, d))


def kernel(x, mem, mix_norm_w, w_in, ssm_lambda_re, ssm_lambda_im, ssm_b_re, ssm_b_im, ssm_c_re, ssm_c_im, ssm_d, ssm_log_dt, ssm_w_glu, mix_out_gain, w_out, cross_norm_w, mem_norm_w, w_cross_q, w_cross_kv, w_cross_o, ffn_norm_w, w_dense_gu, w_dense_down, w_router, w_expert_gu, w_expert_down, final_norm_w):
    batch, seq, d = x.shape
    depth = w_in.shape[0]
    mem_len = mem.shape[1]
    ssm_w = ssm_d.shape[1]
    g, p = ssm_lambda_re.shape[1:]
    nh = ssm_b_re.shape[-1]
    ret_w = ssm_w
    sb_w = d - ssm_w - ret_w
    ret_dh = ret_w // RET_HEADS
    moe_tm = FFN_TM

    perm = _rot_perm(RET_HEADS, ret_dh)
    cols = np.arange(w_in.shape[2])
    cols[ssm_w:ssm_w + ret_w] = ssm_w + perm
    cols[ssm_w + ret_w:ssm_w + 2 * ret_w] = ssm_w + ret_w + perm
    w_in_b = w_in[:, :, cols].astype(BF16)
    w_out_b = w_out.astype(BF16)
    w_q_b = w_cross_q.astype(BF16)
    w_kv_b = w_cross_kv.astype(BF16)
    w_o_b = w_cross_o.astype(BF16)
    w_dgu_b = w_dense_gu.astype(BF16)
    w_ddn_b = w_dense_down.astype(BF16)
    n_exp = w_expert_gu.shape[1]
    w_egu_b = w_expert_gu.astype(BF16).reshape((-1,) + w_expert_gu.shape[2:])
    w_edn_b = w_expert_down.astype(BF16).reshape((-1,) + w_expert_down.shape[2:])
    n_a = ssm_w + 4 * ret_w

    apow, bbc = s5_prep(ssm_lambda_re, ssm_lambda_im, ssm_log_dt, ssm_b_re, ssm_b_im,
                        nstep=S5_TIME_TILE // SUBLANES)

    mem2 = mem.reshape(batch * mem_len, d)
    h = x.reshape(batch * seq, d)
    for i in range(depth):
        p_a, p_sb = norm_proj(h, mix_norm_w[i], w_in_b, i, (n_a, 3 * sb_w), (F32, BF16))
        bb = _block_diag_in(bbc[i], g, p, nh).astype(BF16)
        cc = _block_diag_out(ssm_c_re[i], ssm_c_im[i]).astype(BF16)
        y_a = s5_mixer(p_a, bb, cc, apow[i], ssm_d[i], ssm_w_glu[i].astype(BF16), batch, seq)
        gain = mix_out_gain[i]
        y_b = retention_mixer(p_a, ssm_w, gain[ssm_w:ssm_w + ret_w], batch, seq)
        y_c = stick_breaking_mixer(p_sb, batch, seq)
        (kv,) = norm_proj(mem2, mem_norm_w, w_kv_b, i, (2 * d,), (BF16,))
        h = mix_cross(h, y_a, y_b, y_c, gain, w_out_b, cross_norm_w[i], w_q_b, kv, w_o_b, i, batch, seq)
        if i % 2 == 0:
            h = ffn_dense(h, ffn_norm_w[i], w_dgu_b, w_ddn_b, i // 2)
        else:
            xn, slot, gate = router(h, ffn_norm_w[i], w_router[i // 2])
            n_rows, pos1, pos2, gate1, gate2, tile_expert, n_valid, tile_rows = _dispatch_plan(
                slot, gate, moe_tm)
            xs = sc_scatter_rows2(xn, pos1, pos2, n_rows)
            y = ffn_grouped(xs, tile_expert, n_valid, tile_rows, w_egu_b, w_edn_b, (i // 2) * n_exp,
                            tm=moe_tm)
            yg = sc_gather_rows(y, jnp.concatenate([pos1, pos2]))
            last = i == depth - 1
            h = moe_combine(h, yg, gate1, gate2, norm_w=final_norm_w if last else None)
    if depth % 2 == 1:
        h = final_norm(h, final_norm_w)
    return h.reshape(batch, seq, d)
```

```python
import functools
import math

import numpy as np
import jax
import jax.numpy as jnp
from jax import lax
from jax.experimental import pallas as pl
from jax.experimental.pallas import tpu as pltpu
from jax.experimental.pallas import tpu_sc as plsc

F32 = jnp.float32
BF16 = jnp.bfloat16

NORM_EPS = 1e-6
GN_EPS = 1e-6
ROPE_BASE = 10000.0
LOG2_E = 1.4426950408889634
SB_DEAD_LOG2 = -150.0
SB_MASKED_LOG2 = -1e30

SSM_GROUP = 16
SSM_STATE = 64
RET_HEADS = 4
SB_HEADS = 8
CROSS_HEADS = 4
N_EXPERTS = 8

V7X_VMEM_BYTES = 64 * 1024 * 1024
LANES = 128
SUBLANES = 8


def _cparams(sem, vmem_mb=48):
    return pltpu.CompilerParams(dimension_semantics=sem, vmem_limit_bytes=vmem_mb * 1024 * 1024)


def _rms(x, w):
    ms = jnp.mean(x * x, axis=-1, keepdims=True)
    return x * lax.rsqrt(ms + NORM_EPS) * w


def _dot(a, b):
    return jnp.dot(a, b, preferred_element_type=F32)


def _dot_nt(a, b):
    return lax.dot_general(a, b, (((1,), (1,)), ((), ())), preferred_element_type=F32)


def _dot_tn(a, b):
    return lax.dot_general(a, b, (((0,), (0,)), ((), ())), preferred_element_type=F32)


def _pack_bf16_pairs(a, b):
    ua = lax.bitcast_convert_type(a.astype(BF16).astype(F32), jnp.uint32)
    ub = lax.bitcast_convert_type(b.astype(BF16).astype(F32), jnp.uint32)
    return (ub & jnp.uint32(0xFFFF0000)) | (ua >> 16)


def _unpack_bf16_pairs(p):
    lo = lax.bitcast_convert_type(p << 16, F32)
    hi = lax.bitcast_convert_type(p & jnp.uint32(0xFFFF0000), F32)
    return jnp.concatenate([lo, hi], axis=1)


def _split_dot(x, m_bf16):
    hi = x.astype(BF16)
    lo = (x - hi.astype(F32)).astype(BF16)
    return _dot(hi, m_bf16) + _dot(lo, m_bf16)


def _norm_proj_kernel(x_ref, nw_ref, w_ref, *o_refs, splits, chunk):
    xn = _rms(x_ref[...], nw_ref[...]).astype(BF16)
    col = 0
    for o_ref, width in zip(o_refs, splits):
        for c0 in range(0, width, chunk):
            r = _dot(xn, w_ref[:, col + c0:col + c0 + chunk])
            o_ref[:, c0:c0 + chunk] = r.astype(o_ref.dtype)
        col += width


def norm_proj(x, nw, w, layer, splits, dtypes, tm=512, chunk=256):
    rows, d = x.shape
    n = w.shape[2]
    assert sum(splits) == n and rows % tm == 0
    return pl.pallas_call(
        functools.partial(_norm_proj_kernel, splits=splits, chunk=chunk),
        grid=(rows // tm,),
        in_specs=[
            pl.BlockSpec((tm, d), lambda i: (i, 0)),
            pl.BlockSpec((1, d), lambda i: (0, 0)),
            pl.BlockSpec((None, d, n), lambda i: (layer, 0, 0)),
        ],
        out_specs=[pl.BlockSpec((tm, s), lambda i: (i, 0)) for s in splits],
        out_shape=[jax.ShapeDtypeStruct((rows, s), dt) for s, dt in zip(splits, dtypes)],
        compiler_params=_cparams(("parallel",)),
        name="norm_proj",
    )(x, nw.reshape(1, d), w)


def _s5_prep_kernel(lr_ref, li_ref, ldt_ref, br_ref, bi_ref, apow_ref, bb_ref):
    lr = lr_ref[0]
    li = li_ref[0]
    dt = jnp.exp(ldt_ref[0])
    mag = jnp.exp(lr * dt)
    a_re = mag * jnp.cos(li * dt)
    a_im = mag * jnp.sin(li * dt)
    denom = lr * lr + li * li
    nr = a_re - 1.0
    z_re = (nr * lr + a_im * li) / denom
    z_im = (a_im * lr - nr * li) / denom
    br = br_ref[0]
    bi = bi_ref[0]
    nh = br.shape[0]
    bb_ref[0, :nh, :] = z_re * br - z_im * bi
    bb_ref[0, nh:, :] = z_re * bi + z_im * br
    nstep = (apow_ref.shape[1] - 2 * S5_SEG_LEVELS) // 2
    pr, pi = a_re, a_im
    for j in range(nstep):
        apow_ref[0, j:j + 1, :] = pr
        apow_ref[0, nstep + j:nstep + j + 1, :] = pi
        if j + 1 < nstep:
            pr, pi = pr * a_re - pi * a_im, pr * a_im + pi * a_re
    for k in range(S5_SEG_LEVELS):
        apow_ref[0, 2 * nstep + k:2 * nstep + k + 1, :] = pr
        apow_ref[0, 2 * nstep + S5_SEG_LEVELS + k:2 * nstep + S5_SEG_LEVELS + k + 1, :] = pi
        pr, pi = pr * pr - pi * pi, 2.0 * pr * pi


S5_SEG_LEVELS = 3
S5_TIME_TILE = 512


def s5_prep(lam_re, lam_im, log_dt, b_re, b_im, nstep):
    depth, g, p = lam_re.shape
    nh = b_re.shape[-1]
    gp = g * p
    lr = lam_re.reshape(depth, 1, gp)
    li = lam_im.reshape(depth, 1, gp)
    ldt = jnp.repeat(log_dt, p, axis=1).reshape(depth, 1, gp)
    br = b_re.transpose(0, 3, 1, 2).reshape(depth, nh, gp)
    bi = b_im.transpose(0, 3, 1, 2).reshape(depth, nh, gp)
    vec = pl.BlockSpec((1, 1, gp), lambda i: (i, 0, 0))
    mat = pl.BlockSpec((1, nh, gp), lambda i: (i, 0, 0))
    npow = 2 * (nstep + S5_SEG_LEVELS)
    return pl.pallas_call(
        _s5_prep_kernel,
        grid=(depth,),
        in_specs=[vec, vec, vec, mat, mat],
        out_specs=[pl.BlockSpec((1, npow, gp), lambda i: (i, 0, 0)),
                   pl.BlockSpec((1, 2 * nh, gp), lambda i: (i, 0, 0))],
        out_shape=[jax.ShapeDtypeStruct((depth, npow, gp), F32),
                   jax.ShapeDtypeStruct((depth, 2 * nh, gp), F32)],
        compiler_params=_cparams(("arbitrary",)),
        name="s5_prep",
    )(lr, li, ldt, br, bi)


def _s5_kernel(u_ref, bb_ref, cc_ref, apow_ref, d_ref, wglu_ref, o_ref, xs_ref, carry_ref, perm_ref,
               *, gp):
    ts = u_ref.shape[0]
    nstep = ts // SUBLANES
    rows = lambda s: slice(s * SUBLANES, (s + 1) * SUBLANES)

    @pl.when(pl.program_id(1) == 0)
    def _():
        carry_ref[...] = jnp.zeros_like(carry_ref)

    ntile = u_ref.shape[1] // LANES
    for c in range(ntile):
        perm_ref[c] = u_ref[:, c * LANES:(c + 1) * LANES]
    u = jnp.concatenate(
        [jnp.concatenate([perm_ref[c, pl.ds(s, SUBLANES, stride=nstep), :] for c in range(ntile)], axis=1)
         for s in range(nstep)], axis=0)
    xs_ref[...] = _dot(u.astype(BF16), bb_ref[...])

    ar = apow_ref[0:1, :]
    ai = apow_ref[nstep:nstep + 1, :]
    xr = xs_ref[rows(0), :gp]
    xi = xs_ref[rows(0), gp:]
    for s in range(1, nstep):
        xr, xi = (ar * xr - ai * xi + xs_ref[rows(s), :gp], ar * xi + ai * xr + xs_ref[rows(s), gp:])
        xs_ref[rows(s), :gp] = xr
        xs_ref[rows(s), gp:] = xi

    seg = lax.broadcasted_iota(jnp.int32, (SUBLANES, 1), 0)
    cr = carry_ref[0:1, :]
    ci = carry_ref[1:2, :]
    lr = apow_ref[2 * nstep:2 * nstep + 1, :]
    li = apow_ref[2 * nstep + S5_SEG_LEVELS:2 * nstep + S5_SEG_LEVELS + 1, :]
    first = seg == 0
    xr = xr + jnp.where(first, lr * cr - li * ci, 0.0)
    xi = xi + jnp.where(first, lr * ci + li * cr, 0.0)
    for k in range(S5_SEG_LEVELS):
        sh = 1 << k
        pr = apow_ref[2 * nstep + k:2 * nstep + k + 1, :]
        pi = apow_ref[2 * nstep + S5_SEG_LEVELS + k:2 * nstep + S5_SEG_LEVELS + k + 1, :]
        keep = seg >= sh
        sr = jnp.where(keep, pltpu.roll(xr, sh, axis=0), 0.0)
        si = jnp.where(keep, pltpu.roll(xi, sh, axis=0), 0.0)
        xr, xi = xr + pr * sr - pi * si, xi + pr * si + pi * sr
    carry_ref[0:1, :] = xr[SUBLANES - 1:, :]
    carry_ref[1:2, :] = xi[SUBLANES - 1:, :]
    enter_r = jnp.where(first, cr, pltpu.roll(xr, 1, axis=0))
    enter_i = jnp.where(first, ci, pltpu.roll(xi, 1, axis=0))

    for s in range(nstep):
        pr = apow_ref[s:s + 1, :]
        pi = apow_ref[nstep + s:nstep + s + 1, :]
        xs_ref[rows(s), :gp] += pr * enter_r - pi * enter_i
        xs_ref[rows(s), gp:] += pr * enter_i + pi * enter_r

    y = _dot(xs_ref[...].astype(BF16), cc_ref[...]) + d_ref[...] * u
    g = jax.nn.gelu(y)
    out = g * jax.nn.sigmoid(_dot(g.astype(BF16), wglu_ref[...]))
    for s in range(nstep):
        for c in range(ntile):
            perm_ref[c, pl.ds(s, SUBLANES, stride=nstep), :] = out[rows(s), c * LANES:(c + 1) * LANES]
    for c in range(ntile):
        o_ref[:, c * LANES:(c + 1) * LANES] = perm_ref[c]


def s5_mixer(p_a, bb, cc, apow, d_skip, w_glu, batch, seq):
    w = d_skip.shape[-1]
    gp = apow.shape[-1]
    ts = (apow.shape[0] // 2 - S5_SEG_LEVELS) * SUBLANES
    nt = seq // ts
    return pl.pallas_call(
        functools.partial(_s5_kernel, gp=gp),
        grid=(batch, nt),
        in_specs=[
            pl.BlockSpec((ts, w), lambda b, t: (b * nt + t, 0)),
            pl.BlockSpec((w, 2 * gp), lambda b, t: (0, 0)),
            pl.BlockSpec((2 * gp, w), lambda b, t: (0, 0)),
            pl.BlockSpec(apow.shape, lambda b, t: (0, 0)),
            pl.BlockSpec((1, w), lambda b, t: (0, 0)),
            pl.BlockSpec((w, w), lambda b, t: (0, 0)),
        ],
        out_specs=pl.BlockSpec((ts, w), lambda b, t: (b * nt + t, 0)),
        out_shape=jax.ShapeDtypeStruct((batch * seq, w), F32),
        scratch_shapes=[pltpu.VMEM((ts, 2 * gp), F32), pltpu.VMEM((2, gp), F32),
                        pltpu.VMEM((w // LANES, ts, LANES), F32)],
        compiler_params=_cparams(("parallel", "arbitrary")),
        name="s5_mixer",
    )(p_a, bb, cc, apow, d_skip.reshape(1, w), w_glu)


def _block_diag_in(bb, g, p, nh):
    eye = jnp.eye(g, dtype=F32)
    def one(m):
        m = m.reshape(nh, g, p)
        return jnp.einsum('hgp,kg->khgp', m, eye).reshape(g * nh, g * p)
    return jnp.concatenate([one(bb[:nh]), one(bb[nh:])], axis=1)


def _block_diag_out(c_re, c_im):
    g, nh, p = c_re.shape
    eye = jnp.eye(g, dtype=F32)
    def one(m):
        return jnp.einsum('ghp,gk->gpkh', m, eye).reshape(g * p, g * nh)
    return jnp.concatenate([one(c_re), one(-c_im)], axis=0)


def _ret_tables(seq, chunk, heads, dh):
    half = dh // 2
    w = heads * dh
    hw = w // 2
    inv_freq = ROPE_BASE ** (-np.arange(half, dtype=np.float64) / half)
    ang = np.arange(seq, dtype=np.float64)[:, None] * np.tile(inv_freq, heads)[None, :]
    cos = np.cos(ang).astype(np.float32)
    sin = np.sin(ang).astype(np.float32)
    log_gamma = np.log1p(-(2.0 ** (-5.0 - np.arange(heads, dtype=np.float64))))
    idx = np.arange(chunk, dtype=np.float64)
    rel = idx[:, None] - idx[None, :]
    intra = np.where(rel >= 0, np.exp(log_gamma[:, None, None] * np.maximum(rel, 0.0)), 0.0)
    head_nat = np.arange(w) // dh
    head_rot = (np.arange(w) % hw) // half
    xi = np.exp(log_gamma[head_nat][None, :] * (idx[:, None] + 1.0))
    zeta = np.exp(log_gamma[head_rot][None, :] * (chunk - 1.0 - idx[:, None]))
    decay = np.exp(log_gamma[head_nat] * chunk)[None, :]
    same = (head_rot[:, None] == head_nat[None, :]).astype(np.float32)
    gavg = (head_nat[:, None] == head_nat[None, :]).astype(np.float32) / dh
    f = lambda a: jnp.asarray(a, dtype=F32)
    return dict(cos=f(cos), sin=f(sin), intra=f(intra), xi=f(xi), zeta=f(zeta), decay=f(decay),
                same=f(same), gavg=jnp.asarray(gavg, dtype=BF16))


def _ret_kernel(q_ref, k_ref, v_ref, g_ref, cos_ref, sin_ref, intra_ref, xi_ref, zeta_ref,
                decay_ref, same_ref, gavg_ref, gnw_ref, o_ref, state_ref, *, heads, dh):
    w = heads * dh
    hw = w // 2
    half = dh // 2

    @pl.when(pl.program_id(1) == 0)
    def _():
        state_ref[...] = jnp.zeros_like(state_ref)

    cos = cos_ref[...]
    sin = sin_ref[...]

    def rot(x):
        x1, x2 = x[:, :hw], x[:, hw:]
        return jnp.concatenate([x1 * cos - x2 * sin, x1 * sin + x2 * cos], axis=1)

    q = rot(q_ref[...])
    k = rot(k_ref[...]) * (dh ** -0.5)
    v = v_ref[...]
    qb = q.astype(BF16)
    kb = k.astype(BF16)
    vb = v.astype(BF16)
    lane = lax.broadcasted_iota(jnp.int32, (1, w), 1)
    head_rot = (lane % hw) // half
    head_nat = lane // dh

    o = _dot(qb, state_ref[...].astype(BF16)) * xi_ref[...]
    for h in range(heads):
        qh = jnp.where(head_rot == h, qb, jnp.zeros_like(qb))
        vh = jnp.where(head_nat == h, vb, jnp.zeros_like(vb))
        s = _dot_nt(qh, kb) * intra_ref[h]
        o = o + _dot(s.astype(BF16), vh)
    kz = (k * zeta_ref[...]).astype(BF16)
    state_ref[...] = decay_ref[...] * state_ref[...] + _dot_tn(kz, vb) * same_ref[...]

    gavg = gavg_ref[...]
    mu = _split_dot(o, gavg)
    d = o - mu
    var = _split_dot(d * d, gavg)
    on = d * lax.rsqrt(var + GN_EPS) * gnw_ref[...]
    o_ref[...] = jax.nn.silu(g_ref[...]) * on


def retention_mixer(p_a, col0, gn_w, batch, seq, heads=RET_HEADS, chunk=256):
    w = gn_w.shape[-1]
    dh = w // heads
    nc = seq // chunk
    tb = _ret_tables(seq, chunk, heads, dh)
    cb = col0 // w
    assert col0 % w == 0
    row = lambda j: pl.BlockSpec((chunk, w), lambda b, c, j=j: (b * nc + c, cb + j))
    const = lambda shape: pl.BlockSpec(shape, lambda b, c: (0,) * len(shape))
    return pl.pallas_call(
        functools.partial(_ret_kernel, heads=heads, dh=dh),
        grid=(batch, nc),
        in_specs=[row(0), row(1), row(2), row(3),
                  pl.BlockSpec((chunk, w // 2), lambda b, c: (c, 0)),
                  pl.BlockSpec((chunk, w // 2), lambda b, c: (c, 0)),
                  const((heads, chunk, chunk)), const((chunk, w)), const((chunk, w)),
                  const((1, w)), const((w, w)), const((w, w)), const((1, w))],
        out_specs=pl.BlockSpec((chunk, w), lambda b, c: (b * nc + c, 0)),
        out_shape=jax.ShapeDtypeStruct((batch * seq, w), F32),
        scratch_shapes=[pltpu.VMEM((w, w), F32)],
        compiler_params=_cparams(("parallel", "arbitrary")),
        name="retention_mixer",
    )(p_a, p_a, p_a, p_a, tb["cos"], tb["sin"], tb["intra"], tb["xi"], tb["zeta"], tb["decay"],
      tb["same"], tb["gavg"], gn_w.reshape(1, w))


def _rot_perm(heads, dh):
    half = dh // 2
    first = (np.arange(heads)[:, None] * dh + np.arange(half)[None, :]).reshape(-1)
    return np.concatenate([first, first + half])


def _sb_kernel(q_ref, k_ref, v_ref, o_ref, *, dh, blk, nsub):
    step = pl.program_id(2)
    lane = lax.broadcasted_iota(jnp.int32, (1, 2 * dh), 1)
    r_id = lax.broadcasted_iota(jnp.int32, (blk, blk), 0)
    c_id = lax.broadcasted_iota(jnp.int32, (blk, blk), 1)
    neg_upper = jnp.where(r_id > c_id, -1.0, 0.0).astype(BF16)
    causal = c_id < r_id
    zscale = (dh ** -0.5) * LOG2_E
    hsel = [(lane // dh) == hh for hh in range(2)]

    def pair(qm, j, runs, diag):
        off = pl.multiple_of(j * blk, blk)
        kb = k_ref[pl.ds(off, blk), :]
        vb = v_ref[pl.ds(off, blk), :]
        wgts, vms, new_runs = [], [], []
        for hh in range(2):
            zs = _dot_nt(qm[hh], kb) * zscale
            nfail = jnp.maximum(zs, 0.0) + jnp.log2(1.0 + jnp.exp2(jnp.minimum(zs, -zs)))
            if diag:
                nfail = jnp.where(causal, nfail, 0.0)
            after = _dot(nfail.astype(BF16), neg_upper) + runs[hh]
            wgt = jnp.exp2((zs - nfail) + after)
            if diag:
                wgt = jnp.where(causal, wgt, 0.0)
            wgts.append(wgt.astype(BF16))
            vms.append(jnp.where(hsel[hh], vb, jnp.zeros_like(vb)))
            new_runs.append(runs[hh] - jnp.sum(nfail, axis=1, keepdims=True))
        contrib = _dot(jnp.concatenate(wgts, axis=1), jnp.concatenate(vms, axis=0))
        return contrib, new_runs

    zero = jnp.zeros((blk, 1), F32)
    started = []
    for s in range(nsub):
        i = step * nsub + s
        q2 = q_ref[s * blk:(s + 1) * blk, :]
        qm = [jnp.where(hs, q2, jnp.zeros_like(q2)) for hs in hsel]
        acc0, runs0 = pair(qm, i, [zero, zero], True)
        acc1, runs1 = pair(qm, jnp.maximum(i - 1, 0), runs0, False)
        has_prev = i >= 1
        acc = acc0 + jnp.where(has_prev, acc1, 0.0)
        ra = jnp.where(has_prev, runs1[0], runs0[0])
        rb = jnp.where(has_prev, runs1[1], runs0[1])
        started.append((i, qm, acc, ra, rb))

    for s, (i, qm, acc, ra, rb) in enumerate(started):
        def cond(carry, i=i):
            jj, _, ra, rb = carry
            alive = jnp.maximum(jnp.max(ra), jnp.max(rb)) > SB_DEAD_LOG2
            return jnp.logical_and(jj <= i, alive)

        def body(carry, i=i, qm=qm):
            jj, acc, ra, rb = carry
            c, (ra, rb) = pair(qm, i - jj, [ra, rb], False)
            return jj + 1, acc + c, ra, rb

        _, acc, _, _ = lax.while_loop(cond, body, (jnp.int32(2), acc, ra, rb))
        o_ref[s * blk:(s + 1) * blk, :] = acc


def stick_breaking_mixer(p_sb, batch, seq, heads=SB_HEADS, blk=256, nsub=4):
    w = p_sb.shape[1] // 3
    dh = w // heads
    pw = 2 * dh
    assert pw == LANES
    npair = heads // 2
    nq = seq // (blk * nsub)
    return pl.pallas_call(
        functools.partial(_sb_kernel, dh=dh, blk=blk, nsub=nsub),
        grid=(batch, npair, nq),
        in_specs=[
            pl.BlockSpec((blk * nsub, pw), lambda b, p, i: (b * nq + i, p)),
            pl.BlockSpec((seq, pw), lambda b, p, i: (b, npair + p)),
            pl.BlockSpec((seq, pw), lambda b, p, i: (b, 2 * npair + p)),
        ],
        out_specs=pl.BlockSpec((blk * nsub, pw), lambda b, p, i: (b * nq + i, p)),
        out_shape=jax.ShapeDtypeStruct((batch * seq, w), F32),
        compiler_params=_cparams(("parallel", "parallel", "arbitrary")),
        name="stick_breaking_mixer",
    )(p_sb, p_sb, p_sb)


def _mix_cross_kernel(h_ref, ya_ref, yb_ref, yc_ref, gain_ref, wout_ref, cnw_ref, wq_ref, kv_ref,
                      wo_ref, o_ref, *, heads):
    wa = ya_ref.shape[1]
    wb = yb_ref.shape[1]
    d = h_ref.shape[1]
    gain = gain_ref[...]
    ya = _rms(ya_ref[...], gain[:, :wa]).astype(BF16)
    yb = yb_ref[...].astype(BF16)
    yc = _rms(yc_ref[...], gain[:, wa + wb:]).astype(BF16)
    h1 = (h_ref[...] + _dot(ya, wout_ref[:wa, :]) + _dot(yb, wout_ref[wa:wa + wb, :])
          + _dot(yc, wout_ref[wa + wb:, :]))
    q = _dot(_rms(h1, cnw_ref[...]).astype(BF16), wq_ref[...])
    dh = d // heads
    outs = []
    for hd in range(heads):
        qh = q[:, hd * dh:(hd + 1) * dh].astype(BF16)
        kh = kv_ref[:, hd * dh:(hd + 1) * dh]
        vh = kv_ref[:, d + hd * dh:d + (hd + 1) * dh]
        s = _dot_nt(qh, kh) * (dh ** -0.5)
        s = s - jnp.max(s, axis=-1, keepdims=True)
        e = jnp.exp(s)
        p = e / jnp.sum(e, axis=-1, keepdims=True)
        outs.append(_dot(p.astype(BF16), vh).astype(BF16))
    o_ref[...] = h1 + _dot(jnp.concatenate(outs, axis=1), wo_ref[...])


def mix_cross(h, ya, yb, yc, gain, w_out, cnw, wq, kv, wo, layer, batch, seq, heads=CROSS_HEADS, tm=1024):
    t, d = h.shape
    m = kv.shape[0] // batch
    nt = seq // tm
    rows = lambda wd: pl.BlockSpec((tm, wd), lambda i: (i, 0))
    const = lambda shape: pl.BlockSpec(shape, lambda i: (0, 0))
    stack = pl.BlockSpec((None, d, d), lambda i: (layer, 0, 0))
    return pl.pallas_call(
        functools.partial(_mix_cross_kernel, heads=heads),
        grid=(t // tm,),
        in_specs=[rows(d), rows(ya.shape[1]), rows(yb.shape[1]), rows(yc.shape[1]),
                  const((1, d)), stack, const((1, d)), stack,
                  pl.BlockSpec((m, 2 * d), lambda i: (i // nt, 0)),
                  stack],
        out_specs=rows(d),
        out_shape=jax.ShapeDtypeStruct((t, d), F32),
        compiler_params=_cparams(("parallel",)),
        name="mix_cross",
    )(h, ya, yb, yc, gain.reshape(1, d), w_out, cnw.reshape(1, d), wq, kv, wo)


FFN_TM = 1024
FFN_TF = 1408
FFN_CHUNK = 256
FFN_VMEM_MB = 60


def _swiglu_accumulate(x, wg_ref, wu_ref, wd_ref, o_ref):
    tf = wd_ref.shape[0]
    for c0 in range(0, tf, FFN_CHUNK):
        c1 = min(c0 + FFN_CHUNK, tf)
        wg = wg_ref[:, c0:c1].astype(BF16)
        wu = wu_ref[:, c0:c1].astype(BF16)
        act = jax.nn.silu(_dot(x, wg)) * _dot(x, wu)
        o_ref[...] += _dot(act.astype(BF16), wd_ref[c0:c1, :].astype(BF16))


def _ffn_dense_kernel(h_ref, nw_ref, wg_ref, wu_ref, wd_ref, o_ref, xn_ref):
    f = pl.program_id(1)

    @pl.when(f == 0)
    def _():
        h = h_ref[...]
        xn_ref[...] = _rms(h, nw_ref[...]).astype(BF16)
        o_ref[...] = h

    _swiglu_accumulate(xn_ref[...], wg_ref, wu_ref, wd_ref, o_ref)


def ffn_dense(h, nw, w_gu, w_down, layer, tm=FFN_TM, tf=FFN_TF):
    t, d = h.shape
    ff = w_down.shape[1]
    nf = ff // tf
    return pl.pallas_call(
        _ffn_dense_kernel,
        grid=(t // tm, nf),
        in_specs=[
            pl.BlockSpec((tm, d), lambda i, f: (i, 0)),
            pl.BlockSpec((1, d), lambda i, f: (0, 0)),
            pl.BlockSpec((None, d, tf), lambda i, f: (layer, 0, f)),
            pl.BlockSpec((None, d, tf), lambda i, f: (layer, 0, nf + f)),
            pl.BlockSpec((None, tf, d), lambda i, f: (layer, f, 0)),
        ],
        out_specs=pl.BlockSpec((tm, d), lambda i, f: (i, 0)),
        out_shape=jax.ShapeDtypeStruct((t, d), F32),
        scratch_shapes=[pltpu.VMEM((tm, d), BF16)],
        compiler_params=_cparams(("parallel", "arbitrary"), FFN_VMEM_MB),
        name="ffn_dense",
    )(h, nw.reshape(1, d), w_gu, w_gu, w_down)


def _ffn_group_kernel(te_ref, nv_ref, rows_ref, x_ref, wg_ref, wu_ref, wd_ref, y_ref, xb_ref, acc_ref):
    i = pl.program_id(0)
    f = pl.program_id(1)
    valid = i < nv_ref[0]

    @pl.when(f == 0)
    def _():
        row = lax.broadcasted_iota(jnp.int32, (x_ref.shape[0], 1), 0)
        xb_ref[...] = jnp.where(row < rows_ref[i], _unpack_bf16_pairs(x_ref[...]), 0.0).astype(BF16)
        acc_ref[...] = jnp.zeros_like(acc_ref)

    @pl.when(valid)
    def _():
        _swiglu_accumulate(xb_ref[...], wg_ref.at[0], wu_ref.at[0], wd_ref.at[0], acc_ref)

    @pl.when(f == pl.num_programs(1) - 1)
    def _():
        half = acc_ref.shape[1] // 2
        y_ref[...] = _pack_bf16_pairs(acc_ref[:, :half], acc_ref[:, half:])


def ffn_grouped(xs, tile_expert, n_valid, tile_rows, w_gu, w_down, expert_base, tm=FFN_TM, tf=FFN_TF):
    p = xs.shape[0]
    d = 2 * xs.shape[1]
    ff = w_down.shape[1]
    nf = ff // tf

    def live(i, f, nv):
        ok = i < nv[0]
        return jnp.where(ok, i, nv[0] - 1), jnp.where(ok, f, nf - 1)

    def x_map(i, f, te, nv, rows):
        ii, _ = live(i, f, nv)
        return ii, 0

    def wg_map(i, f, te, nv, rows):
        ii, ff_ = live(i, f, nv)
        return expert_base + te[ii], 0, ff_

    def wu_map(i, f, te, nv, rows):
        ii, ff_ = live(i, f, nv)
        return expert_base + te[ii], 0, nf + ff_

    def wd_map(i, f, te, nv, rows):
        ii, ff_ = live(i, f, nv)
        return expert_base + te[ii], ff_, 0

    grid_spec = pltpu.PrefetchScalarGridSpec(
        num_scalar_prefetch=3,
        grid=(p // tm, nf),
        in_specs=[
            pl.BlockSpec((tm, d // 2), x_map),
            pl.BlockSpec((1, d, tf), wg_map),
            pl.BlockSpec((1, d, tf), wu_map),
            pl.BlockSpec((1, tf, d), wd_map),
        ],
        out_specs=pl.BlockSpec((tm, d // 2), lambda i, f, te, nv, rows: (i, 0)),
        scratch_shapes=[pltpu.VMEM((tm, d), BF16), pltpu.VMEM((tm, d), F32)],
    )
    return pl.pallas_call(
        _ffn_group_kernel,
        grid_spec=grid_spec,
        out_shape=jax.ShapeDtypeStruct((p, d // 2), jnp.uint32),
        compiler_params=_cparams(("arbitrary", "arbitrary"), FFN_VMEM_MB),
        name="ffn_grouped",
    )(tile_expert, n_valid, tile_rows, xs, w_gu, w_gu, w_down)


V7X_SC_CORES = 2
V7X_SC_SUBCORES = 16
V7X_SC_WORKERS = V7X_SC_CORES * V7X_SC_SUBCORES


def _sc_mesh():
    return plsc.VectorSubcoreMesh(core_axis_name="c", subcore_axis_name="s")


def sc_scatter_rows2(x, idx1, idx2, n_out, ch=128):
    t, d = x.shape
    assert t % (V7X_SC_WORKERS * ch) == 0
    t_per_w = t // V7X_SC_WORKERS
    n_ch = t_per_w // ch

    @functools.partial(
        pl.kernel, mesh=_sc_mesh(), out_type=jax.ShapeDtypeStruct((n_out, d), x.dtype),
        scratch_types=[pltpu.VMEM((ch,), jnp.int32), pltpu.VMEM((ch,), jnp.int32),
                       pltpu.VMEM((ch, d), x.dtype), pltpu.SemaphoreType.DMA])
    def k(x_hbm, i1_hbm, i2_hbm, out_hbm, i1_v, i2_v, rows_v, sem):
        wid = lax.axis_index("s") * V7X_SC_CORES + lax.axis_index("c")
        base = wid * t_per_w

        @pl.loop(0, n_ch)
        def _(j):
            off = pl.multiple_of(base + j * ch, 8)
            pltpu.sync_copy(i1_hbm.at[pl.ds(off, ch)], i1_v)
            pltpu.sync_copy(i2_hbm.at[pl.ds(off, ch)], i2_v)
            pltpu.sync_copy(x_hbm.at[pl.ds(off, ch)], rows_v)
            pltpu.async_copy(rows_v, out_hbm.at[i1_v], sem).wait()
            pltpu.async_copy(rows_v, out_hbm.at[i2_v], sem).wait()

    return k(x, idx1, idx2)


def sc_gather_rows(table, idx, ch=128):
    _, d = table.shape
    b = idx.shape[0]
    assert b % (V7X_SC_WORKERS * ch) == 0
    b_per_w = b // V7X_SC_WORKERS
    n_ch = b_per_w // ch

    @functools.partial(
        pl.kernel, mesh=_sc_mesh(), out_type=jax.ShapeDtypeStruct((b, d), table.dtype),
        scratch_types=[pltpu.VMEM((ch,), jnp.int32), pltpu.VMEM((ch, d), table.dtype),
                       pltpu.SemaphoreType.DMA])
    def k(table_hbm, idx_hbm, out_hbm, idx_v, rows_v, sem):
        wid = lax.axis_index("s") * V7X_SC_CORES + lax.axis_index("c")
        base = wid * b_per_w

        @pl.loop(0, n_ch)
        def _(j):
            off = pl.multiple_of(base + j * ch, 8)
            pltpu.sync_copy(idx_hbm.at[pl.ds(off, ch)], idx_v)
            pltpu.async_copy(table_hbm.at[idx_v], rows_v, sem).wait()
            pltpu.sync_copy(rows_v, out_hbm.at[pl.ds(off, ch)])

    return k(table, idx)


def _moe_combine_kernel(h_ref, y1_ref, y2_ref, g1_ref, g2_ref, nw_ref, o_ref, *, normed):
    out = (h_ref[...] + g1_ref[...] * _unpack_bf16_pairs(y1_ref[...])
           + g2_ref[...] * _unpack_bf16_pairs(y2_ref[...]))
    o_ref[...] = _rms(out, nw_ref[...]) if normed else out


def moe_combine(h, yg, gate1, gate2, norm_w=None, tm=1024):
    t, d = h.shape
    nt = t // tm
    row = pl.BlockSpec((tm, d), lambda i: (i, 0))
    col = pl.BlockSpec((tm, 1), lambda i: (i, 0))
    nw = jnp.ones((d,), F32) if norm_w is None else norm_w
    return pl.pallas_call(
        functools.partial(_moe_combine_kernel, normed=norm_w is not None),
        grid=(nt,),
        in_specs=[row, pl.BlockSpec((tm, d // 2), lambda i: (i, 0)),
                  pl.BlockSpec((tm, d // 2), lambda i: (nt + i, 0)), col, col,
                  pl.BlockSpec((1, d), lambda i: (0, 0))],
        out_specs=row,
        out_shape=jax.ShapeDtypeStruct((t, d), F32),
        compiler_params=_cparams(("parallel",)),
        name="moe_combine",
    )(h, yg, yg, gate1, gate2, nw.reshape(1, d))


def _router_kernel(h_ref, nw_ref, wr_ref, xn_ref, slot_ref, gate_ref, *, n_exp):
    xn = _rms(h_ref[...], nw_ref[...])
    half = xn.shape[1] // 2
    xn_ref[...] = _pack_bf16_pairs(xn[:, :half], xn[:, half:])
    xh = xn.astype(BF16)
    xl = (xn - xh.astype(F32)).astype(BF16)
    wr = wr_ref[...]
    wh = wr.astype(BF16)
    wl = (wr - wh.astype(F32)).astype(BF16)
    logits = _dot(xh, wh) + _dot(xh, wl) + _dot(xl, wh)
    lane = lax.broadcasted_iota(jnp.int32, logits.shape, 1)
    neg = -jnp.inf
    lg = jnp.where(lane < n_exp, logits, neg)
    m1 = jnp.max(lg, axis=-1, keepdims=True)
    i1 = jnp.min(jnp.where(lg == m1, lane, LANES), axis=-1, keepdims=True)
    lg2 = jnp.where(lane == i1, neg, lg)
    m2 = jnp.max(lg2, axis=-1, keepdims=True)
    i2 = jnp.min(jnp.where(lg2 == m2, lane, LANES), axis=-1, keepdims=True)
    e2 = jnp.exp(m2 - m1)
    g1 = 1.0 / (1.0 + e2)
    g2 = e2 / (1.0 + e2)
    slot = jnp.where(lane == i1, 1, jnp.where(lane == i2, 2, 0))
    gate = jnp.where(lane == i1, g1, jnp.where(lane == i2, g2, 0.0))
    slot_ref[...] = slot[:, :n_exp]
    gate_ref[...] = gate[:, :n_exp]


def router(h, nw, w_router, tm=512):
    t, d = h.shape
    n_exp = w_router.shape[1]
    wr = jnp.pad(w_router, ((0, 0), (0, LANES - n_exp)))
    return pl.pallas_call(
        functools.partial(_router_kernel, n_exp=n_exp),
        grid=(t // tm,),
        in_specs=[pl.BlockSpec((tm, d), lambda i: (i, 0)),
                  pl.BlockSpec((1, d), lambda i: (0, 0)),
                  pl.BlockSpec((d, LANES), lambda i: (0, 0))],
        out_specs=[pl.BlockSpec((tm, d // 2), lambda i: (i, 0)),
                   pl.BlockSpec((tm, n_exp), lambda i: (i, 0)),
                   pl.BlockSpec((tm, n_exp), lambda i: (i, 0))],
        out_shape=[jax.ShapeDtypeStruct((t, d // 2), jnp.uint32),
                   jax.ShapeDtypeStruct((t, n_exp), jnp.int32),
                   jax.ShapeDtypeStruct((t, n_exp), F32)],
        compiler_params=_cparams(("parallel",)),
        name="router",
    )(h, nw.reshape(1, d), wr)


def _dispatch_plan(slot, gate, tm):
    t, n_exp = slot.shape
    sel = (slot > 0).astype(jnp.int32)
    counts = jnp.sum(sel, axis=0)
    rank = jnp.cumsum(sel, axis=0) - sel
    tiles = (counts + tm - 1) // tm
    tile_end = jnp.cumsum(tiles)
    start = (tile_end - tiles) * tm
    pos = start[None, :] + rank
    n_rows = 2 * t + n_exp * tm
    n_tiles = n_rows // tm
    pos1 = jnp.sum(jnp.where(slot == 1, pos, 0), axis=1)
    pos2 = jnp.sum(jnp.where(slot == 2, pos, 0), axis=1)
    gate1 = jnp.sum(jnp.where(slot == 1, gate, 0.0), axis=1, keepdims=True)
    gate2 = jnp.sum(jnp.where(slot == 2, gate, 0.0), axis=1, keepdims=True)
    tile_id = jnp.arange(n_tiles, dtype=jnp.int32)
    owner = tile_id[:, None] >= tile_end[None, :]
    tile_expert = jnp.minimum(jnp.sum(owner.astype(jnp.int32), axis=1), n_exp - 1)
    mine = tile_expert[:, None] == jnp.arange(n_exp, dtype=jnp.int32)[None, :]
    group_end = jnp.sum(jnp.where(mine, (start + counts)[None, :], 0), axis=1)
    n_valid = tile_end[-1:].astype(jnp.int32)
    tile_rows = jnp.where(tile_id < n_valid[0], jnp.clip(group_end - tile_id * tm, 0, tm), 0)
    return (n_rows, pos1, pos2, gate1, gate2, tile_expert.astype(jnp.int32), n_valid,
            tile_rows.astype(jnp.int32))


def _final_norm_kernel(h_ref, w_ref, o_ref):
    o_ref[...] = _rms(h_ref[...], w_ref[...])


def final_norm(h, w, tm=1024):
    t, d = h.shape
    return pl.pallas_call(
        _final_norm_kernel,
        grid=(t // tm,),
        in_specs=[pl.BlockSpec((tm, d), lambda i: (i, 0)), pl.BlockSpec((1, d), lambda i: (0, 0))],
        out_specs=pl.BlockSpec((tm, d), lambda i: (i, 0)),
        out_shape=jax.ShapeDtypeStruct((t, d), F32),
        compiler_params=_cparams(("parallel",)),
        name="final_norm",
    )(h, w.reshape(1, d))


def kernel(x, mem, mix_norm_w, w_in, ssm_lambda_re, ssm_lambda_im, ssm_b_re, ssm_b_im, ssm_c_re, ssm_c_im, ssm_d, ssm_log_dt, ssm_w_glu, mix_out_gain, w_out, cross_norm_w, mem_norm_w, w_cross_q, w_cross_kv, w_cross_o, ffn_norm_w, w_dense_gu, w_dense_down, w_router, w_expert_gu, w_expert_down, final_norm_w):
    batch, seq, d = x.shape
    depth = w_in.shape[0]
    mem_len = mem.shape[1]
    ssm_w = ssm_d.shape[1]
    g, p = ssm_lambda_re.shape[1:]
    nh = ssm_b_re.shape[-1]
    ret_w = ssm_w
    sb_w = d - ssm_w - ret_w
    ret_dh = ret_w // RET_HEADS
    moe_tm = FFN_TM

    perm = _rot_perm(RET_HEADS, ret_dh)
    cols = np.arange(w_in.shape[2])
    cols[ssm_w:ssm_w + ret_w] = ssm_w + perm
    cols[ssm_w + ret_w:ssm_w + 2 * ret_w] = ssm_w + ret_w + perm
    w_in_b = w_in[:, :, cols].astype(BF16)
    w_out_b = w_out.astype(BF16)
    w_q_b = w_cross_q.astype(BF16)
    w_kv_b = w_cross_kv.astype(BF16)
    w_o_b = w_cross_o.astype(BF16)
    n_exp = w_expert_gu.shape[1]
    w_egu = w_expert_gu.reshape((-1,) + w_expert_gu.shape[2:])
    w_edn = w_expert_down.reshape((-1,) + w_expert_down.shape[2:])
    n_a = ssm_w + 4 * ret_w

    apow, bbc = s5_prep(ssm_lambda_re, ssm_lambda_im, ssm_log_dt, ssm_b_re, ssm_b_im,
                        nstep=S5_TIME_TILE // SUBLANES)

    mem2 = mem.reshape(batch * mem_len, d)
    h = x.reshape(batch * seq, d)
    for i in range(depth):
        p_a, p_sb = norm_proj(h, mix_norm_w[i], w_in_b, i, (n_a, 3 * sb_w), (F32, BF16))
        bb = _block_diag_in(bbc[i], g, p, nh).astype(BF16)
        cc = _block_diag_out(ssm_c_re[i], ssm_c_im[i]).astype(BF16)
        y_a = s5_mixer(p_a, bb, cc, apow[i], ssm_d[i], ssm_w_glu[i].astype(BF16), batch, seq)
        gain = mix_out_gain[i]
        y_b = retention_mixer(p_a, ssm_w, gain[ssm_w:ssm_w + ret_w], batch, seq)
        y_c = stick_breaking_mixer(p_sb, batch, seq)
        (kv,) = norm_proj(mem2, mem_norm_w, w_kv_b, i, (2 * d,), (BF16,))
        h = mix_cross(h, y_a, y_b, y_c, gain, w_out_b, cross_norm_w[i], w_q_b, kv, w_o_b, i, batch, seq)
        if i % 2 == 0:
            h = ffn_dense(h, ffn_norm_w[i], w_dense_gu, w_dense_down, i // 2)
        else:
            xn, slot, gate = router(h, ffn_norm_w[i], w_router[i // 2])
            n_rows, pos1, pos2, gate1, gate2, tile_expert, n_valid, tile_rows = _dispatch_plan(
                slot, gate, moe_tm)
            xs = sc_scatter_rows2(xn, pos1, pos2, n_rows)
            y = ffn_grouped(xs, tile_expert, n_valid, tile_rows, w_egu, w_edn, (i // 2) * n_exp,
                            tm=moe_tm)
            yg = sc_gather_rows(y, jnp.concatenate([pos1, pos2]))
            last = i == depth - 1
            h = moe_combine(h, yg, gate1, gate2, norm_w=final_norm_w if last else None)
    if depth % 2 == 1:
        h = final_norm(h, final_norm_w)
    return h.reshape(batch, seq, d)
```

```python
import functools
import math

import numpy as np
import jax
import jax.numpy as jnp
from jax import lax
from jax.experimental import pallas as pl
from jax.experimental.pallas import tpu as pltpu
from jax.experimental.pallas import tpu_sc as plsc

F32 = jnp.float32
BF16 = jnp.bfloat16

NORM_EPS = 1e-6
GN_EPS = 1e-6
ROPE_BASE = 10000.0
LOG2_E = 1.4426950408889634
SB_DEAD_LOG2 = -150.0
SB_MASKED_LOG2 = -1e30

SSM_GROUP = 16
SSM_STATE = 64
RET_HEADS = 4
SB_HEADS = 8
CROSS_HEADS = 4
N_EXPERTS = 8

V7X_VMEM_BYTES = 64 * 1024 * 1024
LANES = 128
SUBLANES = 8


def _cparams(sem, vmem_mb=48):
    return pltpu.CompilerParams(dimension_semantics=sem, vmem_limit_bytes=vmem_mb * 1024 * 1024)


def _rms(x, w):
    ms = jnp.mean(x * x, axis=-1, keepdims=True)
    return x * lax.rsqrt(ms + NORM_EPS) * w


def _dot(a, b):
    return jnp.dot(a, b, preferred_element_type=F32)


def _dot_nt(a, b):
    return lax.dot_general(a, b, (((1,), (1,)), ((), ())), preferred_element_type=F32)


def _dot_tn(a, b):
    return lax.dot_general(a, b, (((0,), (0,)), ((), ())), preferred_element_type=F32)


def _pack_bf16_pairs(a, b):
    ua = lax.bitcast_convert_type(a.astype(BF16).astype(F32), jnp.uint32)
    ub = lax.bitcast_convert_type(b.astype(BF16).astype(F32), jnp.uint32)
    return (ub & jnp.uint32(0xFFFF0000)) | (ua >> 16)


def _unpack_bf16_pairs(p):
    lo = lax.bitcast_convert_type(p << 16, F32)
    hi = lax.bitcast_convert_type(p & jnp.uint32(0xFFFF0000), F32)
    return jnp.concatenate([lo, hi], axis=1)


def _split_dot(x, m_bf16):
    hi = x.astype(BF16)
    lo = (x - hi.astype(F32)).astype(BF16)
    return _dot(hi, m_bf16) + _dot(lo, m_bf16)


def _norm_proj_kernel(x_ref, nw_ref, w_ref, *o_refs, splits, chunk):
    xn = _rms(x_ref[...], nw_ref[...]).astype(BF16)
    col = 0
    for o_ref, width in zip(o_refs, splits):
        for c0 in range(0, width, chunk):
            r = _dot(xn, w_ref[:, col + c0:col + c0 + chunk])
            o_ref[:, c0:c0 + chunk] = r.astype(o_ref.dtype)
        col += width


def norm_proj(x, nw, w, layer, splits, dtypes, tm=1024, chunk=256):
    rows, d = x.shape
    n = w.shape[2]
    assert sum(splits) == n and rows % tm == 0
    return pl.pallas_call(
        functools.partial(_norm_proj_kernel, splits=splits, chunk=chunk),
        grid=(rows // tm,),
        in_specs=[
            pl.BlockSpec((tm, d), lambda i: (i, 0)),
            pl.BlockSpec((1, d), lambda i: (0, 0)),
            pl.BlockSpec((None, d, n), lambda i: (layer, 0, 0)),
        ],
        out_specs=[pl.BlockSpec((tm, s), lambda i: (i, 0)) for s in splits],
        out_shape=[jax.ShapeDtypeStruct((rows, s), dt) for s, dt in zip(splits, dtypes)],
        compiler_params=_cparams(("parallel",)),
        name="norm_proj",
    )(x, nw.reshape(1, d), w)


def _s5_prep_kernel(lr_ref, li_ref, ldt_ref, br_ref, bi_ref, apow_ref, bb_ref):
    lr = lr_ref[0]
    li = li_ref[0]
    dt = jnp.exp(ldt_ref[0])
    mag = jnp.exp(lr * dt)
    a_re = mag * jnp.cos(li * dt)
    a_im = mag * jnp.sin(li * dt)
    denom = lr * lr + li * li
    nr = a_re - 1.0
    z_re = (nr * lr + a_im * li) / denom
    z_im = (a_im * lr - nr * li) / denom
    br = br_ref[0]
    bi = bi_ref[0]
    nh = br.shape[0]
    bb_ref[0, :nh, :] = z_re * br - z_im * bi
    bb_ref[0, nh:, :] = z_re * bi + z_im * br
    nstep = (apow_ref.shape[1] - 2 * S5_SEG_LEVELS) // 2
    pr, pi = a_re, a_im
    for j in range(nstep):
        apow_ref[0, j:j + 1, :] = pr
        apow_ref[0, nstep + j:nstep + j + 1, :] = pi
        if j + 1 < nstep:
            pr, pi = pr * a_re - pi * a_im, pr * a_im + pi * a_re
    for k in range(S5_SEG_LEVELS):
        apow_ref[0, 2 * nstep + k:2 * nstep + k + 1, :] = pr
        apow_ref[0, 2 * nstep + S5_SEG_LEVELS + k:2 * nstep + S5_SEG_LEVELS + k + 1, :] = pi
        pr, pi = pr * pr - pi * pi, 2.0 * pr * pi


S5_SEG_LEVELS = 3
S5_TIME_TILE = 512


def s5_prep(lam_re, lam_im, log_dt, b_re, b_im, nstep):
    depth, g, p = lam_re.shape
    nh = b_re.shape[-1]
    gp = g * p
    lr = lam_re.reshape(depth, 1, gp)
    li = lam_im.reshape(depth, 1, gp)
    ldt = jnp.repeat(log_dt, p, axis=1).reshape(depth, 1, gp)
    br = b_re.transpose(0, 3, 1, 2).reshape(depth, nh, gp)
    bi = b_im.transpose(0, 3, 1, 2).reshape(depth, nh, gp)
    vec = pl.BlockSpec((1, 1, gp), lambda i: (i, 0, 0))
    mat = pl.BlockSpec((1, nh, gp), lambda i: (i, 0, 0))
    npow = 2 * (nstep + S5_SEG_LEVELS)
    return pl.pallas_call(
        _s5_prep_kernel,
        grid=(depth,),
        in_specs=[vec, vec, vec, mat, mat],
        out_specs=[pl.BlockSpec((1, npow, gp), lambda i: (i, 0, 0)),
                   pl.BlockSpec((1, 2 * nh, gp), lambda i: (i, 0, 0))],
        out_shape=[jax.ShapeDtypeStruct((depth, npow, gp), F32),
                   jax.ShapeDtypeStruct((depth, 2 * nh, gp), F32)],
        compiler_params=_cparams(("arbitrary",)),
        name="s5_prep",
    )(lr, li, ldt, br, bi)


def _s5_kernel(u_ref, bb_ref, cc_ref, apow_ref, d_ref, wglu_ref, o_ref, xs_ref, carry_ref, perm_ref,
               *, gp):
    ts = u_ref.shape[0]
    nstep = ts // SUBLANES
    rows = lambda s: slice(s * SUBLANES, (s + 1) * SUBLANES)

    @pl.when(pl.program_id(1) == 0)
    def _():
        carry_ref[...] = jnp.zeros_like(carry_ref)

    ntile = u_ref.shape[1] // LANES
    for c in range(ntile):
        perm_ref[c] = u_ref[:, c * LANES:(c + 1) * LANES]
    u = jnp.concatenate(
        [jnp.concatenate([perm_ref[c, pl.ds(s, SUBLANES, stride=nstep), :] for c in range(ntile)], axis=1)
         for s in range(nstep)], axis=0)
    xs_ref[...] = _dot(u.astype(BF16), bb_ref[...])

    ar = apow_ref[0:1, :]
    ai = apow_ref[nstep:nstep + 1, :]
    xr = xs_ref[rows(0), :gp]
    xi = xs_ref[rows(0), gp:]
    for s in range(1, nstep):
        xr, xi = (ar * xr - ai * xi + xs_ref[rows(s), :gp], ar * xi + ai * xr + xs_ref[rows(s), gp:])
        xs_ref[rows(s), :gp] = xr
        xs_ref[rows(s), gp:] = xi

    seg = lax.broadcasted_iota(jnp.int32, (SUBLANES, 1), 0)
    cr = carry_ref[0:1, :]
    ci = carry_ref[1:2, :]
    lr = apow_ref[2 * nstep:2 * nstep + 1, :]
    li = apow_ref[2 * nstep + S5_SEG_LEVELS:2 * nstep + S5_SEG_LEVELS + 1, :]
    first = seg == 0
    xr = xr + jnp.where(first, lr * cr - li * ci, 0.0)
    xi = xi + jnp.where(first, lr * ci + li * cr, 0.0)
    for k in range(S5_SEG_LEVELS):
        sh = 1 << k
        pr = apow_ref[2 * nstep + k:2 * nstep + k + 1, :]
        pi = apow_ref[2 * nstep + S5_SEG_LEVELS + k:2 * nstep + S5_SEG_LEVELS + k + 1, :]
        keep = seg >= sh
        sr = jnp.where(keep, pltpu.roll(xr, sh, axis=0), 0.0)
        si = jnp.where(keep, pltpu.roll(xi, sh, axis=0), 0.0)
        xr, xi = xr + pr * sr - pi * si, xi + pr * si + pi * sr
    carry_ref[0:1, :] = xr[SUBLANES - 1:, :]
    carry_ref[1:2, :] = xi[SUBLANES - 1:, :]
    enter_r = jnp.where(first, cr, pltpu.roll(xr, 1, axis=0))
    enter_i = jnp.where(first, ci, pltpu.roll(xi, 1, axis=0))

    for s in range(nstep):
        pr = apow_ref[s:s + 1, :]
        pi = apow_ref[nstep + s:nstep + s + 1, :]
        xs_ref[rows(s), :gp] += pr * enter_r - pi * enter_i
        xs_ref[rows(s), gp:] += pr * enter_i + pi * enter_r

    y = _dot(xs_ref[...].astype(BF16), cc_ref[...]) + d_ref[...] * u
    g = jax.nn.gelu(y)
    out = g * jax.nn.sigmoid(_dot(g.astype(BF16), wglu_ref[...]))
    for s in range(nstep):
        for c in range(ntile):
            perm_ref[c, pl.ds(s, SUBLANES, stride=nstep), :] = out[rows(s), c * LANES:(c + 1) * LANES]
    for c in range(ntile):
        o_ref[:, c * LANES:(c + 1) * LANES] = perm_ref[c]


def s5_mixer(p_a, bb, cc, apow, d_skip, w_glu, batch, seq):
    w = d_skip.shape[-1]
    gp = apow.shape[-1]
    ts = (apow.shape[0] // 2 - S5_SEG_LEVELS) * SUBLANES
    nt = seq // ts
    return pl.pallas_call(
        functools.partial(_s5_kernel, gp=gp),
        grid=(batch, nt),
        in_specs=[
            pl.BlockSpec((ts, w), lambda b, t: (b * nt + t, 0)),
            pl.BlockSpec((w, 2 * gp), lambda b, t: (0, 0)),
            pl.BlockSpec((2 * gp, w), lambda b, t: (0, 0)),
            pl.BlockSpec(apow.shape, lambda b, t: (0, 0)),
            pl.BlockSpec((1, w), lambda b, t: (0, 0)),
            pl.BlockSpec((w, w), lambda b, t: (0, 0)),
        ],
        out_specs=pl.BlockSpec((ts, w), lambda b, t: (b * nt + t, 0)),
        out_shape=jax.ShapeDtypeStruct((batch * seq, w), F32),
        scratch_shapes=[pltpu.VMEM((ts, 2 * gp), F32), pltpu.VMEM((2, gp), F32),
                        pltpu.VMEM((w // LANES, ts, LANES), F32)],
        compiler_params=_cparams(("parallel", "arbitrary")),
        name="s5_mixer",
    )(p_a, bb, cc, apow, d_skip.reshape(1, w), w_glu)


def _block_diag_in(bb, g, p, nh):
    eye = jnp.eye(g, dtype=F32)
    def one(m):
        m = m.reshape(nh, g, p)
        return jnp.einsum('hgp,kg->khgp', m, eye).reshape(g * nh, g * p)
    return jnp.concatenate([one(bb[:nh]), one(bb[nh:])], axis=1)


def _block_diag_out(c_re, c_im):
    g, nh, p = c_re.shape
    eye = jnp.eye(g, dtype=F32)
    def one(m):
        return jnp.einsum('ghp,gk->gpkh', m, eye).reshape(g * p, g * nh)
    return jnp.concatenate([one(c_re), one(-c_im)], axis=0)


def _ret_tables(seq, chunk, heads, dh):
    half = dh // 2
    w = heads * dh
    hw = w // 2
    inv_freq = ROPE_BASE ** (-np.arange(half, dtype=np.float64) / half)
    ang = np.arange(seq, dtype=np.float64)[:, None] * np.tile(inv_freq, heads)[None, :]
    cos = np.cos(ang).astype(np.float32)
    sin = np.sin(ang).astype(np.float32)
    log_gamma = np.log1p(-(2.0 ** (-5.0 - np.arange(heads, dtype=np.float64))))
    idx = np.arange(chunk, dtype=np.float64)
    rel = idx[:, None] - idx[None, :]
    intra = np.where(rel >= 0, np.exp(log_gamma[:, None, None] * np.maximum(rel, 0.0)), 0.0)
    head_nat = np.arange(w) // dh
    head_rot = (np.arange(w) % hw) // half
    xi = np.exp(log_gamma[head_nat][None, :] * (idx[:, None] + 1.0))
    zeta = np.exp(log_gamma[head_rot][None, :] * (chunk - 1.0 - idx[:, None]))
    decay = np.exp(log_gamma[head_nat] * chunk)[None, :]
    same = (head_rot[:, None] == head_nat[None, :]).astype(np.float32)
    gavg = (head_nat[:, None] == head_nat[None, :]).astype(np.float32) / dh
    f = lambda a: jnp.asarray(a, dtype=F32)
    return dict(cos=f(cos), sin=f(sin), intra=f(intra), xi=f(xi), zeta=f(zeta), decay=f(decay),
                same=f(same), gavg=jnp.asarray(gavg, dtype=BF16))


def _ret_kernel(q_ref, k_ref, v_ref, g_ref, cos_ref, sin_ref, intra_ref, xi_ref, zeta_ref,
                decay_ref, same_ref, gavg_ref, gnw_ref, o_ref, state_ref, *, heads, dh):
    w = heads * dh
    hw = w // 2
    half = dh // 2

    @pl.when(pl.program_id(1) == 0)
    def _():
        state_ref[...] = jnp.zeros_like(state_ref)

    cos = cos_ref[...]
    sin = sin_ref[...]

    def rot(x):
        x1, x2 = x[:, :hw], x[:, hw:]
        return jnp.concatenate([x1 * cos - x2 * sin, x1 * sin + x2 * cos], axis=1)

    lane = lax.broadcasted_iota(jnp.int32, (1, w), 1)
    head_rot = (lane % hw) // half
    head_nat = lane // dh
    gavg = gavg_ref[...]

    for b in range(q_ref.shape[0]):
        q = rot(q_ref[b])
        k = rot(k_ref[b]) * (dh ** -0.5)
        qb = q.astype(BF16)
        kb = k.astype(BF16)
        vb = v_ref[b].astype(BF16)

        o = _dot(qb, state_ref[b].astype(BF16)) * xi_ref[...]
        for h in range(heads):
            qh = jnp.where(head_rot == h, qb, jnp.zeros_like(qb))
            vh = jnp.where(head_nat == h, vb, jnp.zeros_like(vb))
            s = _dot_nt(qh, kb) * intra_ref[h]
            o = o + _dot(s.astype(BF16), vh)
        kz = (k * zeta_ref[...]).astype(BF16)
        state_ref[b] = decay_ref[...] * state_ref[b] + _dot_tn(kz, vb) * same_ref[...]

        mu = _split_dot(o, gavg)
        d = o - mu
        var = _split_dot(d * d, gavg)
        on = d * lax.rsqrt(var + GN_EPS) * gnw_ref[...]
        o_ref[b] = jax.nn.silu(g_ref[b]) * on


def retention_mixer(p_a, col0, gn_w, batch, seq, heads=RET_HEADS, chunk=256, nb=2):
    w = gn_w.shape[-1]
    dh = w // heads
    nc = seq // chunk
    tb = _ret_tables(seq, chunk, heads, dh)
    cb = col0 // w
    assert col0 % w == 0 and batch % nb == 0
    p3 = p_a.reshape(batch, seq, p_a.shape[1])
    row = lambda j: pl.BlockSpec((nb, chunk, w), lambda b, c, j=j: (b, c, cb + j))
    const = lambda shape: pl.BlockSpec(shape, lambda b, c: (0,) * len(shape))
    return pl.pallas_call(
        functools.partial(_ret_kernel, heads=heads, dh=dh),
        grid=(batch // nb, nc),
        in_specs=[row(0), row(1), row(2), row(3),
                  pl.BlockSpec((chunk, w // 2), lambda b, c: (c, 0)),
                  pl.BlockSpec((chunk, w // 2), lambda b, c: (c, 0)),
                  const((heads, chunk, chunk)), const((chunk, w)), const((chunk, w)),
                  const((1, w)), const((w, w)), const((w, w)), const((1, w))],
        out_specs=pl.BlockSpec((nb, chunk, w), lambda b, c: (b, c, 0)),
        out_shape=jax.ShapeDtypeStruct((batch, seq, w), F32),
        scratch_shapes=[pltpu.VMEM((nb, w, w), F32)],
        compiler_params=_cparams(("parallel", "arbitrary")),
        name="retention_mixer",
    )(p3, p3, p3, p3, tb["cos"], tb["sin"], tb["intra"], tb["xi"], tb["zeta"], tb["decay"],
      tb["same"], tb["gavg"], gn_w.reshape(1, w)).reshape(batch * seq, w)


def _group_rotary_halves(w, heads):
    lead = w.shape[:-1]
    dh = w.shape[-1] // heads
    return w.reshape(lead + (heads, 2, dh // 2)).swapaxes(-3, -2).reshape(lead + (heads * dh,))


def _sb_kernel(q_ref, k_ref, v_ref, o_ref, *, dh, blk, nsub):
    step = pl.program_id(2)
    lane = lax.broadcasted_iota(jnp.int32, (1, 2 * dh), 1)
    r_id = lax.broadcasted_iota(jnp.int32, (blk, blk), 0)
    c_id = lax.broadcasted_iota(jnp.int32, (blk, blk), 1)
    neg_upper = jnp.where(r_id > c_id, -1.0, 0.0).astype(BF16)
    causal = c_id < r_id
    zscale = (dh ** -0.5) * LOG2_E
    hsel = [(lane // dh) == hh for hh in range(2)]

    def pair(qm, j, runs, diag):
        off = pl.multiple_of(j * blk, blk)
        kb = k_ref[pl.ds(off, blk), :]
        vb = v_ref[pl.ds(off, blk), :]
        wgts, vms, new_runs = [], [], []
        for hh in range(2):
            zs = _dot_nt(qm[hh], kb) * zscale
            nfail = jnp.maximum(zs, 0.0) + jnp.log2(1.0 + jnp.exp2(jnp.minimum(zs, -zs)))
            if diag:
                nfail = jnp.where(causal, nfail, 0.0)
            after = _dot(nfail.astype(BF16), neg_upper) + runs[hh]
            wgt = jnp.exp2((zs - nfail) + after)
            if diag:
                wgt = jnp.where(causal, wgt, 0.0)
            wgts.append(wgt.astype(BF16))
            vms.append(jnp.where(hsel[hh], vb, jnp.zeros_like(vb)))
            new_runs.append(runs[hh] - jnp.sum(nfail, axis=1, keepdims=True))
        contrib = _dot(jnp.concatenate(wgts, axis=1), jnp.concatenate(vms, axis=0))
        return contrib, new_runs

    zero = jnp.zeros((blk, 1), F32)
    started = []
    for s in range(nsub):
        i = step * nsub + s
        q2 = q_ref[s * blk:(s + 1) * blk, :]
        qm = [jnp.where(hs, q2, jnp.zeros_like(q2)) for hs in hsel]
        acc0, runs0 = pair(qm, i, [zero, zero], True)
        acc1, runs1 = pair(qm, jnp.maximum(i - 1, 0), runs0, False)
        has_prev = i >= 1
        acc = acc0 + jnp.where(has_prev, acc1, 0.0)
        ra = jnp.where(has_prev, runs1[0], runs0[0])
        rb = jnp.where(has_prev, runs1[1], runs0[1])
        started.append((i, qm, acc, ra, rb))

    for s, (i, qm, acc, ra, rb) in enumerate(started):
        def cond(carry, i=i):
            jj, _, ra, rb = carry
            alive = jnp.maximum(jnp.max(ra), jnp.max(rb)) > SB_DEAD_LOG2
            return jnp.logical_and(jj <= i, alive)

        def body(carry, i=i, qm=qm):
            jj, acc, ra, rb = carry
            c, (ra, rb) = pair(qm, i - jj, [ra, rb], False)
            return jj + 1, acc + c, ra, rb

        _, acc, _, _ = lax.while_loop(cond, body, (jnp.int32(2), acc, ra, rb))
        o_ref[s * blk:(s + 1) * blk, :] = acc


def stick_breaking_mixer(p_sb, batch, seq, heads=SB_HEADS, blk=256, nsub=4):
    w = p_sb.shape[1] // 3
    dh = w // heads
    pw = 2 * dh
    assert pw == LANES
    npair = heads // 2
    nq = seq // (blk * nsub)
    return pl.pallas_call(
        functools.partial(_sb_kernel, dh=dh, blk=blk, nsub=nsub),
        grid=(batch, npair, nq),
        in_specs=[
            pl.BlockSpec((blk * nsub, pw), lambda b, p, i: (b * nq + i, p)),
            pl.BlockSpec((seq, pw), lambda b, p, i: (b, npair + p)),
            pl.BlockSpec((seq, pw), lambda b, p, i: (b, 2 * npair + p)),
        ],
        out_specs=pl.BlockSpec((blk * nsub, pw), lambda b, p, i: (b * nq + i, p)),
        out_shape=jax.ShapeDtypeStruct((batch * seq, w), F32),
        compiler_params=_cparams(("parallel", "parallel", "arbitrary")),
        name="stick_breaking_mixer",
    )(p_sb, p_sb, p_sb)


def _mix_cross_kernel(h_ref, ya_ref, yb_ref, yc_ref, gain_ref, wout_ref, cnw_ref, wq_ref, kv_ref,
                      wo_ref, o_ref, *, heads):
    wa = ya_ref.shape[1]
    wb = yb_ref.shape[1]
    d = h_ref.shape[1]
    gain = gain_ref[...]
    ya = _rms(ya_ref[...], gain[:, :wa]).astype(BF16)
    yb = yb_ref[...].astype(BF16)
    yc = _rms(yc_ref[...], gain[:, wa + wb:]).astype(BF16)
    h1 = (h_ref[...] + _dot(ya, wout_ref[:wa, :]) + _dot(yb, wout_ref[wa:wa + wb, :])
          + _dot(yc, wout_ref[wa + wb:, :]))
    q = _dot(_rms(h1, cnw_ref[...]).astype(BF16), wq_ref[...])
    dh = d // heads
    outs = []
    for hd in range(heads):
        qh = q[:, hd * dh:(hd + 1) * dh].astype(BF16)
        kh = kv_ref[:, hd * dh:(hd + 1) * dh]
        vh = kv_ref[:, d + hd * dh:d + (hd + 1) * dh]
        s = _dot_nt(qh, kh) * (dh ** -0.5)
        s = s - jnp.max(s, axis=-1, keepdims=True)
        e = jnp.exp(s)
        p = e / jnp.sum(e, axis=-1, keepdims=True)
        outs.append(_dot(p.astype(BF16), vh).astype(BF16))
    o_ref[...] = h1 + _dot(jnp.concatenate(outs, axis=1), wo_ref[...])


def mix_cross(h, ya, yb, yc, gain, w_out, cnw, wq, kv, wo, layer, batch, seq, heads=CROSS_HEADS, tm=1024):
    t, d = h.shape
    m = kv.shape[0] // batch
    nt = seq // tm
    rows = lambda wd: pl.BlockSpec((tm, wd), lambda i: (i, 0))
    const = lambda shape: pl.BlockSpec(shape, lambda i: (0, 0))
    stack = pl.BlockSpec((None, d, d), lambda i: (layer, 0, 0))
    return pl.pallas_call(
        functools.partial(_mix_cross_kernel, heads=heads),
        grid=(t // tm,),
        in_specs=[rows(d), rows(ya.shape[1]), rows(yb.shape[1]), rows(yc.shape[1]),
                  const((1, d)), stack, const((1, d)), stack,
                  pl.BlockSpec((m, 2 * d), lambda i: (i // nt, 0)),
                  stack],
        out_specs=rows(d),
        out_shape=jax.ShapeDtypeStruct((t, d), F32),
        compiler_params=_cparams(("parallel",)),
        name="mix_cross",
    )(h, ya, yb, yc, gain.reshape(1, d), w_out, cnw.reshape(1, d), wq, kv, wo)


FFN_TM = 1024
FFN_TF = 1408
FFN_CHUNK = 256
FFN_VMEM_MB = 60


def _swiglu_accumulate(x, wg_ref, wu_ref, wd_ref, o_ref):
    tf = wd_ref.shape[0]
    for c0 in range(0, tf, FFN_CHUNK):
        c1 = min(c0 + FFN_CHUNK, tf)
        wg = wg_ref[:, c0:c1].astype(BF16)
        wu = wu_ref[:, c0:c1].astype(BF16)
        act = jax.nn.silu(_dot(x, wg)) * _dot(x, wu)
        o_ref[...] += _dot(act.astype(BF16), wd_ref[c0:c1, :].astype(BF16))


def _ffn_dense_kernel(h_ref, nw_ref, wg_ref, wu_ref, wd_ref, o_ref, xn_ref):
    f = pl.program_id(1)

    @pl.when(f == 0)
    def _():
        h = h_ref[...]
        xn_ref[...] = _rms(h, nw_ref[...]).astype(BF16)
        o_ref[...] = h

    _swiglu_accumulate(xn_ref[...], wg_ref, wu_ref, wd_ref, o_ref)


def ffn_dense(h, nw, w_gu, w_down, layer, tm=FFN_TM, tf=FFN_TF):
    t, d = h.shape
    ff = w_down.shape[1]
    nf = ff // tf
    return pl.pallas_call(
        _ffn_dense_kernel,
        grid=(t // tm, nf),
        in_specs=[
            pl.BlockSpec((tm, d), lambda i, f: (i, 0)),
            pl.BlockSpec((1, d), lambda i, f: (0, 0)),
            pl.BlockSpec((None, d, tf), lambda i, f: (layer, 0, f)),
            pl.BlockSpec((None, d, tf), lambda i, f: (layer, 0, nf + f)),
            pl.BlockSpec((None, tf, d), lambda i, f: (layer, f, 0)),
        ],
        out_specs=pl.BlockSpec((tm, d), lambda i, f: (i, 0)),
        out_shape=jax.ShapeDtypeStruct((t, d), F32),
        scratch_shapes=[pltpu.VMEM((tm, d), BF16)],
        compiler_params=_cparams(("parallel", "arbitrary"), FFN_VMEM_MB),
        name="ffn_dense",
    )(h, nw.reshape(1, d), w_gu, w_gu, w_down)


def _ffn_group_kernel(te_ref, nv_ref, rows_ref, x_ref, wg_ref, wu_ref, wd_ref, y_ref, xb_ref, acc_ref):
    i = pl.program_id(0)
    f = pl.program_id(1)
    valid = i < nv_ref[0]

    @pl.when(f == 0)
    def _():
        row = lax.broadcasted_iota(jnp.int32, (x_ref.shape[0], 1), 0)
        xb_ref[...] = jnp.where(row < rows_ref[i], _unpack_bf16_pairs(x_ref[...]), 0.0).astype(BF16)
        acc_ref[...] = jnp.zeros_like(acc_ref)

    @pl.when(valid)
    def _():
        _swiglu_accumulate(xb_ref[...], wg_ref.at[0], wu_ref.at[0], wd_ref.at[0], acc_ref)

    @pl.when(f == pl.num_programs(1) - 1)
    def _():
        half = acc_ref.shape[1] // 2
        y_ref[...] = _pack_bf16_pairs(acc_ref[:, :half], acc_ref[:, half:])


def ffn_grouped(xs, tile_expert, n_valid, tile_rows, w_gu, w_down, expert_base, tm=FFN_TM, tf=FFN_TF):
    p = xs.shape[0]
    d = 2 * xs.shape[1]
    ff = w_down.shape[1]
    nf = ff // tf

    def live(i, f, nv):
        ok = i < nv[0]
        return jnp.where(ok, i, nv[0] - 1), jnp.where(ok, f, nf - 1)

    def x_map(i, f, te, nv, rows):
        ii, _ = live(i, f, nv)
        return ii, 0

    def wg_map(i, f, te, nv, rows):
        ii, ff_ = live(i, f, nv)
        return expert_base + te[ii], 0, ff_

    def wu_map(i, f, te, nv, rows):
        ii, ff_ = live(i, f, nv)
        return expert_base + te[ii], 0, nf + ff_

    def wd_map(i, f, te, nv, rows):
        ii, ff_ = live(i, f, nv)
        return expert_base + te[ii], ff_, 0

    grid_spec = pltpu.PrefetchScalarGridSpec(
        num_scalar_prefetch=3,
        grid=(p // tm, nf),
        in_specs=[
            pl.BlockSpec((tm, d // 2), x_map),
            pl.BlockSpec((1, d, tf), wg_map),
            pl.BlockSpec((1, d, tf), wu_map),
            pl.BlockSpec((1, tf, d), wd_map),
        ],
        out_specs=pl.BlockSpec((tm, d // 2), lambda i, f, te, nv, rows: (i, 0)),
        scratch_shapes=[pltpu.VMEM((tm, d), BF16), pltpu.VMEM((tm, d), F32)],
    )
    return pl.pallas_call(
        _ffn_group_kernel,
        grid_spec=grid_spec,
        out_shape=jax.ShapeDtypeStruct((p, d // 2), jnp.uint32),
        compiler_params=_cparams(("arbitrary", "arbitrary"), FFN_VMEM_MB),
        name="ffn_grouped",
    )(tile_expert, n_valid, tile_rows, xs, w_gu, w_gu, w_down)


V7X_SC_CORES = 2
V7X_SC_SUBCORES = 16
V7X_SC_WORKERS = V7X_SC_CORES * V7X_SC_SUBCORES


def _sc_mesh():
    return plsc.VectorSubcoreMesh(core_axis_name="c", subcore_axis_name="s")


def sc_scatter_rows2(x, idx1, idx2, n_out, ch=128):
    t, d = x.shape
    assert t % (V7X_SC_WORKERS * ch) == 0
    t_per_w = t // V7X_SC_WORKERS
    n_ch = t_per_w // ch

    @functools.partial(
        pl.kernel, mesh=_sc_mesh(), out_type=jax.ShapeDtypeStruct((n_out, d), x.dtype),
        scratch_types=[pltpu.VMEM((ch,), jnp.int32), pltpu.VMEM((ch,), jnp.int32),
                       pltpu.VMEM((ch, d), x.dtype), pltpu.SemaphoreType.DMA])
    def k(x_hbm, i1_hbm, i2_hbm, out_hbm, i1_v, i2_v, rows_v, sem):
        wid = lax.axis_index("s") * V7X_SC_CORES + lax.axis_index("c")
        base = wid * t_per_w

        @pl.loop(0, n_ch)
        def _(j):
            off = pl.multiple_of(base + j * ch, 8)
            pltpu.sync_copy(i1_hbm.at[pl.ds(off, ch)], i1_v)
            pltpu.sync_copy(i2_hbm.at[pl.ds(off, ch)], i2_v)
            pltpu.sync_copy(x_hbm.at[pl.ds(off, ch)], rows_v)
            pltpu.async_copy(rows_v, out_hbm.at[i1_v], sem).wait()
            pltpu.async_copy(rows_v, out_hbm.at[i2_v], sem).wait()

    return k(x, idx1, idx2)


def sc_gather_rows(table, idx, ch=128):
    _, d = table.shape
    b = idx.shape[0]
    assert b % (V7X_SC_WORKERS * ch) == 0
    b_per_w = b // V7X_SC_WORKERS
    n_ch = b_per_w // ch

    @functools.partial(
        pl.kernel, mesh=_sc_mesh(), out_type=jax.ShapeDtypeStruct((b, d), table.dtype),
        scratch_types=[pltpu.VMEM((ch,), jnp.int32), pltpu.VMEM((ch, d), table.dtype),
                       pltpu.SemaphoreType.DMA])
    def k(table_hbm, idx_hbm, out_hbm, idx_v, rows_v, sem):
        wid = lax.axis_index("s") * V7X_SC_CORES + lax.axis_index("c")
        base = wid * b_per_w

        @pl.loop(0, n_ch)
        def _(j):
            off = pl.multiple_of(base + j * ch, 8)
            pltpu.sync_copy(idx_hbm.at[pl.ds(off, ch)], idx_v)
            pltpu.async_copy(table_hbm.at[idx_v], rows_v, sem).wait()
            pltpu.sync_copy(rows_v, out_hbm.at[pl.ds(off, ch)])

    return k(table, idx)


def _moe_combine_kernel(h_ref, y1_ref, y2_ref, g1_ref, g2_ref, nw_ref, o_ref, *, normed):
    out = (h_ref[...] + g1_ref[...] * _unpack_bf16_pairs(y1_ref[...])
           + g2_ref[...] * _unpack_bf16_pairs(y2_ref[...]))
    o_ref[...] = _rms(out, nw_ref[...]) if normed else out


def moe_combine(h, yg, gate1, gate2, norm_w=None, tm=1024):
    t, d = h.shape
    nt = t // tm
    row = pl.BlockSpec((tm, d), lambda i: (i, 0))
    col = pl.BlockSpec((tm, 1), lambda i: (i, 0))
    nw = jnp.ones((d,), F32) if norm_w is None else norm_w
    return pl.pallas_call(
        functools.partial(_moe_combine_kernel, normed=norm_w is not None),
        grid=(nt,),
        in_specs=[row, pl.BlockSpec((tm, d // 2), lambda i: (i, 0)),
                  pl.BlockSpec((tm, d // 2), lambda i: (nt + i, 0)), col, col,
                  pl.BlockSpec((1, d), lambda i: (0, 0))],
        out_specs=row,
        out_shape=jax.ShapeDtypeStruct((t, d), F32),
        compiler_params=_cparams(("parallel",)),
        name="moe_combine",
    )(h, yg, yg, gate1, gate2, nw.reshape(1, d))


def _route(h, nw, wr, n_exp):
    xn = _rms(h, nw)
    half = xn.shape[1] // 2
    packed = _pack_bf16_pairs(xn[:, :half], xn[:, half:])
    xh = xn.astype(BF16)
    xl = (xn - xh.astype(F32)).astype(BF16)
    wh = wr.astype(BF16)
    wl = (wr - wh.astype(F32)).astype(BF16)
    logits = _dot(xh, wh) + _dot(xh, wl) + _dot(xl, wh)
    lane = lax.broadcasted_iota(jnp.int32, logits.shape, 1)
    neg = -jnp.inf
    lg = jnp.where(lane < n_exp, logits, neg)
    m1 = jnp.max(lg, axis=-1, keepdims=True)
    i1 = jnp.min(jnp.where(lg == m1, lane, LANES), axis=-1, keepdims=True)
    lg2 = jnp.where(lane == i1, neg, lg)
    m2 = jnp.max(lg2, axis=-1, keepdims=True)
    i2 = jnp.min(jnp.where(lg2 == m2, lane, LANES), axis=-1, keepdims=True)
    e2 = jnp.exp(m2 - m1)
    g1 = 1.0 / (1.0 + e2)
    g2 = e2 / (1.0 + e2)
    slot = jnp.where(lane == i1, 1, jnp.where(lane == i2, 2, 0))
    gate = jnp.where(lane == i1, g1, jnp.where(lane == i2, g2, 0.0))
    return packed, slot[:, :n_exp], gate[:, :n_exp]


def _router_kernel(h_ref, nw_ref, wr_ref, xn_ref, slot_ref, gate_ref, *, n_exp):
    xn_ref[...], slot_ref[...], gate_ref[...] = _route(h_ref[...], nw_ref[...], wr_ref[...], n_exp)


def router(h, nw, w_router, tm=1024):
    t, d = h.shape
    n_exp = w_router.shape[1]
    wr = jnp.pad(w_router, ((0, 0), (0, LANES - n_exp)))
    return pl.pallas_call(
        functools.partial(_router_kernel, n_exp=n_exp),
        grid=(t // tm,),
        in_specs=[pl.BlockSpec((tm, d), lambda i: (i, 0)),
                  pl.BlockSpec((1, d), lambda i: (0, 0)),
                  pl.BlockSpec((d, LANES), lambda i: (0, 0))],
        out_specs=[pl.BlockSpec((tm, d // 2), lambda i: (i, 0)),
                   pl.BlockSpec((tm, n_exp), lambda i: (i, 0)),
                   pl.BlockSpec((tm, n_exp), lambda i: (i, 0))],
        out_shape=[jax.ShapeDtypeStruct((t, d // 2), jnp.uint32),
                   jax.ShapeDtypeStruct((t, n_exp), jnp.int32),
                   jax.ShapeDtypeStruct((t, n_exp), F32)],
        compiler_params=_cparams(("parallel",)),
        name="router",
    )(h, nw.reshape(1, d), wr)


def _dispatch_plan(slot, gate, tm):
    t, n_exp = slot.shape
    sel = (slot > 0).astype(jnp.int32)
    counts = jnp.sum(sel, axis=0)
    rank = jnp.cumsum(sel, axis=0) - sel
    tiles = (counts + tm - 1) // tm
    tile_end = jnp.cumsum(tiles)
    start = (tile_end - tiles) * tm
    pos = start[None, :] + rank
    n_rows = 2 * t + n_exp * tm
    n_tiles = n_rows // tm
    pos1 = jnp.sum(jnp.where(slot == 1, pos, 0), axis=1)
    pos2 = jnp.sum(jnp.where(slot == 2, pos, 0), axis=1)
    gate1 = jnp.sum(jnp.where(slot == 1, gate, 0.0), axis=1, keepdims=True)
    gate2 = jnp.sum(jnp.where(slot == 2, gate, 0.0), axis=1, keepdims=True)
    tile_id = jnp.arange(n_tiles, dtype=jnp.int32)
    owner = tile_id[:, None] >= tile_end[None, :]
    tile_expert = jnp.minimum(jnp.sum(owner.astype(jnp.int32), axis=1), n_exp - 1)
    mine = tile_expert[:, None] == jnp.arange(n_exp, dtype=jnp.int32)[None, :]
    group_end = jnp.sum(jnp.where(mine, (start + counts)[None, :], 0), axis=1)
    n_valid = tile_end[-1:].astype(jnp.int32)
    tile_rows = jnp.where(tile_id < n_valid[0], jnp.clip(group_end - tile_id * tm, 0, tm), 0)
    return (n_rows, pos1, pos2, gate1, gate2, tile_expert.astype(jnp.int32), n_valid,
            tile_rows.astype(jnp.int32))


def _final_norm_kernel(h_ref, w_ref, o_ref):
    o_ref[...] = _rms(h_ref[...], w_ref[...])


def final_norm(h, w, tm=1024):
    t, d = h.shape
    return pl.pallas_call(
        _final_norm_kernel,
        grid=(t // tm,),
        in_specs=[pl.BlockSpec((tm, d), lambda i: (i, 0)), pl.BlockSpec((1, d), lambda i: (0, 0))],
        out_specs=pl.BlockSpec((tm, d), lambda i: (i, 0)),
        out_shape=jax.ShapeDtypeStruct((t, d), F32),
        compiler_params=_cparams(("parallel",)),
        name="final_norm",
    )(h, w.reshape(1, d))


def kernel(x, mem, mix_norm_w, w_in, ssm_lambda_re, ssm_lambda_im, ssm_b_re, ssm_b_im, ssm_c_re, ssm_c_im, ssm_d, ssm_log_dt, ssm_w_glu, mix_out_gain, w_out, cross_norm_w, mem_norm_w, w_cross_q, w_cross_kv, w_cross_o, ffn_norm_w, w_dense_gu, w_dense_down, w_router, w_expert_gu, w_expert_down, final_norm_w):
    batch, seq, d = x.shape
    depth = w_in.shape[0]
    mem_len = mem.shape[1]
    ssm_w = ssm_d.shape[1]
    g, p = ssm_lambda_re.shape[1:]
    nh = ssm_b_re.shape[-1]
    ret_w = ssm_w
    sb_w = d - ssm_w - ret_w
    ret_dh = ret_w // RET_HEADS
    moe_tm = FFN_TM

    w_in_b = jnp.concatenate(
        [w_in[..., :ssm_w],
         _group_rotary_halves(w_in[..., ssm_w:ssm_w + ret_w], RET_HEADS),
         _group_rotary_halves(w_in[..., ssm_w + ret_w:ssm_w + 2 * ret_w], RET_HEADS),
         w_in[..., ssm_w + 2 * ret_w:]], axis=-1).astype(BF16)
    w_out_b = w_out.astype(BF16)
    w_q_b = w_cross_q.astype(BF16)
    w_kv_b = w_cross_kv.astype(BF16)
    w_o_b = w_cross_o.astype(BF16)
    n_exp = w_expert_gu.shape[1]
    w_egu = w_expert_gu.reshape((-1,) + w_expert_gu.shape[2:])
    w_edn = w_expert_down.reshape((-1,) + w_expert_down.shape[2:])
    n_a = ssm_w + 4 * ret_w

    apow, bbc = s5_prep(ssm_lambda_re, ssm_lambda_im, ssm_log_dt, ssm_b_re, ssm_b_im,
                        nstep=S5_TIME_TILE // SUBLANES)

    mem2 = mem.reshape(batch * mem_len, d)
    h = x.reshape(batch * seq, d)
    for i in range(depth):
        p_a, p_sb = norm_proj(h, mix_norm_w[i], w_in_b, i, (n_a, 3 * sb_w), (F32, BF16))
        bb = _block_diag_in(bbc[i], g, p, nh).astype(BF16)
        cc = _block_diag_out(ssm_c_re[i], ssm_c_im[i]).astype(BF16)
        y_a = s5_mixer(p_a, bb, cc, apow[i], ssm_d[i], ssm_w_glu[i].astype(BF16), batch, seq)
        gain = mix_out_gain[i]
        y_b = retention_mixer(p_a, ssm_w, gain[ssm_w:ssm_w + ret_w], batch, seq)
        y_c = stick_breaking_mixer(p_sb, batch, seq)
        (kv,) = norm_proj(mem2, mem_norm_w, w_kv_b, i, (2 * d,), (BF16,))
        h = mix_cross(h, y_a, y_b, y_c, gain, w_out_b, cross_norm_w[i], w_q_b, kv, w_o_b, i, batch, seq)
        if i % 2 == 0:
            h = ffn_dense(h, ffn_norm_w[i], w_dense_gu, w_dense_down, i // 2)
        else:
            xn, slot, gate = router(h, ffn_norm_w[i], w_router[i // 2])
            n_rows, pos1, pos2, gate1, gate2, tile_expert, n_valid, tile_rows = _dispatch_plan(
                slot, gate, moe_tm)
            xs = sc_scatter_rows2(xn, pos1, pos2, n_rows)
            y = ffn_grouped(xs, tile_expert, n_valid, tile_rows, w_egu, w_edn, (i // 2) * n_exp,
                            tm=moe_tm)
            yg = sc_gather_rows(y, jnp.concatenate([pos1, pos2]))
            last = i == depth - 1
            h = moe_combine(h, yg, gate1, gate2, norm_w=final_norm_w if last else None)
    if depth % 2 == 1:
        h = final_norm(h, final_norm_w)
    return h.reshape(batch, seq, d)
```

```python
import functools
import math

import numpy as np
import jax
import jax.numpy as jnp
from jax import lax
from jax.experimental import pallas as pl
from jax.experimental.pallas import tpu as pltpu
from jax.experimental.pallas import tpu_sc as plsc

F32 = jnp.float32
BF16 = jnp.bfloat16

NORM_EPS = 1e-6
GN_EPS = 1e-6
ROPE_BASE = 10000.0
LOG2_E = 1.4426950408889634
SB_DEAD_LOG2 = -150.0
SB_MASKED_LOG2 = -1e30

SSM_GROUP = 16
SSM_STATE = 64
RET_HEADS = 4
SB_HEADS = 8
CROSS_HEADS = 4
N_EXPERTS = 8

V7X_VMEM_BYTES = 64 * 1024 * 1024
LANES = 128
SUBLANES = 8


def _cparams(sem, vmem_mb=48):
    return pltpu.CompilerParams(dimension_semantics=sem, vmem_limit_bytes=vmem_mb * 1024 * 1024)


def _rms(x, w):
    ms = jnp.mean(x * x, axis=-1, keepdims=True)
    return x * lax.rsqrt(ms + NORM_EPS) * w


def _dot(a, b):
    return jnp.dot(a, b, preferred_element_type=F32)


def _dot_nt(a, b):
    return lax.dot_general(a, b, (((1,), (1,)), ((), ())), preferred_element_type=F32)


def _dot_tn(a, b):
    return lax.dot_general(a, b, (((0,), (0,)), ((), ())), preferred_element_type=F32)


def _pack_bf16_pairs(a, b):
    ua = lax.bitcast_convert_type(a.astype(BF16).astype(F32), jnp.uint32)
    ub = lax.bitcast_convert_type(b.astype(BF16).astype(F32), jnp.uint32)
    return (ub & jnp.uint32(0xFFFF0000)) | (ua >> 16)


def _unpack_bf16_pairs(p):
    lo = lax.bitcast_convert_type(p << 16, F32)
    hi = lax.bitcast_convert_type(p & jnp.uint32(0xFFFF0000), F32)
    return jnp.concatenate([lo, hi], axis=1)


def _split_dot(x, m_bf16):
    hi = x.astype(BF16)
    lo = (x - hi.astype(F32)).astype(BF16)
    return _dot(hi, m_bf16) + _dot(lo, m_bf16)


def _norm_proj_kernel(x_ref, nw_ref, w_ref, *o_refs, splits, chunk):
    xn = _rms(x_ref[...], nw_ref[...]).astype(BF16)
    col = 0
    for o_ref, width in zip(o_refs, splits):
        for c0 in range(0, width, chunk):
            r = _dot(xn, w_ref[:, col + c0:col + c0 + chunk])
            o_ref[:, c0:c0 + chunk] = r.astype(o_ref.dtype)
        col += width


def norm_proj(x, nw, w, layer, splits, dtypes, tm=1024, chunk=256):
    rows, d = x.shape
    n = w.shape[2]
    assert sum(splits) == n and rows % tm == 0
    return pl.pallas_call(
        functools.partial(_norm_proj_kernel, splits=splits, chunk=chunk),
        grid=(rows // tm,),
        in_specs=[
            pl.BlockSpec((tm, d), lambda i: (i, 0)),
            pl.BlockSpec((1, d), lambda i: (0, 0)),
            pl.BlockSpec((None, d, n), lambda i: (layer, 0, 0)),
        ],
        out_specs=[pl.BlockSpec((tm, s), lambda i: (i, 0)) for s in splits],
        out_shape=[jax.ShapeDtypeStruct((rows, s), dt) for s, dt in zip(splits, dtypes)],
        compiler_params=_cparams(("parallel",)),
        name="norm_proj",
    )(x, nw.reshape(1, d), w)


def _s5_prep_kernel(lr_ref, li_ref, ldt_ref, br_ref, bi_ref, apow_ref, bb_ref):
    lr = lr_ref[0]
    li = li_ref[0]
    dt = jnp.exp(ldt_ref[0])
    mag = jnp.exp(lr * dt)
    a_re = mag * jnp.cos(li * dt)
    a_im = mag * jnp.sin(li * dt)
    denom = lr * lr + li * li
    nr = a_re - 1.0
    z_re = (nr * lr + a_im * li) / denom
    z_im = (a_im * lr - nr * li) / denom
    br = br_ref[0]
    bi = bi_ref[0]
    nh = br.shape[0]
    bb_ref[0, :nh, :] = z_re * br - z_im * bi
    bb_ref[0, nh:, :] = z_re * bi + z_im * br
    nstep = (apow_ref.shape[1] - 2 * S5_SEG_LEVELS) // 2
    pr, pi = a_re, a_im
    for j in range(nstep):
        apow_ref[0, j:j + 1, :] = pr
        apow_ref[0, nstep + j:nstep + j + 1, :] = pi
        if j + 1 < nstep:
            pr, pi = pr * a_re - pi * a_im, pr * a_im + pi * a_re
    for k in range(S5_SEG_LEVELS):
        apow_ref[0, 2 * nstep + k:2 * nstep + k + 1, :] = pr
        apow_ref[0, 2 * nstep + S5_SEG_LEVELS + k:2 * nstep + S5_SEG_LEVELS + k + 1, :] = pi
        pr, pi = pr * pr - pi * pi, 2.0 * pr * pi


S5_SEG_LEVELS = 3
S5_TIME_TILE = 512


def s5_prep(lam_re, lam_im, log_dt, b_re, b_im, nstep):
    depth, g, p = lam_re.shape
    nh = b_re.shape[-1]
    gp = g * p
    lr = lam_re.reshape(depth, 1, gp)
    li = lam_im.reshape(depth, 1, gp)
    ldt = jnp.repeat(log_dt, p, axis=1).reshape(depth, 1, gp)
    br = b_re.transpose(0, 3, 1, 2).reshape(depth, nh, gp)
    bi = b_im.transpose(0, 3, 1, 2).reshape(depth, nh, gp)
    vec = pl.BlockSpec((1, 1, gp), lambda i: (i, 0, 0))
    mat = pl.BlockSpec((1, nh, gp), lambda i: (i, 0, 0))
    npow = 2 * (nstep + S5_SEG_LEVELS)
    return pl.pallas_call(
        _s5_prep_kernel,
        grid=(depth,),
        in_specs=[vec, vec, vec, mat, mat],
        out_specs=[pl.BlockSpec((1, npow, gp), lambda i: (i, 0, 0)),
                   pl.BlockSpec((1, 2 * nh, gp), lambda i: (i, 0, 0))],
        out_shape=[jax.ShapeDtypeStruct((depth, npow, gp), F32),
                   jax.ShapeDtypeStruct((depth, 2 * nh, gp), F32)],
        compiler_params=_cparams(("arbitrary",)),
        name="s5_prep",
    )(lr, li, ldt, br, bi)


def _s5_kernel(u_ref, bb_ref, cc_ref, apow_ref, d_ref, wglu_ref, o_ref, xs_ref, carry_ref, perm_ref,
               *, gp):
    ts = u_ref.shape[0]
    nstep = ts // SUBLANES
    rows = lambda s: slice(s * SUBLANES, (s + 1) * SUBLANES)

    @pl.when(pl.program_id(1) == 0)
    def _():
        carry_ref[...] = jnp.zeros_like(carry_ref)

    ntile = u_ref.shape[1] // LANES
    for c in range(ntile):
        perm_ref[c] = u_ref[:, c * LANES:(c + 1) * LANES]
    u = jnp.concatenate(
        [jnp.concatenate([perm_ref[c, pl.ds(s, SUBLANES, stride=nstep), :] for c in range(ntile)], axis=1)
         for s in range(nstep)], axis=0)
    xs_ref[...] = _dot(u.astype(BF16), bb_ref[...])

    ar = apow_ref[0:1, :]
    ai = apow_ref[nstep:nstep + 1, :]
    xr = xs_ref[rows(0), :gp]
    xi = xs_ref[rows(0), gp:]
    for s in range(1, nstep):
        xr, xi = (ar * xr - ai * xi + xs_ref[rows(s), :gp], ar * xi + ai * xr + xs_ref[rows(s), gp:])
        xs_ref[rows(s), :gp] = xr
        xs_ref[rows(s), gp:] = xi

    seg = lax.broadcasted_iota(jnp.int32, (SUBLANES, 1), 0)
    cr = carry_ref[0:1, :]
    ci = carry_ref[1:2, :]
    lr = apow_ref[2 * nstep:2 * nstep + 1, :]
    li = apow_ref[2 * nstep + S5_SEG_LEVELS:2 * nstep + S5_SEG_LEVELS + 1, :]
    first = seg == 0
    xr = xr + jnp.where(first, lr * cr - li * ci, 0.0)
    xi = xi + jnp.where(first, lr * ci + li * cr, 0.0)
    for k in range(S5_SEG_LEVELS):
        sh = 1 << k
        pr = apow_ref[2 * nstep + k:2 * nstep + k + 1, :]
        pi = apow_ref[2 * nstep + S5_SEG_LEVELS + k:2 * nstep + S5_SEG_LEVELS + k + 1, :]
        keep = seg >= sh
        sr = jnp.where(keep, pltpu.roll(xr, sh, axis=0), 0.0)
        si = jnp.where(keep, pltpu.roll(xi, sh, axis=0), 0.0)
        xr, xi = xr + pr * sr - pi * si, xi + pr * si + pi * sr
    carry_ref[0:1, :] = xr[SUBLANES - 1:, :]
    carry_ref[1:2, :] = xi[SUBLANES - 1:, :]
    enter_r = jnp.where(first, cr, pltpu.roll(xr, 1, axis=0))
    enter_i = jnp.where(first, ci, pltpu.roll(xi, 1, axis=0))

    for s in range(nstep):
        pr = apow_ref[s:s + 1, :]
        pi = apow_ref[nstep + s:nstep + s + 1, :]
        xs_ref[rows(s), :gp] += pr * enter_r - pi * enter_i
        xs_ref[rows(s), gp:] += pr * enter_i + pi * enter_r

    y = _dot(xs_ref[...].astype(BF16), cc_ref[...]) + d_ref[...] * u
    g = jax.nn.gelu(y)
    out = g * jax.nn.sigmoid(_dot(g.astype(BF16), wglu_ref[...]))
    for s in range(nstep):
        for c in range(ntile):
            perm_ref[c, pl.ds(s, SUBLANES, stride=nstep), :] = out[rows(s), c * LANES:(c + 1) * LANES]
    for c in range(ntile):
        o_ref[:, c * LANES:(c + 1) * LANES] = perm_ref[c]


def s5_mixer(p_a, bb, cc, apow, d_skip, w_glu, batch, seq):
    w = d_skip.shape[-1]
    gp = apow.shape[-1]
    ts = (apow.shape[0] // 2 - S5_SEG_LEVELS) * SUBLANES
    nt = seq // ts
    return pl.pallas_call(
        functools.partial(_s5_kernel, gp=gp),
        grid=(batch, nt),
        in_specs=[
            pl.BlockSpec((ts, w), lambda b, t: (b * nt + t, 0)),
            pl.BlockSpec((w, 2 * gp), lambda b, t: (0, 0)),
            pl.BlockSpec((2 * gp, w), lambda b, t: (0, 0)),
            pl.BlockSpec(apow.shape, lambda b, t: (0, 0)),
            pl.BlockSpec((1, w), lambda b, t: (0, 0)),
            pl.BlockSpec((w, w), lambda b, t: (0, 0)),
        ],
        out_specs=pl.BlockSpec((ts, w), lambda b, t: (b * nt + t, 0)),
        out_shape=jax.ShapeDtypeStruct((batch * seq, w), F32),
        scratch_shapes=[pltpu.VMEM((ts, 2 * gp), F32), pltpu.VMEM((2, gp), F32),
                        pltpu.VMEM((w // LANES, ts, LANES), F32)],
        compiler_params=_cparams(("parallel", "arbitrary")),
        name="s5_mixer",
    )(p_a, bb, cc, apow, d_skip.reshape(1, w), w_glu)


def _block_diag_in(bb, g, p, nh):
    eye = jnp.eye(g, dtype=F32)
    def one(m):
        m = m.reshape(nh, g, p)
        return jnp.einsum('hgp,kg->khgp', m, eye).reshape(g * nh, g * p)
    return jnp.concatenate([one(bb[:nh]), one(bb[nh:])], axis=1)


def _block_diag_out(c_re, c_im):
    g, nh, p = c_re.shape
    eye = jnp.eye(g, dtype=F32)
    def one(m):
        return jnp.einsum('ghp,gk->gpkh', m, eye).reshape(g * p, g * nh)
    return jnp.concatenate([one(c_re), one(-c_im)], axis=0)


def _ret_tables(seq, chunk, heads, dh):
    half = dh // 2
    w = heads * dh
    hw = w // 2
    inv_freq = ROPE_BASE ** (-np.arange(half, dtype=np.float64) / half)
    ang = np.arange(seq, dtype=np.float64)[:, None] * np.tile(inv_freq, heads)[None, :]
    cos = np.cos(ang).astype(np.float32)
    sin = np.sin(ang).astype(np.float32)
    log_gamma = np.log1p(-(2.0 ** (-5.0 - np.arange(heads, dtype=np.float64))))
    idx = np.arange(chunk, dtype=np.float64)
    rel = idx[:, None] - idx[None, :]
    intra = np.where(rel >= 0, np.exp(log_gamma[:, None, None] * np.maximum(rel, 0.0)), 0.0)
    head_nat = np.arange(w) // dh
    head_rot = (np.arange(w) % hw) // half
    xi = np.exp(log_gamma[head_nat][None, :] * (idx[:, None] + 1.0))
    zeta = np.exp(log_gamma[head_rot][None, :] * (chunk - 1.0 - idx[:, None]))
    decay = np.exp(log_gamma[head_nat] * chunk)[None, :]
    same = (head_rot[:, None] == head_nat[None, :]).astype(np.float32)
    gavg = (head_nat[:, None] == head_nat[None, :]).astype(np.float32) / dh
    f = lambda a: jnp.asarray(a, dtype=F32)
    return dict(cos=f(cos), sin=f(sin), intra=f(intra), xi=f(xi), zeta=f(zeta), decay=f(decay),
                same=f(same), gavg=jnp.asarray(gavg, dtype=BF16))


def _ret_kernel(q_ref, k_ref, v_ref, g_ref, cos_ref, sin_ref, intra_ref, xi_ref, zeta_ref,
                decay_ref, same_ref, gavg_ref, gnw_ref, o_ref, state_ref, *, heads, dh):
    w = heads * dh
    hw = w // 2
    half = dh // 2

    @pl.when(pl.program_id(1) == 0)
    def _():
        state_ref[...] = jnp.zeros_like(state_ref)

    cos = cos_ref[...]
    sin = sin_ref[...]

    def rot(x):
        x1, x2 = x[:, :hw], x[:, hw:]
        return jnp.concatenate([x1 * cos - x2 * sin, x1 * sin + x2 * cos], axis=1)

    lane = lax.broadcasted_iota(jnp.int32, (1, w), 1)
    head_rot = (lane % hw) // half
    head_nat = lane // dh
    gavg = gavg_ref[...]

    for b in range(q_ref.shape[0]):
        q = rot(q_ref[b])
        k = rot(k_ref[b]) * (dh ** -0.5)
        qb = q.astype(BF16)
        kb = k.astype(BF16)
        vb = v_ref[b].astype(BF16)

        o = _dot(qb, state_ref[b].astype(BF16)) * xi_ref[...]
        for h in range(heads):
            qh = jnp.where(head_rot == h, qb, jnp.zeros_like(qb))
            vh = jnp.where(head_nat == h, vb, jnp.zeros_like(vb))
            s = _dot_nt(qh, kb) * intra_ref[h]
            o = o + _dot(s.astype(BF16), vh)
        kz = (k * zeta_ref[...]).astype(BF16)
        state_ref[b] = decay_ref[...] * state_ref[b] + _dot_tn(kz, vb) * same_ref[...]

        mu = _split_dot(o, gavg)
        d = o - mu
        var = _split_dot(d * d, gavg)
        on = d * lax.rsqrt(var + GN_EPS) * gnw_ref[...]
        o_ref[b] = jax.nn.silu(g_ref[b]) * on


def retention_mixer(p_a, col0, gn_w, batch, seq, heads=RET_HEADS, chunk=256, nb=2):
    w = gn_w.shape[-1]
    dh = w // heads
    nc = seq // chunk
    tb = _ret_tables(seq, chunk, heads, dh)
    cb = col0 // w
    assert col0 % w == 0 and batch % nb == 0
    p3 = p_a.reshape(batch, seq, p_a.shape[1])
    row = lambda j: pl.BlockSpec((nb, chunk, w), lambda b, c, j=j: (b, c, cb + j))
    const = lambda shape: pl.BlockSpec(shape, lambda b, c: (0,) * len(shape))
    return pl.pallas_call(
        functools.partial(_ret_kernel, heads=heads, dh=dh),
        grid=(batch // nb, nc),
        in_specs=[row(0), row(1), row(2), row(3),
                  pl.BlockSpec((chunk, w // 2), lambda b, c: (c, 0)),
                  pl.BlockSpec((chunk, w // 2), lambda b, c: (c, 0)),
                  const((heads, chunk, chunk)), const((chunk, w)), const((chunk, w)),
                  const((1, w)), const((w, w)), const((w, w)), const((1, w))],
        out_specs=pl.BlockSpec((nb, chunk, w), lambda b, c: (b, c, 0)),
        out_shape=jax.ShapeDtypeStruct((batch, seq, w), F32),
        scratch_shapes=[pltpu.VMEM((nb, w, w), F32)],
        compiler_params=_cparams(("parallel", "arbitrary")),
        name="retention_mixer",
    )(p3, p3, p3, p3, tb["cos"], tb["sin"], tb["intra"], tb["xi"], tb["zeta"], tb["decay"],
      tb["same"], tb["gavg"], gn_w.reshape(1, w)).reshape(batch * seq, w)


def _group_rotary_halves(w, heads):
    lead = w.shape[:-1]
    dh = w.shape[-1] // heads
    return w.reshape(lead + (heads, 2, dh // 2)).swapaxes(-3, -2).reshape(lead + (heads * dh,))


def _sb_kernel(q_ref, k_ref, v_ref, o_ref, *, dh, blk):
    lane = lax.broadcasted_iota(jnp.int32, (1, 2 * dh), 1)
    r_id = lax.broadcasted_iota(jnp.int32, (blk, blk), 0)
    c_id = lax.broadcasted_iota(jnp.int32, (blk, blk), 1)
    neg_upper = jnp.where(r_id > c_id, -1.0, 0.0).astype(BF16)
    causal = c_id < r_id
    zscale = (dh ** -0.5) * LOG2_E
    hsel = [(lane // dh) == hh for hh in range(2)]

    def pair(qm, j, runs, diag):
        off = j * blk if isinstance(j, int) else pl.multiple_of(j * blk, blk)
        kb = k_ref[pl.ds(off, blk), :]
        vb = v_ref[pl.ds(off, blk), :]
        wgts, vms, new_runs = [], [], []
        for hh in range(2):
            zs = _dot_nt(qm[hh], kb) * zscale
            nfail = jnp.maximum(zs, 0.0) + jnp.log2(1.0 + jnp.exp2(jnp.minimum(zs, -zs)))
            if diag:
                nfail = jnp.where(causal, nfail, 0.0)
            after = _dot(nfail.astype(BF16), neg_upper) + runs[hh]
            wgt = jnp.exp2((zs - nfail) + after)
            if diag:
                wgt = jnp.where(causal, wgt, 0.0)
            wgts.append(wgt.astype(BF16))
            vms.append(jnp.where(hsel[hh], vb, jnp.zeros_like(vb)))
            new_runs.append(runs[hh] - jnp.sum(nfail, axis=1, keepdims=True))
        contrib = _dot(jnp.concatenate(wgts, axis=1), jnp.concatenate(vms, axis=0))
        return contrib, new_runs

    zero = jnp.zeros((blk, 1), F32)
    started = []
    for i in range(q_ref.shape[0] // blk):
        q2 = q_ref[i * blk:(i + 1) * blk, :]
        qm = [jnp.where(hs, q2, jnp.zeros_like(q2)) for hs in hsel]
        acc, runs = pair(qm, i, [zero, zero], True)
        if i >= 1:
            acc1, runs = pair(qm, i - 1, runs, False)
            acc = acc + acc1
        started.append((qm, acc, runs))

    for i, (qm, acc, (ra, rb)) in enumerate(started):
        if i >= 2:
            def cond(carry, i=i):
                jj, _, ra, rb = carry
                alive = jnp.maximum(jnp.max(ra), jnp.max(rb)) > SB_DEAD_LOG2
                return jnp.logical_and(jj <= i, alive)

            def body(carry, i=i, qm=qm):
                jj, acc, ra, rb = carry
                c, (ra, rb) = pair(qm, i - jj, [ra, rb], False)
                return jj + 1, acc + c, ra, rb

            _, acc, _, _ = lax.while_loop(cond, body, (jnp.int32(2), acc, ra, rb))
        o_ref[i * blk:(i + 1) * blk, :] = acc


def stick_breaking_mixer(p_sb, batch, seq, heads=SB_HEADS, blk=256):
    w = p_sb.shape[1] // 3
    dh = w // heads
    pw = 2 * dh
    assert pw == LANES and seq % blk == 0
    npair = heads // 2
    return pl.pallas_call(
        functools.partial(_sb_kernel, dh=dh, blk=blk),
        grid=(batch, npair),
        in_specs=[
            pl.BlockSpec((seq, pw), lambda b, p: (b, p)),
            pl.BlockSpec((seq, pw), lambda b, p: (b, npair + p)),
            pl.BlockSpec((seq, pw), lambda b, p: (b, 2 * npair + p)),
        ],
        out_specs=pl.BlockSpec((seq, pw), lambda b, p: (b, p)),
        out_shape=jax.ShapeDtypeStruct((batch * seq, w), F32),
        compiler_params=_cparams(("parallel", "parallel")),
        name="stick_breaking_mixer",
    )(p_sb, p_sb, p_sb)


def _mix_cross_kernel(h_ref, ya_ref, yb_ref, yc_ref, gain_ref, wout_ref, cnw_ref, wq_ref, kv_ref,
                      wo_ref, o_ref, *, heads):
    wa = ya_ref.shape[1]
    wb = yb_ref.shape[1]
    d = h_ref.shape[1]
    gain = gain_ref[...]
    ya = _rms(ya_ref[...], gain[:, :wa]).astype(BF16)
    yb = yb_ref[...].astype(BF16)
    yc = _rms(yc_ref[...], gain[:, wa + wb:]).astype(BF16)
    h1 = (h_ref[...] + _dot(ya, wout_ref[:wa, :]) + _dot(yb, wout_ref[wa:wa + wb, :])
          + _dot(yc, wout_ref[wa + wb:, :]))
    q = _dot(_rms(h1, cnw_ref[...]).astype(BF16), wq_ref[...])
    dh = d // heads
    outs = []
    for hd in range(heads):
        qh = q[:, hd * dh:(hd + 1) * dh].astype(BF16)
        kh = kv_ref[:, hd * dh:(hd + 1) * dh]
        vh = kv_ref[:, d + hd * dh:d + (hd + 1) * dh]
        s = _dot_nt(qh, kh) * (dh ** -0.5)
        s = s - jnp.max(s, axis=-1, keepdims=True)
        e = jnp.exp(s)
        p = e / jnp.sum(e, axis=-1, keepdims=True)
        outs.append(_dot(p.astype(BF16), vh).astype(BF16))
    o_ref[...] = h1 + _dot(jnp.concatenate(outs, axis=1), wo_ref[...])


def mix_cross(h, ya, yb, yc, gain, w_out, cnw, wq, kv, wo, layer, batch, seq, heads=CROSS_HEADS, tm=1024):
    t, d = h.shape
    m = kv.shape[0] // batch
    nt = seq // tm
    rows = lambda wd: pl.BlockSpec((tm, wd), lambda i: (i, 0))
    const = lambda shape: pl.BlockSpec(shape, lambda i: (0, 0))
    stack = pl.BlockSpec((None, d, d), lambda i: (layer, 0, 0))
    return pl.pallas_call(
        functools.partial(_mix_cross_kernel, heads=heads),
        grid=(t // tm,),
        in_specs=[rows(d), rows(ya.shape[1]), rows(yb.shape[1]), rows(yc.shape[1]),
                  const((1, d)), stack, const((1, d)), stack,
                  pl.BlockSpec((m, 2 * d), lambda i: (i // nt, 0)),
                  stack],
        out_specs=rows(d),
        out_shape=jax.ShapeDtypeStruct((t, d), F32),
        compiler_params=_cparams(("parallel",)),
        name="mix_cross",
    )(h, ya, yb, yc, gain.reshape(1, d), w_out, cnw.reshape(1, d), wq, kv, wo)


FFN_TM = 1024
FFN_TF = 1408
FFN_CHUNK = 256
FFN_VMEM_MB = 60


def _swiglu_accumulate(x, wg_ref, wu_ref, wd_ref, o_ref):
    tf = wd_ref.shape[0]
    for c0 in range(0, tf, FFN_CHUNK):
        c1 = min(c0 + FFN_CHUNK, tf)
        wg = wg_ref[:, c0:c1].astype(BF16)
        wu = wu_ref[:, c0:c1].astype(BF16)
        act = jax.nn.silu(_dot(x, wg)) * _dot(x, wu)
        o_ref[...] += _dot(act.astype(BF16), wd_ref[c0:c1, :].astype(BF16))


def _ffn_dense_kernel(h_ref, nw_ref, wg_ref, wu_ref, wd_ref, o_ref, xn_ref):
    f = pl.program_id(1)

    @pl.when(f == 0)
    def _():
        h = h_ref[...]
        xn_ref[...] = _rms(h, nw_ref[...]).astype(BF16)
        o_ref[...] = h

    _swiglu_accumulate(xn_ref[...], wg_ref, wu_ref, wd_ref, o_ref)


def ffn_dense(h, nw, w_gu, w_down, layer, tm=FFN_TM, tf=FFN_TF):
    t, d = h.shape
    ff = w_down.shape[1]
    nf = ff // tf
    return pl.pallas_call(
        _ffn_dense_kernel,
        grid=(t // tm, nf),
        in_specs=[
            pl.BlockSpec((tm, d), lambda i, f: (i, 0)),
            pl.BlockSpec((1, d), lambda i, f: (0, 0)),
            pl.BlockSpec((None, d, tf), lambda i, f: (layer, 0, f)),
            pl.BlockSpec((None, d, tf), lambda i, f: (layer, 0, nf + f)),
            pl.BlockSpec((None, tf, d), lambda i, f: (layer, f, 0)),
        ],
        out_specs=pl.BlockSpec((tm, d), lambda i, f: (i, 0)),
        out_shape=jax.ShapeDtypeStruct((t, d), F32),
        scratch_shapes=[pltpu.VMEM((tm, d), BF16)],
        compiler_params=_cparams(("parallel", "arbitrary"), FFN_VMEM_MB),
        name="ffn_dense",
    )(h, nw.reshape(1, d), w_gu, w_gu, w_down)


def _ffn_group_kernel(te_ref, nv_ref, rows_ref, x_ref, wg_ref, wu_ref, wd_ref, y_ref, xb_ref, acc_ref):
    i = pl.program_id(0)
    f = pl.program_id(1)
    valid = i < nv_ref[0]

    @pl.when(f == 0)
    def _():
        row = lax.broadcasted_iota(jnp.int32, (x_ref.shape[0], 1), 0)
        xb_ref[...] = jnp.where(row < rows_ref[i], _unpack_bf16_pairs(x_ref[...]), 0.0).astype(BF16)
        acc_ref[...] = jnp.zeros_like(acc_ref)

    @pl.when(valid)
    def _():
        _swiglu_accumulate(xb_ref[...], wg_ref.at[0], wu_ref.at[0], wd_ref.at[0], acc_ref)

    @pl.when(f == pl.num_programs(1) - 1)
    def _():
        half = acc_ref.shape[1] // 2
        y_ref[...] = _pack_bf16_pairs(acc_ref[:, :half], acc_ref[:, half:])


def ffn_grouped(xs, tile_expert, n_valid, tile_rows, w_gu, w_down, expert_base, tm=FFN_TM, tf=FFN_TF):
    p = xs.shape[0]
    d = 2 * xs.shape[1]
    ff = w_down.shape[1]
    nf = ff // tf

    def live(i, f, nv):
        ok = i < nv[0]
        return jnp.where(ok, i, nv[0] - 1), jnp.where(ok, f, nf - 1)

    def x_map(i, f, te, nv, rows):
        ii, _ = live(i, f, nv)
        return ii, 0

    def wg_map(i, f, te, nv, rows):
        ii, ff_ = live(i, f, nv)
        return expert_base + te[ii], 0, ff_

    def wu_map(i, f, te, nv, rows):
        ii, ff_ = live(i, f, nv)
        return expert_base + te[ii], 0, nf + ff_

    def wd_map(i, f, te, nv, rows):
        ii, ff_ = live(i, f, nv)
        return expert_base + te[ii], ff_, 0

    grid_spec = pltpu.PrefetchScalarGridSpec(
        num_scalar_prefetch=3,
        grid=(p // tm, nf),
        in_specs=[
            pl.BlockSpec((tm, d // 2), x_map),
            pl.BlockSpec((1, d, tf), wg_map),
            pl.BlockSpec((1, d, tf), wu_map),
            pl.BlockSpec((1, tf, d), wd_map),
        ],
        out_specs=pl.BlockSpec((tm, d // 2), lambda i, f, te, nv, rows: (i, 0)),
        scratch_shapes=[pltpu.VMEM((tm, d), BF16), pltpu.VMEM((tm, d), F32)],
    )
    return pl.pallas_call(
        _ffn_group_kernel,
        grid_spec=grid_spec,
        out_shape=jax.ShapeDtypeStruct((p, d // 2), jnp.uint32),
        compiler_params=_cparams(("arbitrary", "arbitrary"), FFN_VMEM_MB),
        name="ffn_grouped",
    )(tile_expert, n_valid, tile_rows, xs, w_gu, w_gu, w_down)


V7X_SC_CORES = 2
V7X_SC_SUBCORES = 16
V7X_SC_WORKERS = V7X_SC_CORES * V7X_SC_SUBCORES


def _sc_mesh():
    return plsc.VectorSubcoreMesh(core_axis_name="c", subcore_axis_name="s")


def sc_scatter_rows2(x, idx1, idx2, n_out, ch=128):
    t, d = x.shape
    assert t % (V7X_SC_WORKERS * ch) == 0
    t_per_w = t // V7X_SC_WORKERS
    n_ch = t_per_w // ch

    @functools.partial(
        pl.kernel, mesh=_sc_mesh(), out_type=jax.ShapeDtypeStruct((n_out, d), x.dtype),
        scratch_types=[pltpu.VMEM((ch,), jnp.int32), pltpu.VMEM((ch,), jnp.int32),
                       pltpu.VMEM((ch, d), x.dtype), pltpu.SemaphoreType.DMA])
    def k(x_hbm, i1_hbm, i2_hbm, out_hbm, i1_v, i2_v, rows_v, sem):
        wid = lax.axis_index("s") * V7X_SC_CORES + lax.axis_index("c")
        base = wid * t_per_w

        @pl.loop(0, n_ch)
        def _(j):
            off = pl.multiple_of(base + j * ch, 8)
            pltpu.sync_copy(i1_hbm.at[pl.ds(off, ch)], i1_v)
            pltpu.sync_copy(i2_hbm.at[pl.ds(off, ch)], i2_v)
            pltpu.sync_copy(x_hbm.at[pl.ds(off, ch)], rows_v)
            pltpu.async_copy(rows_v, out_hbm.at[i1_v], sem).wait()
            pltpu.async_copy(rows_v, out_hbm.at[i2_v], sem).wait()

    return k(x, idx1, idx2)


def sc_gather_rows(table, idx, ch=128):
    _, d = table.shape
    b = idx.shape[0]
    assert b % (V7X_SC_WORKERS * ch) == 0
    b_per_w = b // V7X_SC_WORKERS
    n_ch = b_per_w // ch

    @functools.partial(
        pl.kernel, mesh=_sc_mesh(), out_type=jax.ShapeDtypeStruct((b, d), table.dtype),
        scratch_types=[pltpu.VMEM((ch,), jnp.int32), pltpu.VMEM((ch, d), table.dtype),
                       pltpu.SemaphoreType.DMA])
    def k(table_hbm, idx_hbm, out_hbm, idx_v, rows_v, sem):
        wid = lax.axis_index("s") * V7X_SC_CORES + lax.axis_index("c")
        base = wid * b_per_w

        @pl.loop(0, n_ch)
        def _(j):
            off = pl.multiple_of(base + j * ch, 8)
            pltpu.sync_copy(idx_hbm.at[pl.ds(off, ch)], idx_v)
            pltpu.async_copy(table_hbm.at[idx_v], rows_v, sem).wait()
            pltpu.sync_copy(rows_v, out_hbm.at[pl.ds(off, ch)])

    return k(table, idx)


def _moe_combine_kernel(h_ref, y1_ref, y2_ref, g1_ref, g2_ref, nw_ref, o_ref, *, normed):
    out = (h_ref[...] + g1_ref[...] * _unpack_bf16_pairs(y1_ref[...])
           + g2_ref[...] * _unpack_bf16_pairs(y2_ref[...]))
    o_ref[...] = _rms(out, nw_ref[...]) if normed else out


def moe_combine(h, yg, gate1, gate2, norm_w=None, tm=1024):
    t, d = h.shape
    nt = t // tm
    row = pl.BlockSpec((tm, d), lambda i: (i, 0))
    col = pl.BlockSpec((tm, 1), lambda i: (i, 0))
    nw = jnp.ones((d,), F32) if norm_w is None else norm_w
    return pl.pallas_call(
        functools.partial(_moe_combine_kernel, normed=norm_w is not None),
        grid=(nt,),
        in_specs=[row, pl.BlockSpec((tm, d // 2), lambda i: (i, 0)),
                  pl.BlockSpec((tm, d // 2), lambda i: (nt + i, 0)), col, col,
                  pl.BlockSpec((1, d), lambda i: (0, 0))],
        out_specs=row,
        out_shape=jax.ShapeDtypeStruct((t, d), F32),
        compiler_params=_cparams(("parallel",)),
        name="moe_combine",
    )(h, yg, yg, gate1, gate2, nw.reshape(1, d))


def _route(h, nw, wr, n_exp):
    xn = _rms(h, nw)
    half = xn.shape[1] // 2
    packed = _pack_bf16_pairs(xn[:, :half], xn[:, half:])
    xh = xn.astype(BF16)
    xl = (xn - xh.astype(F32)).astype(BF16)
    wh = wr.astype(BF16)
    wl = (wr - wh.astype(F32)).astype(BF16)
    logits = _dot(xh, wh) + _dot(xh, wl) + _dot(xl, wh)
    lane = lax.broadcasted_iota(jnp.int32, logits.shape, 1)
    neg = -jnp.inf
    lg = jnp.where(lane < n_exp, logits, neg)
    m1 = jnp.max(lg, axis=-1, keepdims=True)
    i1 = jnp.min(jnp.where(lg == m1, lane, LANES), axis=-1, keepdims=True)
    lg2 = jnp.where(lane == i1, neg, lg)
    m2 = jnp.max(lg2, axis=-1, keepdims=True)
    i2 = jnp.min(jnp.where(lg2 == m2, lane, LANES), axis=-1, keepdims=True)
    e2 = jnp.exp(m2 - m1)
    g1 = 1.0 / (1.0 + e2)
    g2 = e2 / (1.0 + e2)
    slot = jnp.where(lane == i1, 1, jnp.where(lane == i2, 2, 0))
    gate = jnp.where(lane == i1, g1, jnp.where(lane == i2, g2, 0.0))
    return packed, slot[:, :n_exp], gate[:, :n_exp]


def _router_kernel(h_ref, nw_ref, wr_ref, xn_ref, slot_ref, gate_ref, *, n_exp):
    xn_ref[...], slot_ref[...], gate_ref[...] = _route(h_ref[...], nw_ref[...], wr_ref[...], n_exp)


def router(h, nw, w_router, tm=1024):
    t, d = h.shape
    n_exp = w_router.shape[1]
    wr = jnp.pad(w_router, ((0, 0), (0, LANES - n_exp)))
    return pl.pallas_call(
        functools.partial(_router_kernel, n_exp=n_exp),
        grid=(t // tm,),
        in_specs=[pl.BlockSpec((tm, d), lambda i: (i, 0)),
                  pl.BlockSpec((1, d), lambda i: (0, 0)),
                  pl.BlockSpec((d, LANES), lambda i: (0, 0))],
        out_specs=[pl.BlockSpec((tm, d // 2), lambda i: (i, 0)),
                   pl.BlockSpec((tm, n_exp), lambda i: (i, 0)),
                   pl.BlockSpec((tm, n_exp), lambda i: (i, 0))],
        out_shape=[jax.ShapeDtypeStruct((t, d // 2), jnp.uint32),
                   jax.ShapeDtypeStruct((t, n_exp), jnp.int32),
                   jax.ShapeDtypeStruct((t, n_exp), F32)],
        compiler_params=_cparams(("parallel",)),
        name="router",
    )(h, nw.reshape(1, d), wr)


def _dispatch_plan(slot, gate, tm):
    t, n_exp = slot.shape
    sel = (slot > 0).astype(jnp.int32)
    counts = jnp.sum(sel, axis=0)
    rank = jnp.cumsum(sel, axis=0) - sel
    tiles = (counts + tm - 1) // tm
    tile_end = jnp.cumsum(tiles)
    start = (tile_end - tiles) * tm
    pos = start[None, :] + rank
    n_rows = 2 * t + n_exp * tm
    n_tiles = n_rows // tm
    pos1 = jnp.sum(jnp.where(slot == 1, pos, 0), axis=1)
    pos2 = jnp.sum(jnp.where(slot == 2, pos, 0), axis=1)
    gate1 = jnp.sum(jnp.where(slot == 1, gate, 0.0), axis=1, keepdims=True)
    gate2 = jnp.sum(jnp.where(slot == 2, gate, 0.0), axis=1, keepdims=True)
    tile_id = jnp.arange(n_tiles, dtype=jnp.int32)
    owner = tile_id[:, None] >= tile_end[None, :]
    tile_expert = jnp.minimum(jnp.sum(owner.astype(jnp.int32), axis=1), n_exp - 1)
    mine = tile_expert[:, None] == jnp.arange(n_exp, dtype=jnp.int32)[None, :]
    group_end = jnp.sum(jnp.where(mine, (start + counts)[None, :], 0), axis=1)
    n_valid = tile_end[-1:].astype(jnp.int32)
    tile_rows = jnp.where(tile_id < n_valid[0], jnp.clip(group_end - tile_id * tm, 0, tm), 0)
    return (n_rows, pos1, pos2, gate1, gate2, tile_expert.astype(jnp.int32), n_valid,
            tile_rows.astype(jnp.int32))


def _final_norm_kernel(h_ref, w_ref, o_ref):
    o_ref[...] = _rms(h_ref[...], w_ref[...])


def final_norm(h, w, tm=1024):
    t, d = h.shape
    return pl.pallas_call(
        _final_norm_kernel,
        grid=(t // tm,),
        in_specs=[pl.BlockSpec((tm, d), lambda i: (i, 0)), pl.BlockSpec((1, d), lambda i: (0, 0))],
        out_specs=pl.BlockSpec((tm, d), lambda i: (i, 0)),
        out_shape=jax.ShapeDtypeStruct((t, d), F32),
        compiler_params=_cparams(("parallel",)),
        name="final_norm",
    )(h, w.reshape(1, d))


def kernel(x, mem, mix_norm_w, w_in, ssm_lambda_re, ssm_lambda_im, ssm_b_re, ssm_b_im, ssm_c_re, ssm_c_im, ssm_d, ssm_log_dt, ssm_w_glu, mix_out_gain, w_out, cross_norm_w, mem_norm_w, w_cross_q, w_cross_kv, w_cross_o, ffn_norm_w, w_dense_gu, w_dense_down, w_router, w_expert_gu, w_expert_down, final_norm_w):
    batch, seq, d = x.shape
    depth = w_in.shape[0]
    mem_len = mem.shape[1]
    ssm_w = ssm_d.shape[1]
    g, p = ssm_lambda_re.shape[1:]
    nh = ssm_b_re.shape[-1]
    ret_w = ssm_w
    sb_w = d - ssm_w - ret_w
    ret_dh = ret_w // RET_HEADS
    moe_tm = FFN_TM

    w_in_b = jnp.concatenate(
        [w_in[..., :ssm_w],
         _group_rotary_halves(w_in[..., ssm_w:ssm_w + ret_w], RET_HEADS),
         _group_rotary_halves(w_in[..., ssm_w + ret_w:ssm_w + 2 * ret_w], RET_HEADS),
         w_in[..., ssm_w + 2 * ret_w:]], axis=-1).astype(BF16)
    w_out_b = w_out.astype(BF16)
    w_q_b = w_cross_q.astype(BF16)
    w_kv_b = w_cross_kv.astype(BF16)
    w_o_b = w_cross_o.astype(BF16)
    n_exp = w_expert_gu.shape[1]
    w_egu = w_expert_gu.reshape((-1,) + w_expert_gu.shape[2:])
    w_edn = w_expert_down.reshape((-1,) + w_expert_down.shape[2:])
    n_a = ssm_w + 4 * ret_w

    apow, bbc = s5_prep(ssm_lambda_re, ssm_lambda_im, ssm_log_dt, ssm_b_re, ssm_b_im,
                        nstep=S5_TIME_TILE // SUBLANES)

    mem2 = mem.reshape(batch * mem_len, d)
    h = x.reshape(batch * seq, d)
    for i in range(depth):
        p_a, p_sb = norm_proj(h, mix_norm_w[i], w_in_b, i, (n_a, 3 * sb_w), (F32, BF16))
        bb = _block_diag_in(bbc[i], g, p, nh).astype(BF16)
        cc = _block_diag_out(ssm_c_re[i], ssm_c_im[i]).astype(BF16)
        y_a = s5_mixer(p_a, bb, cc, apow[i], ssm_d[i], ssm_w_glu[i].astype(BF16), batch, seq)
        gain = mix_out_gain[i]
        y_b = retention_mixer(p_a, ssm_w, gain[ssm_w:ssm_w + ret_w], batch, seq)
        y_c = stick_breaking_mixer(p_sb, batch, seq)
        (kv,) = norm_proj(mem2, mem_norm_w, w_kv_b, i, (2 * d,), (BF16,))
        h = mix_cross(h, y_a, y_b, y_c, gain, w_out_b, cross_norm_w[i], w_q_b, kv, w_o_b, i, batch, seq)
        if i % 2 == 0:
            h = ffn_dense(h, ffn_norm_w[i], w_dense_gu, w_dense_down, i // 2)
        else:
            xn, slot, gate = router(h, ffn_norm_w[i], w_router[i // 2])
            n_rows, pos1, pos2, gate1, gate2, tile_expert, n_valid, tile_rows = _dispatch_plan(
                slot, gate, moe_tm)
            xs = sc_scatter_rows2(xn, pos1, pos2, n_rows)
            y = ffn_grouped(xs, tile_expert, n_valid, tile_rows, w_egu, w_edn, (i // 2) * n_exp,
                            tm=moe_tm)
            yg = sc_gather_rows(y, jnp.concatenate([pos1, pos2]))
            last = i == depth - 1
            h = moe_combine(h, yg, gate1, gate2, norm_w=final_norm_w if last else None)
    if depth % 2 == 1:
        h = final_norm(h, final_norm_w)
    return h.reshape(batch, seq, d)
```

```python
import functools

import numpy as np
import jax
import jax.numpy as jnp
from jax import lax
from jax.experimental import pallas as pl
from jax.experimental.pallas import tpu as pltpu
from jax.experimental.pallas import tpu_sc as plsc

F32 = jnp.float32
BF16 = jnp.bfloat16

NORM_EPS = 1e-6
GN_EPS = 1e-6
ROPE_BASE = 10000.0
LOG2_E = 1.4426950408889634
SB_DEAD_LOG2 = -150.0

RET_HEADS = 4
SB_HEADS = 8
CROSS_HEADS = 4

LANES = 128
SUBLANES = 8


def _cparams(sem, vmem_mb=48):
    return pltpu.CompilerParams(dimension_semantics=sem, vmem_limit_bytes=vmem_mb * 1024 * 1024)


def _rms(x, w):
    ms = jnp.mean(x * x, axis=-1, keepdims=True)
    return x * lax.rsqrt(ms + NORM_EPS) * w


def _dot(a, b):
    return jnp.dot(a, b, preferred_element_type=F32)


def _dot_nt(a, b):
    return lax.dot_general(a, b, (((1,), (1,)), ((), ())), preferred_element_type=F32)


def _dot_tn(a, b):
    return lax.dot_general(a, b, (((0,), (0,)), ((), ())), preferred_element_type=F32)


def _pack_bf16_pairs(a, b):
    ua = lax.bitcast_convert_type(a.astype(BF16).astype(F32), jnp.uint32)
    ub = lax.bitcast_convert_type(b.astype(BF16).astype(F32), jnp.uint32)
    return (ub & jnp.uint32(0xFFFF0000)) | (ua >> 16)


def _unpack_bf16_pairs(p):
    lo = lax.bitcast_convert_type(p << 16, F32)
    hi = lax.bitcast_convert_type(p & jnp.uint32(0xFFFF0000), F32)
    return jnp.concatenate([lo, hi], axis=1)


def _split_dot(x, m_bf16):
    hi = x.astype(BF16)
    lo = (x - hi.astype(F32)).astype(BF16)
    return _dot(hi, m_bf16) + _dot(lo, m_bf16)


def _norm_proj_kernel(x_ref, nw_ref, w_ref, *o_refs, splits, chunk):
    xn = _rms(x_ref[...], nw_ref[...]).astype(BF16)
    col = 0
    for o_ref, width in zip(o_refs, splits):
        for c0 in range(0, width, chunk):
            r = _dot(xn, w_ref[:, col + c0:col + c0 + chunk])
            o_ref[:, c0:c0 + chunk] = r.astype(o_ref.dtype)
        col += width


def norm_proj(x, nw, w, layer, splits, dtypes, tm=1024, chunk=256):
    rows, d = x.shape
    n = w.shape[2]
    assert sum(splits) == n and rows % tm == 0
    return pl.pallas_call(
        functools.partial(_norm_proj_kernel, splits=splits, chunk=chunk),
        grid=(rows // tm,),
        in_specs=[
            pl.BlockSpec((tm, d), lambda i: (i, 0)),
            pl.BlockSpec((1, d), lambda i: (0, 0)),
            pl.BlockSpec((None, d, n), lambda i: (layer, 0, 0)),
        ],
        out_specs=[pl.BlockSpec((tm, s), lambda i: (i, 0)) for s in splits],
        out_shape=[jax.ShapeDtypeStruct((rows, s), dt) for s, dt in zip(splits, dtypes)],
        compiler_params=_cparams(("parallel",)),
        name="norm_proj",
    )(x, nw.reshape(1, d), w)


def _s5_prep_kernel(lr_ref, li_ref, ldt_ref, br_ref, bi_ref, apow_ref, bb_ref):
    lr = lr_ref[0]
    li = li_ref[0]
    dt = jnp.exp(ldt_ref[0])
    mag = jnp.exp(lr * dt)
    a_re = mag * jnp.cos(li * dt)
    a_im = mag * jnp.sin(li * dt)
    denom = lr * lr + li * li
    nr = a_re - 1.0
    z_re = (nr * lr + a_im * li) / denom
    z_im = (a_im * lr - nr * li) / denom
    br = br_ref[0]
    bi = bi_ref[0]
    nh = br.shape[0]
    bb_ref[0, :nh, :] = z_re * br - z_im * bi
    bb_ref[0, nh:, :] = z_re * bi + z_im * br
    nstep = (apow_ref.shape[1] - 2 * S5_SEG_LEVELS) // 2
    pr, pi = a_re, a_im
    for j in range(nstep):
        apow_ref[0, j:j + 1, :] = pr
        apow_ref[0, nstep + j:nstep + j + 1, :] = pi
        if j + 1 < nstep:
            pr, pi = pr * a_re - pi * a_im, pr * a_im + pi * a_re
    for k in range(S5_SEG_LEVELS):
        apow_ref[0, 2 * nstep + k:2 * nstep + k + 1, :] = pr
        apow_ref[0, 2 * nstep + S5_SEG_LEVELS + k:2 * nstep + S5_SEG_LEVELS + k + 1, :] = pi
        pr, pi = pr * pr - pi * pi, 2.0 * pr * pi


S5_SEG_LEVELS = 3
S5_TIME_TILE = 512


def s5_prep(lam_re, lam_im, log_dt, b_re, b_im, nstep):
    depth, g, p = lam_re.shape
    nh = b_re.shape[-1]
    gp = g * p
    lr = lam_re.reshape(depth, 1, gp)
    li = lam_im.reshape(depth, 1, gp)
    ldt = jnp.repeat(log_dt, p, axis=1).reshape(depth, 1, gp)
    br = b_re.transpose(0, 3, 1, 2).reshape(depth, nh, gp)
    bi = b_im.transpose(0, 3, 1, 2).reshape(depth, nh, gp)
    vec = pl.BlockSpec((1, 1, gp), lambda i: (i, 0, 0))
    mat = pl.BlockSpec((1, nh, gp), lambda i: (i, 0, 0))
    npow = 2 * (nstep + S5_SEG_LEVELS)
    return pl.pallas_call(
        _s5_prep_kernel,
        grid=(depth,),
        in_specs=[vec, vec, vec, mat, mat],
        out_specs=[pl.BlockSpec((1, npow, gp), lambda i: (i, 0, 0)),
                   pl.BlockSpec((1, 2 * nh, gp), lambda i: (i, 0, 0))],
        out_shape=[jax.ShapeDtypeStruct((depth, npow, gp), F32),
                   jax.ShapeDtypeStruct((depth, 2 * nh, gp), F32)],
        compiler_params=_cparams(("arbitrary",)),
        name="s5_prep",
    )(lr, li, ldt, br, bi)


def _s5_kernel(u_ref, bb_ref, cc_ref, apow_ref, d_ref, wglu_ref, o_ref, xs_ref, carry_ref, perm_ref,
               *, gp):
    ts = u_ref.shape[0]
    nstep = ts // SUBLANES
    rows = lambda s: slice(s * SUBLANES, (s + 1) * SUBLANES)

    @pl.when(pl.program_id(1) == 0)
    def _():
        carry_ref[...] = jnp.zeros_like(carry_ref)

    ntile = u_ref.shape[1] // LANES
    for c in range(ntile):
        perm_ref[c] = u_ref[:, c * LANES:(c + 1) * LANES]
    u = jnp.concatenate(
        [jnp.concatenate([perm_ref[c, pl.ds(s, SUBLANES, stride=nstep), :] for c in range(ntile)], axis=1)
         for s in range(nstep)], axis=0)
    xs_ref[...] = _dot(u.astype(BF16), bb_ref[...])

    ar = apow_ref[0:1, :]
    ai = apow_ref[nstep:nstep + 1, :]
    xr = xs_ref[rows(0), :gp]
    xi = xs_ref[rows(0), gp:]
    for s in range(1, nstep):
        xr, xi = (ar * xr - ai * xi + xs_ref[rows(s), :gp], ar * xi + ai * xr + xs_ref[rows(s), gp:])
        xs_ref[rows(s), :gp] = xr
        xs_ref[rows(s), gp:] = xi

    seg = lax.broadcasted_iota(jnp.int32, (SUBLANES, 1), 0)
    cr = carry_ref[0:1, :]
    ci = carry_ref[1:2, :]
    lr = apow_ref[2 * nstep:2 * nstep + 1, :]
    li = apow_ref[2 * nstep + S5_SEG_LEVELS:2 * nstep + S5_SEG_LEVELS + 1, :]
    first = seg == 0
    xr = xr + jnp.where(first, lr * cr - li * ci, 0.0)
    xi = xi + jnp.where(first, lr * ci + li * cr, 0.0)
    for k in range(S5_SEG_LEVELS):
        sh = 1 << k
        pr = apow_ref[2 * nstep + k:2 * nstep + k + 1, :]
        pi = apow_ref[2 * nstep + S5_SEG_LEVELS + k:2 * nstep + S5_SEG_LEVELS + k + 1, :]
        keep = seg >= sh
        sr = jnp.where(keep, pltpu.roll(xr, sh, axis=0), 0.0)
        si = jnp.where(keep, pltpu.roll(xi, sh, axis=0), 0.0)
        xr, xi = xr + pr * sr - pi * si, xi + pr * si + pi * sr
    carry_ref[0:1, :] = xr[SUBLANES - 1:, :]
    carry_ref[1:2, :] = xi[SUBLANES - 1:, :]
    enter_r = jnp.where(first, cr, pltpu.roll(xr, 1, axis=0))
    enter_i = jnp.where(first, ci, pltpu.roll(xi, 1, axis=0))

    for s in range(nstep):
        pr = apow_ref[s:s + 1, :]
        pi = apow_ref[nstep + s:nstep + s + 1, :]
        xs_ref[rows(s), :gp] += pr * enter_r - pi * enter_i
        xs_ref[rows(s), gp:] += pr * enter_i + pi * enter_r

    y = _dot(xs_ref[...].astype(BF16), cc_ref[...]) + d_ref[...] * u
    g = jax.nn.gelu(y)
    out = g * jax.nn.sigmoid(_dot(g.astype(BF16), wglu_ref[...]))
    for s in range(nstep):
        for c in range(ntile):
            perm_ref[c, pl.ds(s, SUBLANES, stride=nstep), :] = out[rows(s), c * LANES:(c + 1) * LANES]
    for c in range(ntile):
        o_ref[:, c * LANES:(c + 1) * LANES] = perm_ref[c]


def s5_mixer(p_a, bb, cc, apow, d_skip, w_glu, batch, seq):
    w = d_skip.shape[-1]
    gp = apow.shape[-1]
    ts = (apow.shape[0] // 2 - S5_SEG_LEVELS) * SUBLANES
    nt = seq // ts
    return pl.pallas_call(
        functools.partial(_s5_kernel, gp=gp),
        grid=(batch, nt),
        in_specs=[
            pl.BlockSpec((ts, w), lambda b, t: (b * nt + t, 0)),
            pl.BlockSpec((w, 2 * gp), lambda b, t: (0, 0)),
            pl.BlockSpec((2 * gp, w), lambda b, t: (0, 0)),
            pl.BlockSpec(apow.shape, lambda b, t: (0, 0)),
            pl.BlockSpec((1, w), lambda b, t: (0, 0)),
            pl.BlockSpec((w, w), lambda b, t: (0, 0)),
        ],
        out_specs=pl.BlockSpec((ts, w), lambda b, t: (b * nt + t, 0)),
        out_shape=jax.ShapeDtypeStruct((batch * seq, w), F32),
        scratch_shapes=[pltpu.VMEM((ts, 2 * gp), F32), pltpu.VMEM((2, gp), F32),
                        pltpu.VMEM((w // LANES, ts, LANES), F32)],
        compiler_params=_cparams(("parallel", "arbitrary")),
        name="s5_mixer",
    )(p_a, bb, cc, apow, d_skip.reshape(1, w), w_glu)


def _block_diag_in(bb, g, p, nh):
    eye = jnp.eye(g, dtype=F32)
    def one(m):
        m = m.reshape(nh, g, p)
        return jnp.einsum('hgp,kg->khgp', m, eye).reshape(g * nh, g * p)
    return jnp.concatenate([one(bb[:nh]), one(bb[nh:])], axis=1)


def _block_diag_out(c_re, c_im):
    g, nh, p = c_re.shape
    eye = jnp.eye(g, dtype=F32)
    def one(m):
        return jnp.einsum('ghp,gk->gpkh', m, eye).reshape(g * p, g * nh)
    return jnp.concatenate([one(c_re), one(-c_im)], axis=0)


def _ret_tables(seq, chunk, heads, dh):
    half = dh // 2
    w = heads * dh
    hw = w // 2
    inv_freq = ROPE_BASE ** (-np.arange(half, dtype=np.float64) / half)
    ang = np.arange(seq, dtype=np.float64)[:, None] * np.tile(inv_freq, heads)[None, :]
    cos = np.cos(ang).astype(np.float32)
    sin = np.sin(ang).astype(np.float32)
    log_gamma = np.log1p(-(2.0 ** (-5.0 - np.arange(heads, dtype=np.float64))))
    idx = np.arange(chunk, dtype=np.float64)
    rel = idx[:, None] - idx[None, :]
    intra = np.where(rel >= 0, np.exp(log_gamma[:, None, None] * np.maximum(rel, 0.0)), 0.0)
    head_nat = np.arange(w) // dh
    head_rot = (np.arange(w) % hw) // half
    xi = np.exp(log_gamma[head_nat][None, :] * (idx[:, None] + 1.0))
    zeta = np.exp(log_gamma[head_rot][None, :] * (chunk - 1.0 - idx[:, None]))
    decay = np.exp(log_gamma[head_nat] * chunk)[None, :]
    same = (head_rot[:, None] == head_nat[None, :]).astype(np.float32)
    gavg = (head_nat[:, None] == head_nat[None, :]).astype(np.float32) / dh
    f = lambda a: jnp.asarray(a, dtype=F32)
    return dict(cos=f(cos), sin=f(sin), intra=f(intra), xi=f(xi), zeta=f(zeta), decay=f(decay),
                same=f(same), gavg=jnp.asarray(gavg, dtype=BF16))


def _ret_kernel(q_ref, k_ref, v_ref, g_ref, cos_ref, sin_ref, intra_ref, xi_ref, zeta_ref,
                decay_ref, same_ref, gavg_ref, gnw_ref, o_ref, state_ref, *, heads, dh):
    w = heads * dh
    hw = w // 2
    half = dh // 2

    @pl.when(pl.program_id(1) == 0)
    def _():
        state_ref[...] = jnp.zeros_like(state_ref)

    cos = cos_ref[...]
    sin = sin_ref[...]

    def rot(x):
        x1, x2 = x[:, :hw], x[:, hw:]
        return jnp.concatenate([x1 * cos - x2 * sin, x1 * sin + x2 * cos], axis=1)

    lane = lax.broadcasted_iota(jnp.int32, (1, w), 1)
    head_rot = (lane % hw) // half
    head_nat = lane // dh
    gavg = gavg_ref[...]

    for b in range(q_ref.shape[0]):
        q = rot(q_ref[b])
        k = rot(k_ref[b]) * (dh ** -0.5)
        qb = q.astype(BF16)
        kb = k.astype(BF16)
        vb = v_ref[b].astype(BF16)

        o = _dot(qb, state_ref[b].astype(BF16)) * xi_ref[...]
        for h in range(heads):
            qh = jnp.where(head_rot == h, qb, jnp.zeros_like(qb))
            vh = jnp.where(head_nat == h, vb, jnp.zeros_like(vb))
            s = _dot_nt(qh, kb) * intra_ref[h]
            o = o + _dot(s.astype(BF16), vh)
        kz = (k * zeta_ref[...]).astype(BF16)
        state_ref[b] = decay_ref[...] * state_ref[b] + _dot_tn(kz, vb) * same_ref[...]

        mu = _split_dot(o, gavg)
        d = o - mu
        var = _split_dot(d * d, gavg)
        on = d * lax.rsqrt(var + GN_EPS) * gnw_ref[...]
        o_ref[b] = jax.nn.silu(g_ref[b]) * on


def retention_mixer(p_a, col0, gn_w, batch, seq, heads=RET_HEADS, chunk=256, nb=2):
    w = gn_w.shape[-1]
    dh = w // heads
    nc = seq // chunk
    tb = _ret_tables(seq, chunk, heads, dh)
    cb = col0 // w
    assert col0 % w == 0 and batch % nb == 0
    p3 = p_a.reshape(batch, seq, p_a.shape[1])
    row = lambda j: pl.BlockSpec((nb, chunk, w), lambda b, c, j=j: (b, c, cb + j))
    const = lambda shape: pl.BlockSpec(shape, lambda b, c: (0,) * len(shape))
    return pl.pallas_call(
        functools.partial(_ret_kernel, heads=heads, dh=dh),
        grid=(batch // nb, nc),
        in_specs=[row(0), row(1), row(2), row(3),
                  pl.BlockSpec((chunk, w // 2), lambda b, c: (c, 0)),
                  pl.BlockSpec((chunk, w // 2), lambda b, c: (c, 0)),
                  const((heads, chunk, chunk)), const((chunk, w)), const((chunk, w)),
                  const((1, w)), const((w, w)), const((w, w)), const((1, w))],
        out_specs=pl.BlockSpec((nb, chunk, w), lambda b, c: (b, c, 0)),
        out_shape=jax.ShapeDtypeStruct((batch, seq, w), F32),
        scratch_shapes=[pltpu.VMEM((nb, w, w), F32)],
        compiler_params=_cparams(("parallel", "arbitrary")),
        name="retention_mixer",
    )(p3, p3, p3, p3, tb["cos"], tb["sin"], tb["intra"], tb["xi"], tb["zeta"], tb["decay"],
      tb["same"], tb["gavg"], gn_w.reshape(1, w)).reshape(batch * seq, w)


def _group_rotary_halves(w, heads):
    lead = w.shape[:-1]
    dh = w.shape[-1] // heads
    return w.reshape(lead + (heads, 2, dh // 2)).swapaxes(-3, -2).reshape(lead + (heads * dh,))


def _sb_kernel(q_ref, k_ref, v_ref, o_ref, *, dh, blk):
    lane = lax.broadcasted_iota(jnp.int32, (1, 2 * dh), 1)
    r_id = lax.broadcasted_iota(jnp.int32, (blk, blk), 0)
    c_id = lax.broadcasted_iota(jnp.int32, (blk, blk), 1)
    neg_upper = jnp.where(r_id > c_id, -1.0, 0.0).astype(BF16)
    causal = c_id < r_id
    zscale = (dh ** -0.5) * LOG2_E
    hsel = [(lane // dh) == hh for hh in range(2)]

    def pair(qm, j, runs, diag):
        off = j * blk if isinstance(j, int) else pl.multiple_of(j * blk, blk)
        kb = k_ref[pl.ds(off, blk), :]
        vb = v_ref[pl.ds(off, blk), :]
        wgts, vms, new_runs = [], [], []
        for hh in range(2):
            zs = _dot_nt(qm[hh], kb) * zscale
            nfail = jnp.maximum(zs, 0.0) + jnp.log2(1.0 + jnp.exp2(jnp.minimum(zs, -zs)))
            if diag:
                nfail = jnp.where(causal, nfail, 0.0)
            after = _dot(nfail.astype(BF16), neg_upper) + runs[hh]
            wgt = jnp.exp2((zs - nfail) + after)
            if diag:
                wgt = jnp.where(causal, wgt, 0.0)
            wgts.append(wgt.astype(BF16))
            vms.append(jnp.where(hsel[hh], vb, jnp.zeros_like(vb)))
            new_runs.append(runs[hh] - jnp.sum(nfail, axis=1, keepdims=True))
        contrib = _dot(jnp.concatenate(wgts, axis=1), jnp.concatenate(vms, axis=0))
        return contrib, new_runs

    zero = jnp.zeros((blk, 1), F32)
    started = []
    for i in range(q_ref.shape[0] // blk):
        q2 = q_ref[i * blk:(i + 1) * blk, :]
        qm = [jnp.where(hs, q2, jnp.zeros_like(q2)) for hs in hsel]
        acc, runs = pair(qm, i, [zero, zero], True)
        if i >= 1:
            acc1, runs = pair(qm, i - 1, runs, False)
            acc = acc + acc1
        started.append((qm, acc, runs))

    for i, (qm, acc, (ra, rb)) in enumerate(started):
        if i >= 2:
            def cond(carry, i=i):
                jj, _, ra, rb = carry
                alive = jnp.maximum(jnp.max(ra), jnp.max(rb)) > SB_DEAD_LOG2
                return jnp.logical_and(jj <= i, alive)

            def body(carry, i=i, qm=qm):
                jj, acc, ra, rb = carry
                c, (ra, rb) = pair(qm, i - jj, [ra, rb], False)
                return jj + 1, acc + c, ra, rb

            _, acc, _, _ = lax.while_loop(cond, body, (jnp.int32(2), acc, ra, rb))
        o_ref[i * blk:(i + 1) * blk, :] = acc


def stick_breaking_mixer(p_sb, batch, seq, heads=SB_HEADS, blk=256):
    w = p_sb.shape[1] // 3
    dh = w // heads
    pw = 2 * dh
    assert pw == LANES and seq % blk == 0
    npair = heads // 2
    return pl.pallas_call(
        functools.partial(_sb_kernel, dh=dh, blk=blk),
        grid=(batch, npair),
        in_specs=[
            pl.BlockSpec((seq, pw), lambda b, p: (b, p)),
            pl.BlockSpec((seq, pw), lambda b, p: (b, npair + p)),
            pl.BlockSpec((seq, pw), lambda b, p: (b, 2 * npair + p)),
        ],
        out_specs=pl.BlockSpec((seq, pw), lambda b, p: (b, p)),
        out_shape=jax.ShapeDtypeStruct((batch * seq, w), F32),
        compiler_params=_cparams(("parallel", "parallel")),
        name="stick_breaking_mixer",
    )(p_sb, p_sb, p_sb)


def _mix_cross_kernel(h_ref, ya_ref, yb_ref, yc_ref, gain_ref, wout_ref, cnw_ref, wq_ref, kv_ref,
                      wo_ref, o_ref, *, heads):
    wa = ya_ref.shape[1]
    wb = yb_ref.shape[1]
    d = h_ref.shape[1]
    gain = gain_ref[...]
    ya = _rms(ya_ref[...], gain[:, :wa]).astype(BF16)
    yb = yb_ref[...].astype(BF16)
    yc = _rms(yc_ref[...], gain[:, wa + wb:]).astype(BF16)
    h1 = (h_ref[...] + _dot(ya, wout_ref[:wa, :]) + _dot(yb, wout_ref[wa:wa + wb, :])
          + _dot(yc, wout_ref[wa + wb:, :]))
    q = _dot(_rms(h1, cnw_ref[...]).astype(BF16), wq_ref[...])
    dh = d // heads
    outs = []
    for hd in range(heads):
        qh = q[:, hd * dh:(hd + 1) * dh].astype(BF16)
        kh = kv_ref[:, hd * dh:(hd + 1) * dh]
        vh = kv_ref[:, d + hd * dh:d + (hd + 1) * dh]
        s = _dot_nt(qh, kh) * (dh ** -0.5)
        s = s - jnp.max(s, axis=-1, keepdims=True)
        e = jnp.exp(s)
        p = e / jnp.sum(e, axis=-1, keepdims=True)
        outs.append(_dot(p.astype(BF16), vh).astype(BF16))
    o_ref[...] = h1 + _dot(jnp.concatenate(outs, axis=1), wo_ref[...])


def mix_cross(h, ya, yb, yc, gain, w_out, cnw, wq, kv, wo, layer, batch, seq, heads=CROSS_HEADS, tm=1024):
    t, d = h.shape
    m = kv.shape[0] // batch
    nt = seq // tm
    rows = lambda wd: pl.BlockSpec((tm, wd), lambda i: (i, 0))
    const = lambda shape: pl.BlockSpec(shape, lambda i: (0, 0))
    stack = pl.BlockSpec((None, d, d), lambda i: (layer, 0, 0))
    return pl.pallas_call(
        functools.partial(_mix_cross_kernel, heads=heads),
        grid=(t // tm,),
        in_specs=[rows(d), rows(ya.shape[1]), rows(yb.shape[1]), rows(yc.shape[1]),
                  const((1, d)), stack, const((1, d)), stack,
                  pl.BlockSpec((m, 2 * d), lambda i: (i // nt, 0)),
                  stack],
        out_specs=rows(d),
        out_shape=jax.ShapeDtypeStruct((t, d), F32),
        compiler_params=_cparams(("parallel",)),
        name="mix_cross",
    )(h, ya, yb, yc, gain.reshape(1, d), w_out, cnw.reshape(1, d), wq, kv, wo)


FFN_TM = 1024
FFN_TF = 1408
FFN_CHUNK = 256
FFN_VMEM_MB = 60


def _swiglu_accumulate(x, wg_ref, wu_ref, wd_ref, o_ref):
    tf = wd_ref.shape[0]
    for c0 in range(0, tf, FFN_CHUNK):
        c1 = min(c0 + FFN_CHUNK, tf)
        wg = wg_ref[:, c0:c1].astype(BF16)
        wu = wu_ref[:, c0:c1].astype(BF16)
        act = jax.nn.silu(_dot(x, wg)) * _dot(x, wu)
        o_ref[...] += _dot(act.astype(BF16), wd_ref[c0:c1, :].astype(BF16))


def _ffn_dense_kernel(h_ref, nw_ref, wg_ref, wu_ref, wd_ref, o_ref, xn_ref):
    f = pl.program_id(1)

    @pl.when(f == 0)
    def _():
        h = h_ref[...]
        xn_ref[...] = _rms(h, nw_ref[...]).astype(BF16)
        o_ref[...] = h

    _swiglu_accumulate(xn_ref[...], wg_ref, wu_ref, wd_ref, o_ref)


def ffn_dense(h, nw, w_gu, w_down, layer, tm=FFN_TM, tf=FFN_TF):
    t, d = h.shape
    ff = w_down.shape[1]
    nf = ff // tf
    return pl.pallas_call(
        _ffn_dense_kernel,
        grid=(t // tm, nf),
        in_specs=[
            pl.BlockSpec((tm, d), lambda i, f: (i, 0)),
            pl.BlockSpec((1, d), lambda i, f: (0, 0)),
            pl.BlockSpec((None, d, tf), lambda i, f: (layer, 0, f)),
            pl.BlockSpec((None, d, tf), lambda i, f: (layer, 0, nf + f)),
            pl.BlockSpec((None, tf, d), lambda i, f: (layer, f, 0)),
        ],
        out_specs=pl.BlockSpec((tm, d), lambda i, f: (i, 0)),
        out_shape=jax.ShapeDtypeStruct((t, d), F32),
        scratch_shapes=[pltpu.VMEM((tm, d), BF16)],
        compiler_params=_cparams(("parallel", "arbitrary"), FFN_VMEM_MB),
        name="ffn_dense",
    )(h, nw.reshape(1, d), w_gu, w_gu, w_down)


def _ffn_group_kernel(te_ref, nv_ref, rows_ref, x_ref, wg_ref, wu_ref, wd_ref, y_ref, xb_ref, acc_ref):
    i = pl.program_id(0)
    f = pl.program_id(1)
    valid = i < nv_ref[0]

    @pl.when(f == 0)
    def _():
        row = lax.broadcasted_iota(jnp.int32, (x_ref.shape[0], 1), 0)
        xb_ref[...] = jnp.where(row < rows_ref[i], _unpack_bf16_pairs(x_ref[...]), 0.0).astype(BF16)
        acc_ref[...] = jnp.zeros_like(acc_ref)

    @pl.when(valid)
    def _():
        _swiglu_accumulate(xb_ref[...], wg_ref.at[0], wu_ref.at[0], wd_ref.at[0], acc_ref)

    @pl.when(f == pl.num_programs(1) - 1)
    def _():
        half = acc_ref.shape[1] // 2
        y_ref[...] = _pack_bf16_pairs(acc_ref[:, :half], acc_ref[:, half:])


def ffn_grouped(xs, tile_expert, n_valid, tile_rows, w_gu, w_down, expert_base, tm=FFN_TM, tf=FFN_TF):
    p = xs.shape[0]
    d = 2 * xs.shape[1]
    ff = w_down.shape[1]
    nf = ff // tf

    def live(i, f, nv):
        ok = i < nv[0]
        return jnp.where(ok, i, nv[0] - 1), jnp.where(ok, f, nf - 1)

    def x_map(i, f, te, nv, rows):
        ii, _ = live(i, f, nv)
        return ii, 0

    def wg_map(i, f, te, nv, rows):
        ii, ff_ = live(i, f, nv)
        return expert_base + te[ii], 0, ff_

    def wu_map(i, f, te, nv, rows):
        ii, ff_ = live(i, f, nv)
        return expert_base + te[ii], 0, nf + ff_

    def wd_map(i, f, te, nv, rows):
        ii, ff_ = live(i, f, nv)
        return expert_base + te[ii], ff_, 0

    grid_spec = pltpu.PrefetchScalarGridSpec(
        num_scalar_prefetch=3,
        grid=(p // tm, nf),
        in_specs=[
            pl.BlockSpec((tm, d // 2), x_map),
            pl.BlockSpec((1, d, tf), wg_map),
            pl.BlockSpec((1, d, tf), wu_map),
            pl.BlockSpec((1, tf, d), wd_map),
        ],
        out_specs=pl.BlockSpec((tm, d // 2), lambda i, f, te, nv, rows: (i, 0)),
        scratch_shapes=[pltpu.VMEM((tm, d), BF16), pltpu.VMEM((tm, d), F32)],
    )
    return pl.pallas_call(
        _ffn_group_kernel,
        grid_spec=grid_spec,
        out_shape=jax.ShapeDtypeStruct((p, d // 2), jnp.uint32),
        compiler_params=_cparams(("arbitrary", "arbitrary"), FFN_VMEM_MB),
        name="ffn_grouped",
    )(tile_expert, n_valid, tile_rows, xs, w_gu, w_gu, w_down)


V7X_SC_CORES = 2
V7X_SC_SUBCORES = 16
V7X_SC_WORKERS = V7X_SC_CORES * V7X_SC_SUBCORES


def _sc_mesh():
    return plsc.VectorSubcoreMesh(core_axis_name="c", subcore_axis_name="s")


def sc_scatter_rows2(x, idx1, idx2, n_out, ch=128):
    t, d = x.shape
    assert t % (V7X_SC_WORKERS * ch) == 0
    t_per_w = t // V7X_SC_WORKERS
    n_ch = t_per_w // ch

    @functools.partial(
        pl.kernel, mesh=_sc_mesh(), out_type=jax.ShapeDtypeStruct((n_out, d), x.dtype),
        scratch_types=[pltpu.VMEM((ch,), jnp.int32), pltpu.VMEM((ch,), jnp.int32),
                       pltpu.VMEM((ch, d), x.dtype), pltpu.SemaphoreType.DMA])
    def k(x_hbm, i1_hbm, i2_hbm, out_hbm, i1_v, i2_v, rows_v, sem):
        wid = lax.axis_index("s") * V7X_SC_CORES + lax.axis_index("c")
        base = wid * t_per_w

        @pl.loop(0, n_ch)
        def _(j):
            off = pl.multiple_of(base + j * ch, 8)
            pltpu.sync_copy(i1_hbm.at[pl.ds(off, ch)], i1_v)
            pltpu.sync_copy(i2_hbm.at[pl.ds(off, ch)], i2_v)
            pltpu.sync_copy(x_hbm.at[pl.ds(off, ch)], rows_v)
            pltpu.async_copy(rows_v, out_hbm.at[i1_v], sem).wait()
            pltpu.async_copy(rows_v, out_hbm.at[i2_v], sem).wait()

    return k(x, idx1, idx2)


def sc_gather_rows(table, idx, ch=128):
    _, d = table.shape
    b = idx.shape[0]
    assert b % (V7X_SC_WORKERS * ch) == 0
    b_per_w = b // V7X_SC_WORKERS
    n_ch = b_per_w // ch

    @functools.partial(
        pl.kernel, mesh=_sc_mesh(), out_type=jax.ShapeDtypeStruct((b, d), table.dtype),
        scratch_types=[pltpu.VMEM((ch,), jnp.int32), pltpu.VMEM((ch, d), table.dtype),
                       pltpu.SemaphoreType.DMA])
    def k(table_hbm, idx_hbm, out_hbm, idx_v, rows_v, sem):
        wid = lax.axis_index("s") * V7X_SC_CORES + lax.axis_index("c")
        base = wid * b_per_w

        @pl.loop(0, n_ch)
        def _(j):
            off = pl.multiple_of(base + j * ch, 8)
            pltpu.sync_copy(idx_hbm.at[pl.ds(off, ch)], idx_v)
            pltpu.async_copy(table_hbm.at[idx_v], rows_v, sem).wait()
            pltpu.sync_copy(rows_v, out_hbm.at[pl.ds(off, ch)])

    return k(table, idx)


def _moe_combine_kernel(h_ref, y1_ref, y2_ref, g1_ref, g2_ref, nw_ref, o_ref, *, normed):
    out = (h_ref[...] + g1_ref[...] * _unpack_bf16_pairs(y1_ref[...])
           + g2_ref[...] * _unpack_bf16_pairs(y2_ref[...]))
    o_ref[...] = _rms(out, nw_ref[...]) if normed else out


def moe_combine(h, yg, gate1, gate2, norm_w=None, tm=1024):
    t, d = h.shape
    nt = t // tm
    row = pl.BlockSpec((tm, d), lambda i: (i, 0))
    col = pl.BlockSpec((tm, 1), lambda i: (i, 0))
    nw = jnp.ones((d,), F32) if norm_w is None else norm_w
    return pl.pallas_call(
        functools.partial(_moe_combine_kernel, normed=norm_w is not None),
        grid=(nt,),
        in_specs=[row, pl.BlockSpec((tm, d // 2), lambda i: (i, 0)),
                  pl.BlockSpec((tm, d // 2), lambda i: (nt + i, 0)), col, col,
                  pl.BlockSpec((1, d), lambda i: (0, 0))],
        out_specs=row,
        out_shape=jax.ShapeDtypeStruct((t, d), F32),
        compiler_params=_cparams(("parallel",)),
        name="moe_combine",
    )(h, yg, yg, gate1, gate2, nw.reshape(1, d))


def _route(h, nw, wr, n_exp):
    xn = _rms(h, nw)
    half = xn.shape[1] // 2
    packed = _pack_bf16_pairs(xn[:, :half], xn[:, half:])
    xh = xn.astype(BF16)
    xl = (xn - xh.astype(F32)).astype(BF16)
    wh = wr.astype(BF16)
    wl = (wr - wh.astype(F32)).astype(BF16)
    logits = _dot(xh, wh) + _dot(xh, wl) + _dot(xl, wh)
    lane = lax.broadcasted_iota(jnp.int32, logits.shape, 1)
    neg = -jnp.inf
    lg = jnp.where(lane < n_exp, logits, neg)
    m1 = jnp.max(lg, axis=-1, keepdims=True)
    i1 = jnp.min(jnp.where(lg == m1, lane, LANES), axis=-1, keepdims=True)
    lg2 = jnp.where(lane == i1, neg, lg)
    m2 = jnp.max(lg2, axis=-1, keepdims=True)
    i2 = jnp.min(jnp.where(lg2 == m2, lane, LANES), axis=-1, keepdims=True)
    e2 = jnp.exp(m2 - m1)
    g1 = 1.0 / (1.0 + e2)
    g2 = e2 / (1.0 + e2)
    slot = jnp.where(lane == i1, 1, jnp.where(lane == i2, 2, 0))
    gate = jnp.where(lane == i1, g1, jnp.where(lane == i2, g2, 0.0))
    return packed, slot, gate


def _router_kernel(h_ref, nw_ref, wr_ref, tri_ref, xn_ref, slot_ref, gate_ref, rank_ref, cnt_ref, *, n_exp):
    @pl.when(pl.program_id(0) == 0)
    def _():
        cnt_ref[...] = jnp.zeros_like(cnt_ref)

    xn_ref[...], slot, gate = _route(h_ref[...], nw_ref[...], wr_ref[...], n_exp)
    slot_ref[...] = slot[:, :n_exp]
    gate_ref[...] = gate[:, :n_exp]
    chosen = jnp.where(slot > 0, 1.0, 0.0)
    before = _dot(tri_ref[...], chosen.astype(BF16)) + cnt_ref[...]
    rank_ref[...] = before.astype(jnp.int32)[:, :n_exp]
    cnt_ref[...] += jnp.sum(chosen, axis=0, keepdims=True)


def router(h, nw, w_router, tm=1024):
    t, d = h.shape
    n_exp = w_router.shape[1]
    wr = jnp.pad(w_router, ((0, 0), (0, LANES - n_exp)))
    tri = jnp.asarray(np.tril(np.ones((tm, tm), np.float32), -1), dtype=BF16)
    per_expert = pl.BlockSpec((tm, n_exp), lambda i: (i, 0))
    return pl.pallas_call(
        functools.partial(_router_kernel, n_exp=n_exp),
        grid=(t // tm,),
        in_specs=[pl.BlockSpec((tm, d), lambda i: (i, 0)),
                  pl.BlockSpec((1, d), lambda i: (0, 0)),
                  pl.BlockSpec((d, LANES), lambda i: (0, 0)),
                  pl.BlockSpec((tm, tm), lambda i: (0, 0))],
        out_specs=[pl.BlockSpec((tm, d // 2), lambda i: (i, 0)), per_expert, per_expert, per_expert,
                   pl.BlockSpec((1, LANES), lambda i: (0, 0))],
        out_shape=[jax.ShapeDtypeStruct((t, d // 2), jnp.uint32),
                   jax.ShapeDtypeStruct((t, n_exp), jnp.int32),
                   jax.ShapeDtypeStruct((t, n_exp), F32),
                   jax.ShapeDtypeStruct((t, n_exp), jnp.int32),
                   jax.ShapeDtypeStruct((1, LANES), F32)],
        compiler_params=_cparams(("arbitrary",)),
        name="router",
    )(h, nw.reshape(1, d), wr, tri)


def _dispatch_plan(slot, gate, rank, totals, tm):
    t, n_exp = slot.shape
    counts = totals[0, :n_exp].astype(jnp.int32)
    tiles = (counts + tm - 1) // tm
    tile_end = jnp.cumsum(tiles)
    start = (tile_end - tiles) * tm
    pos = start[None, :] + rank
    n_rows = 2 * t + n_exp * tm
    n_tiles = n_rows // tm
    pos1 = jnp.sum(jnp.where(slot == 1, pos, 0), axis=1)
    pos2 = jnp.sum(jnp.where(slot == 2, pos, 0), axis=1)
    gate1 = jnp.sum(jnp.where(slot == 1, gate, 0.0), axis=1, keepdims=True)
    gate2 = jnp.sum(jnp.where(slot == 2, gate, 0.0), axis=1, keepdims=True)
    tile_id = jnp.arange(n_tiles, dtype=jnp.int32)
    owner = tile_id[:, None] >= tile_end[None, :]
    tile_expert = jnp.minimum(jnp.sum(owner.astype(jnp.int32), axis=1), n_exp - 1)
    mine = tile_expert[:, None] == jnp.arange(n_exp, dtype=jnp.int32)[None, :]
    group_end = jnp.sum(jnp.where(mine, (start + counts)[None, :], 0), axis=1)
    n_valid = tile_end[-1:].astype(jnp.int32)
    tile_rows = jnp.where(tile_id < n_valid[0], jnp.clip(group_end - tile_id * tm, 0, tm), 0)
    return (n_rows, pos1, pos2, gate1, gate2, tile_expert.astype(jnp.int32), n_valid,
            tile_rows.astype(jnp.int32))


def _final_norm_kernel(h_ref, w_ref, o_ref):
    o_ref[...] = _rms(h_ref[...], w_ref[...])


def final_norm(h, w, tm=1024):
    t, d = h.shape
    return pl.pallas_call(
        _final_norm_kernel,
        grid=(t // tm,),
        in_specs=[pl.BlockSpec((tm, d), lambda i: (i, 0)), pl.BlockSpec((1, d), lambda i: (0, 0))],
        out_specs=pl.BlockSpec((tm, d), lambda i: (i, 0)),
        out_shape=jax.ShapeDtypeStruct((t, d), F32),
        compiler_params=_cparams(("parallel",)),
        name="final_norm",
    )(h, w.reshape(1, d))


def kernel(x, mem, mix_norm_w, w_in, ssm_lambda_re, ssm_lambda_im, ssm_b_re, ssm_b_im, ssm_c_re, ssm_c_im, ssm_d, ssm_log_dt, ssm_w_glu, mix_out_gain, w_out, cross_norm_w, mem_norm_w, w_cross_q, w_cross_kv, w_cross_o, ffn_norm_w, w_dense_gu, w_dense_down, w_router, w_expert_gu, w_expert_down, final_norm_w):
    batch, seq, d = x.shape
    depth = w_in.shape[0]
    mem_len = mem.shape[1]
    ssm_w = ssm_d.shape[1]
    g, p = ssm_lambda_re.shape[1:]
    nh = ssm_b_re.shape[-1]
    ret_w = ssm_w
    sb_w = d - ssm_w - ret_w
    ret_dh = ret_w // RET_HEADS
    moe_tm = FFN_TM

    w_in_b = jnp.concatenate(
        [w_in[..., :ssm_w],
         _group_rotary_halves(w_in[..., ssm_w:ssm_w + ret_w], RET_HEADS),
         _group_rotary_halves(w_in[..., ssm_w + ret_w:ssm_w + 2 * ret_w], RET_HEADS),
         w_in[..., ssm_w + 2 * ret_w:]], axis=-1).astype(BF16)
    w_out_b = w_out.astype(BF16)
    w_q_b = w_cross_q.astype(BF16)
    w_kv_b = w_cross_kv.astype(BF16)
    w_o_b = w_cross_o.astype(BF16)
    n_exp = w_expert_gu.shape[1]
    w_egu = w_expert_gu.reshape((-1,) + w_expert_gu.shape[2:])
    w_edn = w_expert_down.reshape((-1,) + w_expert_down.shape[2:])
    n_a = ssm_w + 4 * ret_w

    apow, bbc = s5_prep(ssm_lambda_re, ssm_lambda_im, ssm_log_dt, ssm_b_re, ssm_b_im,
                        nstep=S5_TIME_TILE // SUBLANES)

    mem2 = mem.reshape(batch * mem_len, d)
    h = x.reshape(batch * seq, d)
    for i in range(depth):
        p_a, p_sb = norm_proj(h, mix_norm_w[i], w_in_b, i, (n_a, 3 * sb_w), (F32, BF16))
        bb = _block_diag_in(bbc[i], g, p, nh).astype(BF16)
        cc = _block_diag_out(ssm_c_re[i], ssm_c_im[i]).astype(BF16)
        y_a = s5_mixer(p_a, bb, cc, apow[i], ssm_d[i], ssm_w_glu[i].astype(BF16), batch, seq)
        gain = mix_out_gain[i]
        y_b = retention_mixer(p_a, ssm_w, gain[ssm_w:ssm_w + ret_w], batch, seq)
        y_c = stick_breaking_mixer(p_sb, batch, seq)
        (kv,) = norm_proj(mem2, mem_norm_w, w_kv_b, i, (2 * d,), (BF16,))
        h = mix_cross(h, y_a, y_b, y_c, gain, w_out_b, cross_norm_w[i], w_q_b, kv, w_o_b, i, batch, seq)
        if i % 2 == 0:
            h = ffn_dense(h, ffn_norm_w[i], w_dense_gu, w_dense_down, i // 2)
        else:
            xn, slot, gate, rank, totals = router(h, ffn_norm_w[i], w_router[i // 2])
            n_rows, pos1, pos2, gate1, gate2, tile_expert, n_valid, tile_rows = _dispatch_plan(
                slot, gate, rank, totals, moe_tm)
            xs = sc_scatter_rows2(xn, pos1, pos2, n_rows)
            y = ffn_grouped(xs, tile_expert, n_valid, tile_rows, w_egu, w_edn, (i // 2) * n_exp,
                            tm=moe_tm)
            yg = sc_gather_rows(y, jnp.concatenate([pos1, pos2]))
            last = i == depth - 1
            h = moe_combine(h, yg, gate1, gate2, norm_w=final_norm_w if last else None)
    if depth % 2 == 1:
        h = final_norm(h, final_norm_w)
    return h.reshape(batch, seq, d)
```

```python
import functools

import numpy as np
import jax
import jax.numpy as jnp
from jax import lax
from jax.experimental import pallas as pl
from jax.experimental.pallas import tpu as pltpu
from jax.experimental.pallas import tpu_sc as plsc

F32 = jnp.float32
BF16 = jnp.bfloat16

NORM_EPS = 1e-6
GN_EPS = 1e-6
ROPE_BASE = 10000.0
LOG2_E = 1.4426950408889634
SB_DEAD_LOG2 = -150.0

RET_HEADS = 4
SB_HEADS = 8
CROSS_HEADS = 4

LANES = 128
SUBLANES = 8


def _cparams(sem, vmem_mb=48):
    return pltpu.CompilerParams(dimension_semantics=sem, vmem_limit_bytes=vmem_mb * 1024 * 1024)


def _rms(x, w):
    ms = jnp.mean(x * x, axis=-1, keepdims=True)
    return x * lax.rsqrt(ms + NORM_EPS) * w


def _dot(a, b):
    return jnp.dot(a, b, preferred_element_type=F32)


def _dot_nt(a, b):
    return lax.dot_general(a, b, (((1,), (1,)), ((), ())), preferred_element_type=F32)


def _dot_tn(a, b):
    return lax.dot_general(a, b, (((0,), (0,)), ((), ())), preferred_element_type=F32)


def _pack_bf16_pairs(a, b):
    ua = lax.bitcast_convert_type(a.astype(BF16).astype(F32), jnp.uint32)
    ub = lax.bitcast_convert_type(b.astype(BF16).astype(F32), jnp.uint32)
    return (ub & jnp.uint32(0xFFFF0000)) | (ua >> 16)


def _unpack_bf16_pairs(p):
    lo = lax.bitcast_convert_type(p << 16, F32)
    hi = lax.bitcast_convert_type(p & jnp.uint32(0xFFFF0000), F32)
    return jnp.concatenate([lo, hi], axis=1)


def _split_dot(x, m_bf16):
    hi = x.astype(BF16)
    lo = (x - hi.astype(F32)).astype(BF16)
    return _dot(hi, m_bf16) + _dot(lo, m_bf16)


def _norm_proj_kernel(x_ref, nw_ref, w_ref, *o_refs, splits, chunk):
    xn = _rms(x_ref[...], nw_ref[...]).astype(BF16)
    col = 0
    for o_ref, width in zip(o_refs, splits):
        for c0 in range(0, width, chunk):
            r = _dot(xn, w_ref[:, col + c0:col + c0 + chunk])
            o_ref[:, c0:c0 + chunk] = r.astype(o_ref.dtype)
        col += width


def norm_proj(x, nw, w, layer, splits, dtypes, tm=1024, chunk=256):
    rows, d = x.shape
    n = w.shape[2]
    assert sum(splits) == n and rows % tm == 0
    return pl.pallas_call(
        functools.partial(_norm_proj_kernel, splits=splits, chunk=chunk),
        grid=(rows // tm,),
        in_specs=[
            pl.BlockSpec((tm, d), lambda i: (i, 0)),
            pl.BlockSpec((1, d), lambda i: (0, 0)),
            pl.BlockSpec((None, d, n), lambda i: (layer, 0, 0)),
        ],
        out_specs=[pl.BlockSpec((tm, s), lambda i: (i, 0)) for s in splits],
        out_shape=[jax.ShapeDtypeStruct((rows, s), dt) for s, dt in zip(splits, dtypes)],
        compiler_params=_cparams(("parallel",)),
        name="norm_proj",
    )(x, nw.reshape(1, d), w)


def _s5_prep_kernel(lr_ref, li_ref, ldt_ref, br_ref, bi_ref, apow_ref, bb_ref):
    lr = lr_ref[0]
    li = li_ref[0]
    dt = jnp.exp(ldt_ref[0])
    mag = jnp.exp(lr * dt)
    a_re = mag * jnp.cos(li * dt)
    a_im = mag * jnp.sin(li * dt)
    denom = lr * lr + li * li
    nr = a_re - 1.0
    z_re = (nr * lr + a_im * li) / denom
    z_im = (a_im * lr - nr * li) / denom
    br = br_ref[0]
    bi = bi_ref[0]
    nh = br.shape[0]
    bb_ref[0, :nh, :] = z_re * br - z_im * bi
    bb_ref[0, nh:, :] = z_re * bi + z_im * br
    nstep = (apow_ref.shape[1] - 2 * S5_SEG_LEVELS) // 2
    pr, pi = a_re, a_im
    for j in range(nstep):
        apow_ref[0, j:j + 1, :] = pr
        apow_ref[0, nstep + j:nstep + j + 1, :] = pi
        if j + 1 < nstep:
            pr, pi = pr * a_re - pi * a_im, pr * a_im + pi * a_re
    for k in range(S5_SEG_LEVELS):
        apow_ref[0, 2 * nstep + k:2 * nstep + k + 1, :] = pr
        apow_ref[0, 2 * nstep + S5_SEG_LEVELS + k:2 * nstep + S5_SEG_LEVELS + k + 1, :] = pi
        pr, pi = pr * pr - pi * pi, 2.0 * pr * pi


S5_SEG_LEVELS = 3
S5_TIME_TILE = 512


def s5_prep(lam_re, lam_im, log_dt, b_re, b_im, nstep):
    depth, g, p = lam_re.shape
    nh = b_re.shape[-1]
    gp = g * p
    lr = lam_re.reshape(depth, 1, gp)
    li = lam_im.reshape(depth, 1, gp)
    ldt = jnp.repeat(log_dt, p, axis=1).reshape(depth, 1, gp)
    br = b_re.transpose(0, 3, 1, 2).reshape(depth, nh, gp)
    bi = b_im.transpose(0, 3, 1, 2).reshape(depth, nh, gp)
    vec = pl.BlockSpec((1, 1, gp), lambda i: (i, 0, 0))
    mat = pl.BlockSpec((1, nh, gp), lambda i: (i, 0, 0))
    npow = 2 * (nstep + S5_SEG_LEVELS)
    return pl.pallas_call(
        _s5_prep_kernel,
        grid=(depth,),
        in_specs=[vec, vec, vec, mat, mat],
        out_specs=[pl.BlockSpec((1, npow, gp), lambda i: (i, 0, 0)),
                   pl.BlockSpec((1, 2 * nh, gp), lambda i: (i, 0, 0))],
        out_shape=[jax.ShapeDtypeStruct((depth, npow, gp), F32),
                   jax.ShapeDtypeStruct((depth, 2 * nh, gp), F32)],
        compiler_params=_cparams(("arbitrary",)),
        name="s5_prep",
    )(lr, li, ldt, br, bi)


def _s5_kernel(u_ref, bb_ref, cc_ref, apow_ref, d_ref, wglu_ref, o_ref, xs_ref, carry_ref, perm_ref,
               *, gp):
    ts = u_ref.shape[0]
    nstep = ts // SUBLANES
    rows = lambda s: slice(s * SUBLANES, (s + 1) * SUBLANES)

    @pl.when(pl.program_id(1) == 0)
    def _():
        carry_ref[...] = jnp.zeros_like(carry_ref)

    ntile = u_ref.shape[1] // LANES
    for c in range(ntile):
        perm_ref[c] = u_ref[:, c * LANES:(c + 1) * LANES]
    u = jnp.concatenate(
        [jnp.concatenate([perm_ref[c, pl.ds(s, SUBLANES, stride=nstep), :] for c in range(ntile)], axis=1)
         for s in range(nstep)], axis=0)
    xs_ref[...] = _dot(u.astype(BF16), bb_ref[...])

    ar = apow_ref[0:1, :]
    ai = apow_ref[nstep:nstep + 1, :]
    xr = xs_ref[rows(0), :gp]
    xi = xs_ref[rows(0), gp:]
    for s in range(1, nstep):
        xr, xi = (ar * xr - ai * xi + xs_ref[rows(s), :gp], ar * xi + ai * xr + xs_ref[rows(s), gp:])
        xs_ref[rows(s), :gp] = xr
        xs_ref[rows(s), gp:] = xi

    seg = lax.broadcasted_iota(jnp.int32, (SUBLANES, 1), 0)
    cr = carry_ref[0:1, :]
    ci = carry_ref[1:2, :]
    lr = apow_ref[2 * nstep:2 * nstep + 1, :]
    li = apow_ref[2 * nstep + S5_SEG_LEVELS:2 * nstep + S5_SEG_LEVELS + 1, :]
    first = seg == 0
    xr = xr + jnp.where(first, lr * cr - li * ci, 0.0)
    xi = xi + jnp.where(first, lr * ci + li * cr, 0.0)
    for k in range(S5_SEG_LEVELS):
        sh = 1 << k
        pr = apow_ref[2 * nstep + k:2 * nstep + k + 1, :]
        pi = apow_ref[2 * nstep + S5_SEG_LEVELS + k:2 * nstep + S5_SEG_LEVELS + k + 1, :]
        keep = seg >= sh
        sr = jnp.where(keep, pltpu.roll(xr, sh, axis=0), 0.0)
        si = jnp.where(keep, pltpu.roll(xi, sh, axis=0), 0.0)
        xr, xi = xr + pr * sr - pi * si, xi + pr * si + pi * sr
    carry_ref[0:1, :] = xr[SUBLANES - 1:, :]
    carry_ref[1:2, :] = xi[SUBLANES - 1:, :]
    enter_r = jnp.where(first, cr, pltpu.roll(xr, 1, axis=0))
    enter_i = jnp.where(first, ci, pltpu.roll(xi, 1, axis=0))

    for s in range(nstep):
        pr = apow_ref[s:s + 1, :]
        pi = apow_ref[nstep + s:nstep + s + 1, :]
        xs_ref[rows(s), :gp] += pr * enter_r - pi * enter_i
        xs_ref[rows(s), gp:] += pr * enter_i + pi * enter_r

    y = _dot(xs_ref[...].astype(BF16), cc_ref[...]) + d_ref[...] * u
    g = jax.nn.gelu(y)
    out = g * jax.nn.sigmoid(_dot(g.astype(BF16), wglu_ref[...]))
    for s in range(nstep):
        for c in range(ntile):
            perm_ref[c, pl.ds(s, SUBLANES, stride=nstep), :] = out[rows(s), c * LANES:(c + 1) * LANES]
    for c in range(ntile):
        o_ref[:, c * LANES:(c + 1) * LANES] = perm_ref[c]


def s5_mixer(p_a, bb, cc, apow, d_skip, w_glu, batch, seq):
    w = d_skip.shape[-1]
    gp = apow.shape[-1]
    ts = (apow.shape[0] // 2 - S5_SEG_LEVELS) * SUBLANES
    nt = seq // ts
    return pl.pallas_call(
        functools.partial(_s5_kernel, gp=gp),
        grid=(batch, nt),
        in_specs=[
            pl.BlockSpec((ts, w), lambda b, t: (b * nt + t, 0)),
            pl.BlockSpec((w, 2 * gp), lambda b, t: (0, 0)),
            pl.BlockSpec((2 * gp, w), lambda b, t: (0, 0)),
            pl.BlockSpec(apow.shape, lambda b, t: (0, 0)),
            pl.BlockSpec((1, w), lambda b, t: (0, 0)),
            pl.BlockSpec((w, w), lambda b, t: (0, 0)),
        ],
        out_specs=pl.BlockSpec((ts, w), lambda b, t: (b * nt + t, 0)),
        out_shape=jax.ShapeDtypeStruct((batch * seq, w), F32),
        scratch_shapes=[pltpu.VMEM((ts, 2 * gp), F32), pltpu.VMEM((2, gp), F32),
                        pltpu.VMEM((w // LANES, ts, LANES), F32)],
        compiler_params=_cparams(("parallel", "arbitrary")),
        name="s5_mixer",
    )(p_a, bb, cc, apow, d_skip.reshape(1, w), w_glu)


def _block_diag_in(bb, g, p, nh):
    eye = jnp.eye(g, dtype=F32)
    def one(m):
        m = m.reshape(nh, g, p)
        return jnp.einsum('hgp,kg->khgp', m, eye).reshape(g * nh, g * p)
    return jnp.concatenate([one(bb[:nh]), one(bb[nh:])], axis=1)


def _block_diag_out(c_re, c_im):
    g, nh, p = c_re.shape
    eye = jnp.eye(g, dtype=F32)
    def one(m):
        return jnp.einsum('ghp,gk->gpkh', m, eye).reshape(g * p, g * nh)
    return jnp.concatenate([one(c_re), one(-c_im)], axis=0)


def _ret_tables(seq, chunk, heads, dh):
    half = dh // 2
    w = heads * dh
    hw = w // 2
    inv_freq = ROPE_BASE ** (-np.arange(half, dtype=np.float64) / half)
    ang = np.arange(seq, dtype=np.float64)[:, None] * np.tile(inv_freq, heads)[None, :]
    cos = np.cos(ang).astype(np.float32)
    sin = np.sin(ang).astype(np.float32)
    log_gamma = np.log1p(-(2.0 ** (-5.0 - np.arange(heads, dtype=np.float64))))
    idx = np.arange(chunk, dtype=np.float64)
    rel = idx[:, None] - idx[None, :]
    intra = np.where(rel >= 0, np.exp(log_gamma[:, None, None] * np.maximum(rel, 0.0)), 0.0)
    head_nat = np.arange(w) // dh
    head_rot = (np.arange(w) % hw) // half
    xi = np.exp(log_gamma[head_nat][None, :] * (idx[:, None] + 1.0))
    zeta = np.exp(log_gamma[head_rot][None, :] * (chunk - 1.0 - idx[:, None]))
    decay = np.exp(log_gamma[head_nat] * chunk)[None, :]
    same = (head_rot[:, None] == head_nat[None, :]).astype(np.float32)
    gavg = (head_nat[:, None] == head_nat[None, :]).astype(np.float32) / dh
    f = lambda a: jnp.asarray(a, dtype=F32)
    return dict(cos=f(cos), sin=f(sin), intra=f(intra), xi=f(xi), zeta=f(zeta), decay=f(decay),
                same=f(same), gavg=jnp.asarray(gavg, dtype=BF16))


def _ret_kernel(q_ref, k_ref, v_ref, g_ref, cos_ref, sin_ref, intra_ref, xi_ref, zeta_ref,
                decay_ref, same_ref, gavg_ref, gnw_ref, o_ref, state_ref, *, heads, dh):
    w = heads * dh
    hw = w // 2
    half = dh // 2

    @pl.when(pl.program_id(1) == 0)
    def _():
        state_ref[...] = jnp.zeros_like(state_ref)

    cos = cos_ref[...]
    sin = sin_ref[...]

    def rot(x):
        x1, x2 = x[:, :hw], x[:, hw:]
        return jnp.concatenate([x1 * cos - x2 * sin, x1 * sin + x2 * cos], axis=1)

    lane = lax.broadcasted_iota(jnp.int32, (1, w), 1)
    head_rot = (lane % hw) // half
    head_nat = lane // dh
    gavg = gavg_ref[...]

    for b in range(q_ref.shape[0]):
        q = rot(q_ref[b])
        k = rot(k_ref[b]) * (dh ** -0.5)
        qb = q.astype(BF16)
        kb = k.astype(BF16)
        vb = v_ref[b].astype(BF16)

        o = _dot(qb, state_ref[b].astype(BF16)) * xi_ref[...]
        for h in range(heads):
            qh = jnp.where(head_rot == h, qb, jnp.zeros_like(qb))
            vh = jnp.where(head_nat == h, vb, jnp.zeros_like(vb))
            s = _dot_nt(qh, kb) * intra_ref[h]
            o = o + _dot(s.astype(BF16), vh)
        kz = (k * zeta_ref[...]).astype(BF16)
        state_ref[b] = decay_ref[...] * state_ref[b] + _dot_tn(kz, vb) * same_ref[...]

        mu = _split_dot(o, gavg)
        d = o - mu
        var = _split_dot(d * d, gavg)
        on = d * lax.rsqrt(var + GN_EPS) * gnw_ref[...]
        o_ref[b] = jax.nn.silu(g_ref[b]) * on


def retention_mixer(p_a, col0, gn_w, batch, seq, heads=RET_HEADS, chunk=256, nb=2):
    w = gn_w.shape[-1]
    dh = w // heads
    nc = seq // chunk
    tb = _ret_tables(seq, chunk, heads, dh)
    cb = col0 // w
    assert col0 % w == 0 and batch % nb == 0
    p3 = p_a.reshape(batch, seq, p_a.shape[1])
    row = lambda j: pl.BlockSpec((nb, chunk, w), lambda b, c, j=j: (b, c, cb + j))
    const = lambda shape: pl.BlockSpec(shape, lambda b, c: (0,) * len(shape))
    return pl.pallas_call(
        functools.partial(_ret_kernel, heads=heads, dh=dh),
        grid=(batch // nb, nc),
        in_specs=[row(0), row(1), row(2), row(3),
                  pl.BlockSpec((chunk, w // 2), lambda b, c: (c, 0)),
                  pl.BlockSpec((chunk, w // 2), lambda b, c: (c, 0)),
                  const((heads, chunk, chunk)), const((chunk, w)), const((chunk, w)),
                  const((1, w)), const((w, w)), const((w, w)), const((1, w))],
        out_specs=pl.BlockSpec((nb, chunk, w), lambda b, c: (b, c, 0)),
        out_shape=jax.ShapeDtypeStruct((batch, seq, w), F32),
        scratch_shapes=[pltpu.VMEM((nb, w, w), F32)],
        compiler_params=_cparams(("parallel", "arbitrary")),
        name="retention_mixer",
    )(p3, p3, p3, p3, tb["cos"], tb["sin"], tb["intra"], tb["xi"], tb["zeta"], tb["decay"],
      tb["same"], tb["gavg"], gn_w.reshape(1, w)).reshape(batch * seq, w)


def _group_rotary_halves(w, heads):
    lead = w.shape[:-1]
    dh = w.shape[-1] // heads
    return w.reshape(lead + (heads, 2, dh // 2)).swapaxes(-3, -2).reshape(lead + (heads * dh,))


def _sb_kernel(q_ref, k_ref, v_ref, o_ref, *, dh, blk):
    lane = lax.broadcasted_iota(jnp.int32, (1, 2 * dh), 1)
    r_id = lax.broadcasted_iota(jnp.int32, (blk, blk), 0)
    c_id = lax.broadcasted_iota(jnp.int32, (blk, blk), 1)
    neg_upper = jnp.where(r_id > c_id, -1.0, 0.0).astype(BF16)
    causal = c_id < r_id
    zscale = (dh ** -0.5) * LOG2_E
    hsel = [(lane // dh) == hh for hh in range(2)]

    def pair(qm, j, runs, diag):
        off = j * blk if isinstance(j, int) else pl.multiple_of(j * blk, blk)
        kb = k_ref[pl.ds(off, blk), :]
        vb = v_ref[pl.ds(off, blk), :]
        wgts, vms, new_runs = [], [], []
        for hh in range(2):
            zs = _dot_nt(qm[hh], kb) * zscale
            nfail = jnp.maximum(zs, 0.0) + jnp.log2(1.0 + jnp.exp2(jnp.minimum(zs, -zs)))
            if diag:
                nfail = jnp.where(causal, nfail, 0.0)
            after = _dot(nfail.astype(BF16), neg_upper) + runs[hh]
            wgt = jnp.exp2((zs - nfail) + after)
            if diag:
                wgt = jnp.where(causal, wgt, 0.0)
            wgts.append(wgt.astype(BF16))
            vms.append(jnp.where(hsel[hh], vb, jnp.zeros_like(vb)))
            new_runs.append(runs[hh] - jnp.sum(nfail, axis=1, keepdims=True))
        contrib = _dot(jnp.concatenate(wgts, axis=1), jnp.concatenate(vms, axis=0))
        return contrib, new_runs

    zero = jnp.zeros((blk, 1), F32)
    started = []
    for i in range(q_ref.shape[0] // blk):
        q2 = q_ref[i * blk:(i + 1) * blk, :]
        qm = [jnp.where(hs, q2, jnp.zeros_like(q2)) for hs in hsel]
        acc, runs = pair(qm, i, [zero, zero], True)
        if i >= 1:
            acc1, runs = pair(qm, i - 1, runs, False)
            acc = acc + acc1
        started.append((qm, acc, runs))

    for i, (qm, acc, (ra, rb)) in enumerate(started):
        if i >= 2:
            def cond(carry, i=i):
                jj, _, ra, rb = carry
                alive = jnp.maximum(jnp.max(ra), jnp.max(rb)) > SB_DEAD_LOG2
                return jnp.logical_and(jj <= i, alive)

            def body(carry, i=i, qm=qm):
                jj, acc, ra, rb = carry
                c, (ra, rb) = pair(qm, i - jj, [ra, rb], False)
                return jj + 1, acc + c, ra, rb

            _, acc, _, _ = lax.while_loop(cond, body, (jnp.int32(2), acc, ra, rb))
        o_ref[i * blk:(i + 1) * blk, :] = acc


def stick_breaking_mixer(p_sb, batch, seq, heads=SB_HEADS, blk=256):
    w = p_sb.shape[1] // 3
    dh = w // heads
    pw = 2 * dh
    assert pw == LANES and seq % blk == 0
    npair = heads // 2
    return pl.pallas_call(
        functools.partial(_sb_kernel, dh=dh, blk=blk),
        grid=(batch, npair),
        in_specs=[
            pl.BlockSpec((seq, pw), lambda b, p: (b, p)),
            pl.BlockSpec((seq, pw), lambda b, p: (b, npair + p)),
            pl.BlockSpec((seq, pw), lambda b, p: (b, 2 * npair + p)),
        ],
        out_specs=pl.BlockSpec((seq, pw), lambda b, p: (b, p)),
        out_shape=jax.ShapeDtypeStruct((batch * seq, w), F32),
        compiler_params=_cparams(("parallel", "parallel")),
        name="stick_breaking_mixer",
    )(p_sb, p_sb, p_sb)


def _mix_cross_kernel(h_ref, ya_ref, yb_ref, yc_ref, gain_ref, wout_ref, cnw_ref, wq_ref, kv_ref,
                      wo_ref, o_ref, *, heads):
    wa = ya_ref.shape[1]
    wb = yb_ref.shape[1]
    d = h_ref.shape[1]
    gain = gain_ref[...]
    ya = _rms(ya_ref[...], gain[:, :wa]).astype(BF16)
    yb = yb_ref[...].astype(BF16)
    yc = _rms(yc_ref[...], gain[:, wa + wb:]).astype(BF16)
    h1 = (h_ref[...] + _dot(ya, wout_ref[:wa, :]) + _dot(yb, wout_ref[wa:wa + wb, :])
          + _dot(yc, wout_ref[wa + wb:, :]))
    q = _dot(_rms(h1, cnw_ref[...]).astype(BF16), wq_ref[...])
    dh = d // heads
    outs = []
    for hd in range(heads):
        qh = q[:, hd * dh:(hd + 1) * dh].astype(BF16)
        kh = kv_ref[:, hd * dh:(hd + 1) * dh]
        vh = kv_ref[:, d + hd * dh:d + (hd + 1) * dh]
        s = _dot_nt(qh, kh) * (dh ** -0.5)
        s = s - jnp.max(s, axis=-1, keepdims=True)
        e = jnp.exp(s)
        p = e / jnp.sum(e, axis=-1, keepdims=True)
        outs.append(_dot(p.astype(BF16), vh).astype(BF16))
    o_ref[...] = h1 + _dot(jnp.concatenate(outs, axis=1), wo_ref[...])


def mix_cross(h, ya, yb, yc, gain, w_out, cnw, wq, kv, wo, layer, batch, seq, heads=CROSS_HEADS, tm=1024):
    t, d = h.shape
    m = kv.shape[0] // batch
    nt = seq // tm
    rows = lambda wd: pl.BlockSpec((tm, wd), lambda i: (i, 0))
    const = lambda shape: pl.BlockSpec(shape, lambda i: (0, 0))
    stack = pl.BlockSpec((None, d, d), lambda i: (layer, 0, 0))
    return pl.pallas_call(
        functools.partial(_mix_cross_kernel, heads=heads),
        grid=(t // tm,),
        in_specs=[rows(d), rows(ya.shape[1]), rows(yb.shape[1]), rows(yc.shape[1]),
                  const((1, d)), stack, const((1, d)), stack,
                  pl.BlockSpec((m, 2 * d), lambda i: (i // nt, 0)),
                  stack],
        out_specs=rows(d),
        out_shape=jax.ShapeDtypeStruct((t, d), F32),
        compiler_params=_cparams(("parallel",)),
        name="mix_cross",
    )(h, ya, yb, yc, gain.reshape(1, d), w_out, cnw.reshape(1, d), wq, kv, wo)


FFN_TM = 1024
FFN_TF = 1408
FFN_CHUNK = 256
FFN_VMEM_MB = 60


def _swiglu_accumulate(x, wg_ref, wu_ref, wd_ref, o_ref):
    tf = wd_ref.shape[0]
    for c0 in range(0, tf, FFN_CHUNK):
        c1 = min(c0 + FFN_CHUNK, tf)
        wg = wg_ref[:, c0:c1].astype(BF16)
        wu = wu_ref[:, c0:c1].astype(BF16)
        act = jax.nn.silu(_dot(x, wg)) * _dot(x, wu)
        o_ref[...] += _dot(act.astype(BF16), wd_ref[c0:c1, :].astype(BF16))


def _ffn_dense_kernel(h_ref, nw_ref, wg_ref, wu_ref, wd_ref, o_ref, xn_ref):
    f = pl.program_id(1)

    @pl.when(f == 0)
    def _():
        h = h_ref[...]
        xn_ref[...] = _rms(h, nw_ref[...]).astype(BF16)
        o_ref[...] = h

    _swiglu_accumulate(xn_ref[...], wg_ref, wu_ref, wd_ref, o_ref)


def ffn_dense(h, nw, w_gu, w_down, layer, tm=FFN_TM, tf=FFN_TF):
    t, d = h.shape
    ff = w_down.shape[1]
    nf = ff // tf
    return pl.pallas_call(
        _ffn_dense_kernel,
        grid=(t // tm, nf),
        in_specs=[
            pl.BlockSpec((tm, d), lambda i, f: (i, 0)),
            pl.BlockSpec((1, d), lambda i, f: (0, 0)),
            pl.BlockSpec((None, d, tf), lambda i, f: (layer, 0, f)),
            pl.BlockSpec((None, d, tf), lambda i, f: (layer, 0, nf + f)),
            pl.BlockSpec((None, tf, d), lambda i, f: (layer, f, 0)),
        ],
        out_specs=pl.BlockSpec((tm, d), lambda i, f: (i, 0)),
        out_shape=jax.ShapeDtypeStruct((t, d), F32),
        scratch_shapes=[pltpu.VMEM((tm, d), BF16)],
        compiler_params=_cparams(("parallel", "arbitrary"), FFN_VMEM_MB),
        name="ffn_dense",
    )(h, nw.reshape(1, d), w_gu, w_gu, w_down)


def _ffn_group_kernel(te_ref, nv_ref, rows_ref, x_ref, wg_ref, wu_ref, wd_ref, y_ref, xb_ref, acc_ref):
    i = pl.program_id(0)
    f = pl.program_id(1)
    valid = i < nv_ref[0]

    @pl.when(f == 0)
    def _():
        row = lax.broadcasted_iota(jnp.int32, (x_ref.shape[0], 1), 0)
        xb_ref[...] = jnp.where(row < rows_ref[i], _unpack_bf16_pairs(x_ref[...]), 0.0).astype(BF16)
        acc_ref[...] = jnp.zeros_like(acc_ref)

    @pl.when(valid)
    def _():
        _swiglu_accumulate(xb_ref[...], wg_ref.at[0], wu_ref.at[0], wd_ref.at[0], acc_ref)

    @pl.when(f == pl.num_programs(1) - 1)
    def _():
        half = acc_ref.shape[1] // 2
        y_ref[...] = _pack_bf16_pairs(acc_ref[:, :half], acc_ref[:, half:])


def ffn_grouped(xs, tile_expert, n_valid, tile_rows, w_gu, w_down, expert_base, tm=FFN_TM, tf=FFN_TF):
    p = xs.shape[0]
    d = 2 * xs.shape[1]
    ff = w_down.shape[1]
    nf = ff // tf

    def live(i, f, nv):
        ok = i < nv[0]
        return jnp.where(ok, i, nv[0] - 1), jnp.where(ok, f, nf - 1)

    def x_map(i, f, te, nv, rows):
        ii, _ = live(i, f, nv)
        return ii, 0

    def wg_map(i, f, te, nv, rows):
        ii, ff_ = live(i, f, nv)
        return expert_base + te[ii], 0, ff_

    def wu_map(i, f, te, nv, rows):
        ii, ff_ = live(i, f, nv)
        return expert_base + te[ii], 0, nf + ff_

    def wd_map(i, f, te, nv, rows):
        ii, ff_ = live(i, f, nv)
        return expert_base + te[ii], ff_, 0

    grid_spec = pltpu.PrefetchScalarGridSpec(
        num_scalar_prefetch=3,
        grid=(p // tm, nf),
        in_specs=[
            pl.BlockSpec((tm, d // 2), x_map),
            pl.BlockSpec((1, d, tf), wg_map),
            pl.BlockSpec((1, d, tf), wu_map),
            pl.BlockSpec((1, tf, d), wd_map),
        ],
        out_specs=pl.BlockSpec((tm, d // 2), lambda i, f, te, nv, rows: (i, 0)),
        scratch_shapes=[pltpu.VMEM((tm, d), BF16), pltpu.VMEM((tm, d), F32)],
    )
    return pl.pallas_call(
        _ffn_group_kernel,
        grid_spec=grid_spec,
        out_shape=jax.ShapeDtypeStruct((p, d // 2), jnp.uint32),
        compiler_params=_cparams(("arbitrary", "arbitrary"), FFN_VMEM_MB),
        name="ffn_grouped",
    )(tile_expert, n_valid, tile_rows, xs, w_gu, w_gu, w_down)


V7X_SC_CORES = 2
V7X_SC_SUBCORES = 16
V7X_SC_WORKERS = V7X_SC_CORES * V7X_SC_SUBCORES


def _sc_mesh():
    return plsc.VectorSubcoreMesh(core_axis_name="c", subcore_axis_name="s")


def sc_scatter_rows2(x, idx1, idx2, n_out, ch=128):
    t, d = x.shape
    assert t % (V7X_SC_WORKERS * ch) == 0
    t_per_w = t // V7X_SC_WORKERS
    n_ch = t_per_w // ch

    @functools.partial(
        pl.kernel, mesh=_sc_mesh(), out_type=jax.ShapeDtypeStruct((n_out, d), x.dtype),
        scratch_types=[pltpu.VMEM((ch,), jnp.int32), pltpu.VMEM((ch,), jnp.int32),
                       pltpu.VMEM((ch, d), x.dtype), pltpu.SemaphoreType.DMA])
    def k(x_hbm, i1_hbm, i2_hbm, out_hbm, i1_v, i2_v, rows_v, sem):
        wid = lax.axis_index("s") * V7X_SC_CORES + lax.axis_index("c")
        base = wid * t_per_w

        @pl.loop(0, n_ch)
        def _(j):
            off = pl.multiple_of(base + j * ch, 8)
            pltpu.sync_copy(i1_hbm.at[pl.ds(off, ch)], i1_v)
            pltpu.sync_copy(i2_hbm.at[pl.ds(off, ch)], i2_v)
            pltpu.sync_copy(x_hbm.at[pl.ds(off, ch)], rows_v)
            pltpu.async_copy(rows_v, out_hbm.at[i1_v], sem).wait()
            pltpu.async_copy(rows_v, out_hbm.at[i2_v], sem).wait()

    return k(x, idx1, idx2)


def sc_gather_rows(table, idx, ch=128):
    _, d = table.shape
    b = idx.shape[0]
    assert b % (V7X_SC_WORKERS * ch) == 0
    b_per_w = b // V7X_SC_WORKERS
    n_ch = b_per_w // ch

    @functools.partial(
        pl.kernel, mesh=_sc_mesh(), out_type=jax.ShapeDtypeStruct((b, d), table.dtype),
        scratch_types=[pltpu.VMEM((ch,), jnp.int32), pltpu.VMEM((ch, d), table.dtype),
                       pltpu.SemaphoreType.DMA])
    def k(table_hbm, idx_hbm, out_hbm, idx_v, rows_v, sem):
        wid = lax.axis_index("s") * V7X_SC_CORES + lax.axis_index("c")
        base = wid * b_per_w

        @pl.loop(0, n_ch)
        def _(j):
            off = pl.multiple_of(base + j * ch, 8)
            pltpu.sync_copy(idx_hbm.at[pl.ds(off, ch)], idx_v)
            pltpu.async_copy(table_hbm.at[idx_v], rows_v, sem).wait()
            pltpu.sync_copy(rows_v, out_hbm.at[pl.ds(off, ch)])

    return k(table, idx)


def _moe_combine_kernel(h_ref, y1_ref, y2_ref, g1_ref, g2_ref, nw_ref, o_ref, *, normed):
    out = (h_ref[...] + g1_ref[...] * _unpack_bf16_pairs(y1_ref[...])
           + g2_ref[...] * _unpack_bf16_pairs(y2_ref[...]))
    o_ref[...] = _rms(out, nw_ref[...]) if normed else out


def moe_combine(h, yg, gate1, gate2, norm_w=None, tm=1024):
    t, d = h.shape
    nt = t // tm
    row = pl.BlockSpec((tm, d), lambda i: (i, 0))
    col = pl.BlockSpec((tm, 1), lambda i: (i, 0))
    nw = jnp.ones((d,), F32) if norm_w is None else norm_w
    return pl.pallas_call(
        functools.partial(_moe_combine_kernel, normed=norm_w is not None),
        grid=(nt,),
        in_specs=[row, pl.BlockSpec((tm, d // 2), lambda i: (i, 0)),
                  pl.BlockSpec((tm, d // 2), lambda i: (nt + i, 0)), col, col,
                  pl.BlockSpec((1, d), lambda i: (0, 0))],
        out_specs=row,
        out_shape=jax.ShapeDtypeStruct((t, d), F32),
        compiler_params=_cparams(("parallel",)),
        name="moe_combine",
    )(h, yg, yg, gate1, gate2, nw.reshape(1, d))


def _route(h, nw, wr, n_exp):
    xn = _rms(h, nw)
    half = xn.shape[1] // 2
    packed = _pack_bf16_pairs(xn[:, :half], xn[:, half:])
    xh = xn.astype(BF16)
    xl = (xn - xh.astype(F32)).astype(BF16)
    wh = wr.astype(BF16)
    wl = (wr - wh.astype(F32)).astype(BF16)
    logits = _dot(xh, wh) + _dot(xh, wl) + _dot(xl, wh)
    lane = lax.broadcasted_iota(jnp.int32, logits.shape, 1)
    neg = -jnp.inf
    lg = jnp.where(lane < n_exp, logits, neg)
    m1 = jnp.max(lg, axis=-1, keepdims=True)
    i1 = jnp.min(jnp.where(lg == m1, lane, LANES), axis=-1, keepdims=True)
    lg2 = jnp.where(lane == i1, neg, lg)
    m2 = jnp.max(lg2, axis=-1, keepdims=True)
    i2 = jnp.min(jnp.where(lg2 == m2, lane, LANES), axis=-1, keepdims=True)
    e2 = jnp.exp(m2 - m1)
    g1 = 1.0 / (1.0 + e2)
    g2 = e2 / (1.0 + e2)
    slot = jnp.where(lane == i1, 1, jnp.where(lane == i2, 2, 0))
    gate = jnp.where(lane == i1, g1, jnp.where(lane == i2, g2, 0.0))
    return packed, slot, gate


def _router_kernel(h_ref, nw_ref, wr_ref, tri_ref, xn_ref, slot_ref, gate_ref, rank_ref, cnt_ref, *, n_exp):
    @pl.when(pl.program_id(0) == 0)
    def _():
        cnt_ref[...] = jnp.zeros_like(cnt_ref)

    xn_ref[...], slot, gate = _route(h_ref[...], nw_ref[...], wr_ref[...], n_exp)
    slot_ref[...] = slot[:, :n_exp]
    gate_ref[...] = gate[:, :n_exp]
    chosen = jnp.where(slot > 0, 1.0, 0.0)
    tri = tri_ref[...]
    grp = tri.shape[0]
    seen = cnt_ref[...]
    for r0 in range(0, chosen.shape[0], grp):
        rows = chosen[r0:r0 + grp]
        before = _dot(tri, rows.astype(BF16)) + seen
        rank_ref[r0:r0 + grp, :] = before.astype(jnp.int32)[:, :n_exp]
        seen = seen + jnp.sum(rows, axis=0, keepdims=True)
    cnt_ref[...] = seen


def router(h, nw, w_router, tm=1024):
    t, d = h.shape
    n_exp = w_router.shape[1]
    wr = jnp.pad(w_router, ((0, 0), (0, LANES - n_exp)))
    tri = jnp.asarray(np.tril(np.ones((LANES, LANES), np.float32), -1), dtype=BF16)
    per_expert = pl.BlockSpec((tm, n_exp), lambda i: (i, 0))
    return pl.pallas_call(
        functools.partial(_router_kernel, n_exp=n_exp),
        grid=(t // tm,),
        in_specs=[pl.BlockSpec((tm, d), lambda i: (i, 0)),
                  pl.BlockSpec((1, d), lambda i: (0, 0)),
                  pl.BlockSpec((d, LANES), lambda i: (0, 0)),
                  pl.BlockSpec((LANES, LANES), lambda i: (0, 0))],
        out_specs=[pl.BlockSpec((tm, d // 2), lambda i: (i, 0)), per_expert, per_expert, per_expert,
                   pl.BlockSpec((1, LANES), lambda i: (0, 0))],
        out_shape=[jax.ShapeDtypeStruct((t, d // 2), jnp.uint32),
                   jax.ShapeDtypeStruct((t, n_exp), jnp.int32),
                   jax.ShapeDtypeStruct((t, n_exp), F32),
                   jax.ShapeDtypeStruct((t, n_exp), jnp.int32),
                   jax.ShapeDtypeStruct((1, LANES), F32)],
        compiler_params=_cparams(("arbitrary",)),
        name="router",
    )(h, nw.reshape(1, d), wr, tri)


def _dispatch_plan(slot, gate, rank, totals, tm):
    t, n_exp = slot.shape
    counts = totals[0, :n_exp].astype(jnp.int32)
    tiles = (counts + tm - 1) // tm
    tile_end = jnp.cumsum(tiles)
    start = (tile_end - tiles) * tm
    pos = start[None, :] + rank
    n_rows = 2 * t + n_exp * tm
    n_tiles = n_rows // tm
    pos1 = jnp.sum(jnp.where(slot == 1, pos, 0), axis=1)
    pos2 = jnp.sum(jnp.where(slot == 2, pos, 0), axis=1)
    gate1 = jnp.sum(jnp.where(slot == 1, gate, 0.0), axis=1, keepdims=True)
    gate2 = jnp.sum(jnp.where(slot == 2, gate, 0.0), axis=1, keepdims=True)
    tile_id = jnp.arange(n_tiles, dtype=jnp.int32)
    owner = tile_id[:, None] >= tile_end[None, :]
    tile_expert = jnp.minimum(jnp.sum(owner.astype(jnp.int32), axis=1), n_exp - 1)
    mine = tile_expert[:, None] == jnp.arange(n_exp, dtype=jnp.int32)[None, :]
    group_end = jnp.sum(jnp.where(mine, (start + counts)[None, :], 0), axis=1)
    n_valid = tile_end[-1:].astype(jnp.int32)
    tile_rows = jnp.where(tile_id < n_valid[0], jnp.clip(group_end - tile_id * tm, 0, tm), 0)
    return (n_rows, pos1, pos2, gate1, gate2, tile_expert.astype(jnp.int32), n_valid,
            tile_rows.astype(jnp.int32))


def _final_norm_kernel(h_ref, w_ref, o_ref):
    o_ref[...] = _rms(h_ref[...], w_ref[...])


def final_norm(h, w, tm=1024):
    t, d = h.shape
    return pl.pallas_call(
        _final_norm_kernel,
        grid=(t // tm,),
        in_specs=[pl.BlockSpec((tm, d), lambda i: (i, 0)), pl.BlockSpec((1, d), lambda i: (0, 0))],
        out_specs=pl.BlockSpec((tm, d), lambda i: (i, 0)),
        out_shape=jax.ShapeDtypeStruct((t, d), F32),
        compiler_params=_cparams(("parallel",)),
        name="final_norm",
    )(h, w.reshape(1, d))


def kernel(x, mem, mix_norm_w, w_in, ssm_lambda_re, ssm_lambda_im, ssm_b_re, ssm_b_im, ssm_c_re, ssm_c_im, ssm_d, ssm_log_dt, ssm_w_glu, mix_out_gain, w_out, cross_norm_w, mem_norm_w, w_cross_q, w_cross_kv, w_cross_o, ffn_norm_w, w_dense_gu, w_dense_down, w_router, w_expert_gu, w_expert_down, final_norm_w):
    batch, seq, d = x.shape
    depth = w_in.shape[0]
    mem_len = mem.shape[1]
    ssm_w = ssm_d.shape[1]
    g, p = ssm_lambda_re.shape[1:]
    nh = ssm_b_re.shape[-1]
    ret_w = ssm_w
    sb_w = d - ssm_w - ret_w
    ret_dh = ret_w // RET_HEADS
    moe_tm = FFN_TM

    w_in_b = jnp.concatenate(
        [w_in[..., :ssm_w],
         _group_rotary_halves(w_in[..., ssm_w:ssm_w + ret_w], RET_HEADS),
         _group_rotary_halves(w_in[..., ssm_w + ret_w:ssm_w + 2 * ret_w], RET_HEADS),
         w_in[..., ssm_w + 2 * ret_w:]], axis=-1).astype(BF16)
    w_out_b = w_out.astype(BF16)
    w_q_b = w_cross_q.astype(BF16)
    w_kv_b = w_cross_kv.astype(BF16)
    w_o_b = w_cross_o.astype(BF16)
    n_exp = w_expert_gu.shape[1]
    w_egu = w_expert_gu.reshape((-1,) + w_expert_gu.shape[2:])
    w_edn = w_expert_down.reshape((-1,) + w_expert_down.shape[2:])
    n_a = ssm_w + 4 * ret_w

    apow, bbc = s5_prep(ssm_lambda_re, ssm_lambda_im, ssm_log_dt, ssm_b_re, ssm_b_im,
                        nstep=S5_TIME_TILE // SUBLANES)

    mem2 = mem.reshape(batch * mem_len, d)
    h = x.reshape(batch * seq, d)
    for i in range(depth):
        p_a, p_sb = norm_proj(h, mix_norm_w[i], w_in_b, i, (n_a, 3 * sb_w), (F32, BF16))
        bb = _block_diag_in(bbc[i], g, p, nh).astype(BF16)
        cc = _block_diag_out(ssm_c_re[i], ssm_c_im[i]).astype(BF16)
        y_a = s5_mixer(p_a, bb, cc, apow[i], ssm_d[i], ssm_w_glu[i].astype(BF16), batch, seq)
        gain = mix_out_gain[i]
        y_b = retention_mixer(p_a, ssm_w, gain[ssm_w:ssm_w + ret_w], batch, seq)
        y_c = stick_breaking_mixer(p_sb, batch, seq)
        (kv,) = norm_proj(mem2, mem_norm_w, w_kv_b, i, (2 * d,), (BF16,))
        h = mix_cross(h, y_a, y_b, y_c, gain, w_out_b, cross_norm_w[i], w_q_b, kv, w_o_b, i, batch, seq)
        if i % 2 == 0:
            h = ffn_dense(h, ffn_norm_w[i], w_dense_gu, w_dense_down, i // 2)
        else:
            xn, slot, gate, rank, totals = router(h, ffn_norm_w[i], w_router[i // 2])
            n_rows, pos1, pos2, gate1, gate2, tile_expert, n_valid, tile_rows = _dispatch_plan(
                slot, gate, rank, totals, moe_tm)
            xs = sc_scatter_rows2(xn, pos1, pos2, n_rows)
            y = ffn_grouped(xs, tile_expert, n_valid, tile_rows, w_egu, w_edn, (i // 2) * n_exp,
                            tm=moe_tm)
            yg = sc_gather_rows(y, jnp.concatenate([pos1, pos2]))
            last = i == depth - 1
            h = moe_combine(h, yg, gate1, gate2, norm_w=final_norm_w if last else None)
    if depth % 2 == 1:
        h = final_norm(h, final_norm_w)
    return h.reshape(batch, seq, d)
```

```python
import functools

import numpy as np
import jax
import jax.numpy as jnp
from jax import lax
from jax.experimental import pallas as pl
from jax.experimental.pallas import tpu as pltpu
from jax.experimental.pallas import tpu_sc as plsc

F32 = jnp.float32
BF16 = jnp.bfloat16

NORM_EPS = 1e-6
GN_EPS = 1e-6
ROPE_BASE = 10000.0
LOG2_E = 1.4426950408889634
SB_DEAD_LOG2 = -150.0

RET_HEADS = 4
SB_HEADS = 8
CROSS_HEADS = 4

LANES = 128
SUBLANES = 8


def _cparams(sem, vmem_mb=48):
    return pltpu.CompilerParams(dimension_semantics=sem, vmem_limit_bytes=vmem_mb * 1024 * 1024)


def _rms(x, w):
    ms = jnp.mean(x * x, axis=-1, keepdims=True)
    return x * lax.rsqrt(ms + NORM_EPS) * w


def _dot(a, b):
    return jnp.dot(a, b, preferred_element_type=F32)


def _dot_nt(a, b):
    return lax.dot_general(a, b, (((1,), (1,)), ((), ())), preferred_element_type=F32)


def _dot_tn(a, b):
    return lax.dot_general(a, b, (((0,), (0,)), ((), ())), preferred_element_type=F32)


def _pack_bf16_pairs(a, b):
    ua = lax.bitcast_convert_type(a.astype(BF16).astype(F32), jnp.uint32)
    ub = lax.bitcast_convert_type(b.astype(BF16).astype(F32), jnp.uint32)
    return (ub & jnp.uint32(0xFFFF0000)) | (ua >> 16)


def _unpack_bf16_pairs(p):
    lo = lax.bitcast_convert_type(p << 16, F32)
    hi = lax.bitcast_convert_type(p & jnp.uint32(0xFFFF0000), F32)
    return jnp.concatenate([lo, hi], axis=1)


def _split_dot(x, m_bf16):
    hi = x.astype(BF16)
    lo = (x - hi.astype(F32)).astype(BF16)
    return _dot(hi, m_bf16) + _dot(lo, m_bf16)


def _norm_proj_kernel(x_ref, nw_ref, w_ref, *o_refs, splits, chunk):
    xn = _rms(x_ref[...], nw_ref[...]).astype(BF16)
    col = 0
    for o_ref, width in zip(o_refs, splits):
        for c0 in range(0, width, chunk):
            r = _dot(xn, w_ref[:, col + c0:col + c0 + chunk].astype(BF16))
            o_ref[:, c0:c0 + chunk] = r.astype(o_ref.dtype)
        col += width


def norm_proj(x, nw, w, layer, splits, dtypes, tm=1024, chunk=256):
    rows, d = x.shape
    n = w.shape[2]
    assert sum(splits) == n and rows % tm == 0
    return pl.pallas_call(
        functools.partial(_norm_proj_kernel, splits=splits, chunk=chunk),
        grid=(rows // tm,),
        in_specs=[
            pl.BlockSpec((tm, d), lambda i: (i, 0)),
            pl.BlockSpec((1, d), lambda i: (0, 0)),
            pl.BlockSpec((None, d, n), lambda i: (layer, 0, 0)),
        ],
        out_specs=[pl.BlockSpec((tm, s), lambda i: (i, 0)) for s in splits],
        out_shape=[jax.ShapeDtypeStruct((rows, s), dt) for s, dt in zip(splits, dtypes)],
        compiler_params=_cparams(("parallel",)),
        name="norm_proj",
    )(x, nw.reshape(1, d), w)


def _s5_prep_kernel(lr_ref, li_ref, ldt_ref, br_ref, bi_ref, apow_ref, bb_ref):
    lr = lr_ref[0]
    li = li_ref[0]
    dt = jnp.exp(ldt_ref[0])
    mag = jnp.exp(lr * dt)
    a_re = mag * jnp.cos(li * dt)
    a_im = mag * jnp.sin(li * dt)
    denom = lr * lr + li * li
    nr = a_re - 1.0
    z_re = (nr * lr + a_im * li) / denom
    z_im = (a_im * lr - nr * li) / denom
    br = br_ref[0]
    bi = bi_ref[0]
    nh = br.shape[0]
    bb_ref[0, :nh, :] = z_re * br - z_im * bi
    bb_ref[0, nh:, :] = z_re * bi + z_im * br
    nstep = (apow_ref.shape[1] - 2 * S5_SEG_LEVELS) // 2
    pr, pi = a_re, a_im
    for j in range(nstep):
        apow_ref[0, j:j + 1, :] = pr
        apow_ref[0, nstep + j:nstep + j + 1, :] = pi
        if j + 1 < nstep:
            pr, pi = pr * a_re - pi * a_im, pr * a_im + pi * a_re
    for k in range(S5_SEG_LEVELS):
        apow_ref[0, 2 * nstep + k:2 * nstep + k + 1, :] = pr
        apow_ref[0, 2 * nstep + S5_SEG_LEVELS + k:2 * nstep + S5_SEG_LEVELS + k + 1, :] = pi
        pr, pi = pr * pr - pi * pi, 2.0 * pr * pi


S5_SEG_LEVELS = 3
S5_TIME_TILE = 512


def s5_prep(lam_re, lam_im, log_dt, b_re, b_im, nstep):
    depth, g, p = lam_re.shape
    nh = b_re.shape[-1]
    gp = g * p
    lr = lam_re.reshape(depth, 1, gp)
    li = lam_im.reshape(depth, 1, gp)
    ldt = jnp.repeat(log_dt, p, axis=1).reshape(depth, 1, gp)
    br = b_re.transpose(0, 3, 1, 2).reshape(depth, nh, gp)
    bi = b_im.transpose(0, 3, 1, 2).reshape(depth, nh, gp)
    vec = pl.BlockSpec((1, 1, gp), lambda i: (i, 0, 0))
    mat = pl.BlockSpec((1, nh, gp), lambda i: (i, 0, 0))
    npow = 2 * (nstep + S5_SEG_LEVELS)
    return pl.pallas_call(
        _s5_prep_kernel,
        grid=(depth,),
        in_specs=[vec, vec, vec, mat, mat],
        out_specs=[pl.BlockSpec((1, npow, gp), lambda i: (i, 0, 0)),
                   pl.BlockSpec((1, 2 * nh, gp), lambda i: (i, 0, 0))],
        out_shape=[jax.ShapeDtypeStruct((depth, npow, gp), F32),
                   jax.ShapeDtypeStruct((depth, 2 * nh, gp), F32)],
        compiler_params=_cparams(("arbitrary",)),
        name="s5_prep",
    )(lr, li, ldt, br, bi)


def _s5_kernel(u_ref, bb_ref, cc_ref, apow_ref, d_ref, wglu_ref, o_ref, xs_ref, carry_ref, perm_ref,
               *, gp):
    ts = u_ref.shape[0]
    nstep = ts // SUBLANES
    rows = lambda s: slice(s * SUBLANES, (s + 1) * SUBLANES)

    @pl.when(pl.program_id(1) == 0)
    def _():
        carry_ref[...] = jnp.zeros_like(carry_ref)

    ntile = u_ref.shape[1] // LANES
    for c in range(ntile):
        perm_ref[c] = u_ref[:, c * LANES:(c + 1) * LANES]
    u = jnp.concatenate(
        [jnp.concatenate([perm_ref[c, pl.ds(s, SUBLANES, stride=nstep), :] for c in range(ntile)], axis=1)
         for s in range(nstep)], axis=0)
    xs_ref[...] = _dot(u.astype(BF16), bb_ref[...])

    ar = apow_ref[0:1, :]
    ai = apow_ref[nstep:nstep + 1, :]
    xr = xs_ref[rows(0), :gp]
    xi = xs_ref[rows(0), gp:]
    for s in range(1, nstep):
        xr, xi = (ar * xr - ai * xi + xs_ref[rows(s), :gp], ar * xi + ai * xr + xs_ref[rows(s), gp:])
        xs_ref[rows(s), :gp] = xr
        xs_ref[rows(s), gp:] = xi

    seg = lax.broadcasted_iota(jnp.int32, (SUBLANES, 1), 0)
    cr = carry_ref[0:1, :]
    ci = carry_ref[1:2, :]
    lr = apow_ref[2 * nstep:2 * nstep + 1, :]
    li = apow_ref[2 * nstep + S5_SEG_LEVELS:2 * nstep + S5_SEG_LEVELS + 1, :]
    first = seg == 0
    xr = xr + jnp.where(first, lr * cr - li * ci, 0.0)
    xi = xi + jnp.where(first, lr * ci + li * cr, 0.0)
    for k in range(S5_SEG_LEVELS):
        sh = 1 << k
        pr = apow_ref[2 * nstep + k:2 * nstep + k + 1, :]
        pi = apow_ref[2 * nstep + S5_SEG_LEVELS + k:2 * nstep + S5_SEG_LEVELS + k + 1, :]
        keep = seg >= sh
        sr = jnp.where(keep, pltpu.roll(xr, sh, axis=0), 0.0)
        si = jnp.where(keep, pltpu.roll(xi, sh, axis=0), 0.0)
        xr, xi = xr + pr * sr - pi * si, xi + pr * si + pi * sr
    carry_ref[0:1, :] = xr[SUBLANES - 1:, :]
    carry_ref[1:2, :] = xi[SUBLANES - 1:, :]
    enter_r = jnp.where(first, cr, pltpu.roll(xr, 1, axis=0))
    enter_i = jnp.where(first, ci, pltpu.roll(xi, 1, axis=0))

    for s in range(nstep):
        pr = apow_ref[s:s + 1, :]
        pi = apow_ref[nstep + s:nstep + s + 1, :]
        xs_ref[rows(s), :gp] += pr * enter_r - pi * enter_i
        xs_ref[rows(s), gp:] += pr * enter_i + pi * enter_r

    y = _dot(xs_ref[...].astype(BF16), cc_ref[...]) + d_ref[...] * u
    g = jax.nn.gelu(y)
    out = g * jax.nn.sigmoid(_dot(g.astype(BF16), wglu_ref[...]))
    for s in range(nstep):
        for c in range(ntile):
            perm_ref[c, pl.ds(s, SUBLANES, stride=nstep), :] = out[rows(s), c * LANES:(c + 1) * LANES]
    for c in range(ntile):
        o_ref[:, c * LANES:(c + 1) * LANES] = perm_ref[c]


def s5_mixer(p_a, bb, cc, apow, d_skip, w_glu, batch, seq):
    w = d_skip.shape[-1]
    gp = apow.shape[-1]
    ts = (apow.shape[0] // 2 - S5_SEG_LEVELS) * SUBLANES
    nt = seq // ts
    return pl.pallas_call(
        functools.partial(_s5_kernel, gp=gp),
        grid=(batch, nt),
        in_specs=[
            pl.BlockSpec((ts, w), lambda b, t: (b * nt + t, 0)),
            pl.BlockSpec((w, 2 * gp), lambda b, t: (0, 0)),
            pl.BlockSpec((2 * gp, w), lambda b, t: (0, 0)),
            pl.BlockSpec(apow.shape, lambda b, t: (0, 0)),
            pl.BlockSpec((1, w), lambda b, t: (0, 0)),
            pl.BlockSpec((w, w), lambda b, t: (0, 0)),
        ],
        out_specs=pl.BlockSpec((ts, w), lambda b, t: (b * nt + t, 0)),
        out_shape=jax.ShapeDtypeStruct((batch * seq, w), F32),
        scratch_shapes=[pltpu.VMEM((ts, 2 * gp), F32), pltpu.VMEM((2, gp), F32),
                        pltpu.VMEM((w // LANES, ts, LANES), F32)],
        compiler_params=_cparams(("parallel", "arbitrary")),
        name="s5_mixer",
    )(p_a, bb, cc, apow, d_skip.reshape(1, w), w_glu)


def _block_diag_in(bb, g, p, nh):
    eye = jnp.eye(g, dtype=F32)
    def one(m):
        m = m.reshape(nh, g, p)
        return jnp.einsum('hgp,kg->khgp', m, eye).reshape(g * nh, g * p)
    return jnp.concatenate([one(bb[:nh]), one(bb[nh:])], axis=1)


def _block_diag_out(c_re, c_im):
    g, nh, p = c_re.shape
    eye = jnp.eye(g, dtype=F32)
    def one(m):
        return jnp.einsum('ghp,gk->gpkh', m, eye).reshape(g * p, g * nh)
    return jnp.concatenate([one(c_re), one(-c_im)], axis=0)


def _ret_tables(seq, chunk, heads, dh):
    half = dh // 2
    w = heads * dh
    hw = w // 2
    inv_freq = ROPE_BASE ** (-np.arange(half, dtype=np.float64) / half)
    ang = np.arange(seq, dtype=np.float64)[:, None] * np.tile(inv_freq, heads)[None, :]
    cos = np.cos(ang).astype(np.float32)
    sin = np.sin(ang).astype(np.float32)
    log_gamma = np.log1p(-(2.0 ** (-5.0 - np.arange(heads, dtype=np.float64))))
    idx = np.arange(chunk, dtype=np.float64)
    rel = idx[:, None] - idx[None, :]
    intra = np.where(rel >= 0, np.exp(log_gamma[:, None, None] * np.maximum(rel, 0.0)), 0.0)
    head_nat = np.arange(w) // dh
    head_rot = (np.arange(w) % hw) // half
    xi = np.exp(log_gamma[head_nat][None, :] * (idx[:, None] + 1.0))
    zeta = np.exp(log_gamma[head_rot][None, :] * (chunk - 1.0 - idx[:, None]))
    decay = np.exp(log_gamma[head_nat] * chunk)[None, :]
    same = (head_rot[:, None] == head_nat[None, :]).astype(np.float32)
    gavg = (head_nat[:, None] == head_nat[None, :]).astype(np.float32) / dh
    f = lambda a: jnp.asarray(a, dtype=F32)
    return dict(cos=f(cos), sin=f(sin), intra=f(intra), xi=f(xi), zeta=f(zeta), decay=f(decay),
                same=f(same), gavg=jnp.asarray(gavg, dtype=BF16))


def _ret_kernel(q_ref, k_ref, v_ref, g_ref, cos_ref, sin_ref, intra_ref, xi_ref, zeta_ref,
                decay_ref, same_ref, gavg_ref, gnw_ref, o_ref, state_ref, *, heads, dh):
    w = heads * dh
    hw = w // 2
    half = dh // 2

    @pl.when(pl.program_id(1) == 0)
    def _():
        state_ref[...] = jnp.zeros_like(state_ref)

    cos = cos_ref[...]
    sin = sin_ref[...]

    def rot(x):
        x1, x2 = x[:, :hw], x[:, hw:]
        return jnp.concatenate([x1 * cos - x2 * sin, x1 * sin + x2 * cos], axis=1)

    lane = lax.broadcasted_iota(jnp.int32, (1, w), 1)
    head_rot = (lane % hw) // half
    head_nat = lane // dh
    gavg = gavg_ref[...]

    for b in range(q_ref.shape[0]):
        q = rot(q_ref[b])
        k = rot(k_ref[b]) * (dh ** -0.5)
        qb = q.astype(BF16)
        kb = k.astype(BF16)
        vb = v_ref[b].astype(BF16)

        o = _dot(qb, state_ref[b].astype(BF16)) * xi_ref[...]
        for h in range(heads):
            qh = jnp.where(head_rot == h, qb, jnp.zeros_like(qb))
            vh = jnp.where(head_nat == h, vb, jnp.zeros_like(vb))
            s = _dot_nt(qh, kb) * intra_ref[h]
            o = o + _dot(s.astype(BF16), vh)
        kz = (k * zeta_ref[...]).astype(BF16)
        state_ref[b] = decay_ref[...] * state_ref[b] + _dot_tn(kz, vb) * same_ref[...]

        mu = _split_dot(o, gavg)
        d = o - mu
        var = _split_dot(d * d, gavg)
        on = d * lax.rsqrt(var + GN_EPS) * gnw_ref[...]
        o_ref[b] = jax.nn.silu(g_ref[b]) * on


def retention_mixer(p_a, col0, gn_w, batch, seq, heads=RET_HEADS, chunk=256, nb=2):
    w = gn_w.shape[-1]
    dh = w // heads
    nc = seq // chunk
    tb = _ret_tables(seq, chunk, heads, dh)
    cb = col0 // w
    assert col0 % w == 0 and batch % nb == 0
    p3 = p_a.reshape(batch, seq, p_a.shape[1])
    row = lambda j: pl.BlockSpec((nb, chunk, w), lambda b, c, j=j: (b, c, cb + j))
    const = lambda shape: pl.BlockSpec(shape, lambda b, c: (0,) * len(shape))
    return pl.pallas_call(
        functools.partial(_ret_kernel, heads=heads, dh=dh),
        grid=(batch // nb, nc),
        in_specs=[row(0), row(1), row(2), row(3),
                  pl.BlockSpec((chunk, w // 2), lambda b, c: (c, 0)),
                  pl.BlockSpec((chunk, w // 2), lambda b, c: (c, 0)),
                  const((heads, chunk, chunk)), const((chunk, w)), const((chunk, w)),
                  const((1, w)), const((w, w)), const((w, w)), const((1, w))],
        out_specs=pl.BlockSpec((nb, chunk, w), lambda b, c: (b, c, 0)),
        out_shape=jax.ShapeDtypeStruct((batch, seq, w), F32),
        scratch_shapes=[pltpu.VMEM((nb, w, w), F32)],
        compiler_params=_cparams(("parallel", "arbitrary")),
        name="retention_mixer",
    )(p3, p3, p3, p3, tb["cos"], tb["sin"], tb["intra"], tb["xi"], tb["zeta"], tb["decay"],
      tb["same"], tb["gavg"], gn_w.reshape(1, w)).reshape(batch * seq, w)


def _group_rotary_halves(w, heads):
    lead = w.shape[:-1]
    dh = w.shape[-1] // heads
    return w.reshape(lead + (heads, 2, dh // 2)).swapaxes(-3, -2).reshape(lead + (heads * dh,))


def _sb_kernel(q_ref, k_ref, v_ref, o_ref, *, dh, blk):
    lane = lax.broadcasted_iota(jnp.int32, (1, 2 * dh), 1)
    r_id = lax.broadcasted_iota(jnp.int32, (blk, blk), 0)
    c_id = lax.broadcasted_iota(jnp.int32, (blk, blk), 1)
    neg_upper = jnp.where(r_id > c_id, -1.0, 0.0).astype(BF16)
    causal = c_id < r_id
    zscale = (dh ** -0.5) * LOG2_E
    hsel = [(lane // dh) == hh for hh in range(2)]

    def pair(qm, j, runs, diag):
        off = j * blk if isinstance(j, int) else pl.multiple_of(j * blk, blk)
        kb = k_ref[pl.ds(off, blk), :]
        vb = v_ref[pl.ds(off, blk), :]
        wgts, vms, new_runs = [], [], []
        for hh in range(2):
            zs = _dot_nt(qm[hh], kb) * zscale
            nfail = jnp.maximum(zs, 0.0) + jnp.log2(1.0 + jnp.exp2(jnp.minimum(zs, -zs)))
            if diag:
                nfail = jnp.where(causal, nfail, 0.0)
            after = _dot(nfail.astype(BF16), neg_upper) + runs[hh]
            wgt = jnp.exp2((zs - nfail) + after)
            if diag:
                wgt = jnp.where(causal, wgt, 0.0)
            wgts.append(wgt.astype(BF16))
            vms.append(jnp.where(hsel[hh], vb, jnp.zeros_like(vb)))
            new_runs.append(runs[hh] - jnp.sum(nfail, axis=1, keepdims=True))
        contrib = _dot(jnp.concatenate(wgts, axis=1), jnp.concatenate(vms, axis=0))
        return contrib, new_runs

    zero = jnp.zeros((blk, 1), F32)
    started = []
    for i in range(q_ref.shape[0] // blk):
        q2 = q_ref[i * blk:(i + 1) * blk, :]
        qm = [jnp.where(hs, q2, jnp.zeros_like(q2)) for hs in hsel]
        acc, runs = pair(qm, i, [zero, zero], True)
        if i >= 1:
            acc1, runs = pair(qm, i - 1, runs, False)
            acc = acc + acc1
        started.append((qm, acc, runs))

    for i, (qm, acc, (ra, rb)) in enumerate(started):
        if i >= 2:
            def cond(carry, i=i):
                jj, _, ra, rb = carry
                alive = jnp.maximum(jnp.max(ra), jnp.max(rb)) > SB_DEAD_LOG2
                return jnp.logical_and(jj <= i, alive)

            def body(carry, i=i, qm=qm):
                jj, acc, ra, rb = carry
                c, (ra, rb) = pair(qm, i - jj, [ra, rb], False)
                return jj + 1, acc + c, ra, rb

            _, acc, _, _ = lax.while_loop(cond, body, (jnp.int32(2), acc, ra, rb))
        o_ref[i * blk:(i + 1) * blk, :] = acc


def stick_breaking_mixer(p_sb, batch, seq, heads=SB_HEADS, blk=256):
    w = p_sb.shape[1] // 3
    dh = w // heads
    pw = 2 * dh
    assert pw == LANES and seq % blk == 0
    npair = heads // 2
    return pl.pallas_call(
        functools.partial(_sb_kernel, dh=dh, blk=blk),
        grid=(batch, npair),
        in_specs=[
            pl.BlockSpec((seq, pw), lambda b, p: (b, p)),
            pl.BlockSpec((seq, pw), lambda b, p: (b, npair + p)),
            pl.BlockSpec((seq, pw), lambda b, p: (b, 2 * npair + p)),
        ],
        out_specs=pl.BlockSpec((seq, pw), lambda b, p: (b, p)),
        out_shape=jax.ShapeDtypeStruct((batch * seq, w), F32),
        compiler_params=_cparams(("parallel", "parallel")),
        name="stick_breaking_mixer",
    )(p_sb, p_sb, p_sb)


MIX_CROSS_VMEM_MB = 56


def _mix_cross_kernel(h_ref, ya_ref, yb_ref, yc_ref, gain_ref, wout_ref, cnw_ref, wq_ref, kv_ref,
                      wo_ref, o_ref, *, heads):
    wa = ya_ref.shape[1]
    wb = yb_ref.shape[1]
    d = h_ref.shape[1]
    gain = gain_ref[...]
    ya = _rms(ya_ref[...], gain[:, :wa]).astype(BF16)
    yb = yb_ref[...].astype(BF16)
    yc = _rms(yc_ref[...], gain[:, wa + wb:]).astype(BF16)
    wout = lambda r0, r1: wout_ref[r0:r1, :].astype(BF16)
    h1 = (h_ref[...] + _dot(ya, wout(0, wa)) + _dot(yb, wout(wa, wa + wb)) + _dot(yc, wout(wa + wb, d)))
    q = _dot(_rms(h1, cnw_ref[...]).astype(BF16), wq_ref[...].astype(BF16))
    dh = d // heads
    outs = []
    for hd in range(heads):
        qh = q[:, hd * dh:(hd + 1) * dh].astype(BF16)
        kh = kv_ref[:, hd * dh:(hd + 1) * dh]
        vh = kv_ref[:, d + hd * dh:d + (hd + 1) * dh]
        s = _dot_nt(qh, kh) * (dh ** -0.5)
        s = s - jnp.max(s, axis=-1, keepdims=True)
        e = jnp.exp(s)
        p = e / jnp.sum(e, axis=-1, keepdims=True)
        outs.append(_dot(p.astype(BF16), vh).astype(BF16))
    o_ref[...] = h1 + _dot(jnp.concatenate(outs, axis=1), wo_ref[...].astype(BF16))


def mix_cross(h, ya, yb, yc, gain, w_out, cnw, wq, kv, wo, layer, batch, seq, heads=CROSS_HEADS, tm=1024):
    t, d = h.shape
    m = kv.shape[0] // batch
    nt = seq // tm
    rows = lambda wd: pl.BlockSpec((tm, wd), lambda i: (i, 0))
    const = lambda shape: pl.BlockSpec(shape, lambda i: (0, 0))
    stack = pl.BlockSpec((None, d, d), lambda i: (layer, 0, 0))
    return pl.pallas_call(
        functools.partial(_mix_cross_kernel, heads=heads),
        grid=(t // tm,),
        in_specs=[rows(d), rows(ya.shape[1]), rows(yb.shape[1]), rows(yc.shape[1]),
                  const((1, d)), stack, const((1, d)), stack,
                  pl.BlockSpec((m, 2 * d), lambda i: (i // nt, 0)),
                  stack],
        out_specs=rows(d),
        out_shape=jax.ShapeDtypeStruct((t, d), F32),
        compiler_params=_cparams(("parallel",), MIX_CROSS_VMEM_MB),
        name="mix_cross",
    )(h, ya, yb, yc, gain.reshape(1, d), w_out, cnw.reshape(1, d), wq, kv, wo)


FFN_TM = 1024
FFN_TF = 1408
FFN_CHUNK = 256
FFN_VMEM_MB = 60


def _swiglu_accumulate(x, wg_ref, wu_ref, wd_ref, o_ref):
    tf = wd_ref.shape[0]
    for c0 in range(0, tf, FFN_CHUNK):
        c1 = min(c0 + FFN_CHUNK, tf)
        wg = wg_ref[:, c0:c1].astype(BF16)
        wu = wu_ref[:, c0:c1].astype(BF16)
        act = jax.nn.silu(_dot(x, wg)) * _dot(x, wu)
        o_ref[...] += _dot(act.astype(BF16), wd_ref[c0:c1, :].astype(BF16))


def _ffn_dense_kernel(h_ref, nw_ref, wg_ref, wu_ref, wd_ref, o_ref, xn_ref):
    f = pl.program_id(1)

    @pl.when(f == 0)
    def _():
        h = h_ref[...]
        xn_ref[...] = _rms(h, nw_ref[...]).astype(BF16)
        o_ref[...] = h

    _swiglu_accumulate(xn_ref[...], wg_ref, wu_ref, wd_ref, o_ref)


def ffn_dense(h, nw, w_gu, w_down, layer, tm=FFN_TM, tf=FFN_TF):
    t, d = h.shape
    ff = w_down.shape[1]
    nf = ff // tf
    return pl.pallas_call(
        _ffn_dense_kernel,
        grid=(t // tm, nf),
        in_specs=[
            pl.BlockSpec((tm, d), lambda i, f: (i, 0)),
            pl.BlockSpec((1, d), lambda i, f: (0, 0)),
            pl.BlockSpec((None, d, tf), lambda i, f: (layer, 0, f)),
            pl.BlockSpec((None, d, tf), lambda i, f: (layer, 0, nf + f)),
            pl.BlockSpec((None, tf, d), lambda i, f: (layer, f, 0)),
        ],
        out_specs=pl.BlockSpec((tm, d), lambda i, f: (i, 0)),
        out_shape=jax.ShapeDtypeStruct((t, d), F32),
        scratch_shapes=[pltpu.VMEM((tm, d), BF16)],
        compiler_params=_cparams(("parallel", "arbitrary"), FFN_VMEM_MB),
        name="ffn_dense",
    )(h, nw.reshape(1, d), w_gu, w_gu, w_down)


def _ffn_group_kernel(te_ref, nv_ref, rows_ref, x_ref, wg_ref, wu_ref, wd_ref, y_ref, xb_ref, acc_ref):
    i = pl.program_id(0)
    f = pl.program_id(1)
    valid = i < nv_ref[0]

    @pl.when(f == 0)
    def _():
        row = lax.broadcasted_iota(jnp.int32, (x_ref.shape[0], 1), 0)
        xb_ref[...] = jnp.where(row < rows_ref[i], _unpack_bf16_pairs(x_ref[...]), 0.0).astype(BF16)
        acc_ref[...] = jnp.zeros_like(acc_ref)

    @pl.when(valid)
    def _():
        _swiglu_accumulate(xb_ref[...], wg_ref.at[0], wu_ref.at[0], wd_ref.at[0], acc_ref)

    @pl.when(f == pl.num_programs(1) - 1)
    def _():
        half = acc_ref.shape[1] // 2
        y_ref[...] = _pack_bf16_pairs(acc_ref[:, :half], acc_ref[:, half:])


def ffn_grouped(xs, tile_expert, n_valid, tile_rows, w_gu, w_down, expert_base, tm=FFN_TM, tf=FFN_TF):
    p = xs.shape[0]
    d = 2 * xs.shape[1]
    ff = w_down.shape[1]
    nf = ff // tf

    def live(i, f, nv):
        ok = i < nv[0]
        return jnp.where(ok, i, nv[0] - 1), jnp.where(ok, f, nf - 1)

    def x_map(i, f, te, nv, rows):
        ii, _ = live(i, f, nv)
        return ii, 0

    def wg_map(i, f, te, nv, rows):
        ii, ff_ = live(i, f, nv)
        return expert_base + te[ii], 0, ff_

    def wu_map(i, f, te, nv, rows):
        ii, ff_ = live(i, f, nv)
        return expert_base + te[ii], 0, nf + ff_

    def wd_map(i, f, te, nv, rows):
        ii, ff_ = live(i, f, nv)
        return expert_base + te[ii], ff_, 0

    grid_spec = pltpu.PrefetchScalarGridSpec(
        num_scalar_prefetch=3,
        grid=(p // tm, nf),
        in_specs=[
            pl.BlockSpec((tm, d // 2), x_map),
            pl.BlockSpec((1, d, tf), wg_map),
            pl.BlockSpec((1, d, tf), wu_map),
            pl.BlockSpec((1, tf, d), wd_map),
        ],
        out_specs=pl.BlockSpec((tm, d // 2), lambda i, f, te, nv, rows: (i, 0)),
        scratch_shapes=[pltpu.VMEM((tm, d), BF16), pltpu.VMEM((tm, d), F32)],
    )
    return pl.pallas_call(
        _ffn_group_kernel,
        grid_spec=grid_spec,
        out_shape=jax.ShapeDtypeStruct((p, d // 2), jnp.uint32),
        compiler_params=_cparams(("arbitrary", "arbitrary"), FFN_VMEM_MB),
        name="ffn_grouped",
    )(tile_expert, n_valid, tile_rows, xs, w_gu, w_gu, w_down)


V7X_SC_CORES = 2
V7X_SC_SUBCORES = 16
V7X_SC_WORKERS = V7X_SC_CORES * V7X_SC_SUBCORES


def _sc_mesh():
    return plsc.VectorSubcoreMesh(core_axis_name="c", subcore_axis_name="s")


def sc_scatter_rows2(x, idx1, idx2, n_out, ch=128):
    t, d = x.shape
    assert t % (V7X_SC_WORKERS * ch) == 0
    t_per_w = t // V7X_SC_WORKERS
    n_ch = t_per_w // ch

    @functools.partial(
        pl.kernel, mesh=_sc_mesh(), out_type=jax.ShapeDtypeStruct((n_out, d), x.dtype),
        scratch_types=[pltpu.VMEM((ch,), jnp.int32), pltpu.VMEM((ch,), jnp.int32),
                       pltpu.VMEM((ch, d), x.dtype), pltpu.SemaphoreType.DMA])
    def k(x_hbm, i1_hbm, i2_hbm, out_hbm, i1_v, i2_v, rows_v, sem):
        wid = lax.axis_index("s") * V7X_SC_CORES + lax.axis_index("c")
        base = wid * t_per_w

        @pl.loop(0, n_ch)
        def _(j):
            off = pl.multiple_of(base + j * ch, 8)
            pltpu.sync_copy(i1_hbm.at[pl.ds(off, ch)], i1_v)
            pltpu.sync_copy(i2_hbm.at[pl.ds(off, ch)], i2_v)
            pltpu.sync_copy(x_hbm.at[pl.ds(off, ch)], rows_v)
            pltpu.async_copy(rows_v, out_hbm.at[i1_v], sem).wait()
            pltpu.async_copy(rows_v, out_hbm.at[i2_v], sem).wait()

    return k(x, idx1, idx2)


def sc_gather_rows(table, idx, ch=128):
    _, d = table.shape
    b = idx.shape[0]
    assert b % (V7X_SC_WORKERS * ch) == 0
    b_per_w = b // V7X_SC_WORKERS
    n_ch = b_per_w // ch

    @functools.partial(
        pl.kernel, mesh=_sc_mesh(), out_type=jax.ShapeDtypeStruct((b, d), table.dtype),
        scratch_types=[pltpu.VMEM((ch,), jnp.int32), pltpu.VMEM((ch, d), table.dtype),
                       pltpu.SemaphoreType.DMA])
    def k(table_hbm, idx_hbm, out_hbm, idx_v, rows_v, sem):
        wid = lax.axis_index("s") * V7X_SC_CORES + lax.axis_index("c")
        base = wid * b_per_w

        @pl.loop(0, n_ch)
        def _(j):
            off = pl.multiple_of(base + j * ch, 8)
            pltpu.sync_copy(idx_hbm.at[pl.ds(off, ch)], idx_v)
            pltpu.async_copy(table_hbm.at[idx_v], rows_v, sem).wait()
            pltpu.sync_copy(rows_v, out_hbm.at[pl.ds(off, ch)])

    return k(table, idx)


def _moe_combine_kernel(h_ref, y1_ref, y2_ref, g1_ref, g2_ref, nw_ref, o_ref, *, normed):
    out = (h_ref[...] + g1_ref[...] * _unpack_bf16_pairs(y1_ref[...])
           + g2_ref[...] * _unpack_bf16_pairs(y2_ref[...]))
    o_ref[...] = _rms(out, nw_ref[...]) if normed else out


def moe_combine(h, yg, gate1, gate2, norm_w=None, tm=1024):
    t, d = h.shape
    nt = t // tm
    row = pl.BlockSpec((tm, d), lambda i: (i, 0))
    col = pl.BlockSpec((tm, 1), lambda i: (i, 0))
    nw = jnp.ones((d,), F32) if norm_w is None else norm_w
    return pl.pallas_call(
        functools.partial(_moe_combine_kernel, normed=norm_w is not None),
        grid=(nt,),
        in_specs=[row, pl.BlockSpec((tm, d // 2), lambda i: (i, 0)),
                  pl.BlockSpec((tm, d // 2), lambda i: (nt + i, 0)), col, col,
                  pl.BlockSpec((1, d), lambda i: (0, 0))],
        out_specs=row,
        out_shape=jax.ShapeDtypeStruct((t, d), F32),
        compiler_params=_cparams(("parallel",)),
        name="moe_combine",
    )(h, yg, yg, gate1, gate2, nw.reshape(1, d))


def _route(h, nw, wr, n_exp):
    xn = _rms(h, nw)
    half = xn.shape[1] // 2
    packed = _pack_bf16_pairs(xn[:, :half], xn[:, half:])
    xh = xn.astype(BF16)
    xl = (xn - xh.astype(F32)).astype(BF16)
    wh = wr.astype(BF16)
    wl = (wr - wh.astype(F32)).astype(BF16)
    logits = _dot(xh, wh) + _dot(xh, wl) + _dot(xl, wh)
    lane = lax.broadcasted_iota(jnp.int32, logits.shape, 1)
    neg = -jnp.inf
    lg = jnp.where(lane < n_exp, logits, neg)
    m1 = jnp.max(lg, axis=-1, keepdims=True)
    i1 = jnp.min(jnp.where(lg == m1, lane, LANES), axis=-1, keepdims=True)
    lg2 = jnp.where(lane == i1, neg, lg)
    m2 = jnp.max(lg2, axis=-1, keepdims=True)
    i2 = jnp.min(jnp.where(lg2 == m2, lane, LANES), axis=-1, keepdims=True)
    e2 = jnp.exp(m2 - m1)
    g1 = 1.0 / (1.0 + e2)
    g2 = e2 / (1.0 + e2)
    slot = jnp.where(lane == i1, 1, jnp.where(lane == i2, 2, 0))
    gate = jnp.where(lane == i1, g1, jnp.where(lane == i2, g2, 0.0))
    return packed, slot, gate


def _router_kernel(h_ref, nw_ref, wr_ref, tri_ref, xn_ref, slot_ref, gate_ref, rank_ref, cnt_ref, *, n_exp):
    @pl.when(pl.program_id(0) == 0)
    def _():
        cnt_ref[...] = jnp.zeros_like(cnt_ref)

    xn_ref[...], slot, gate = _route(h_ref[...], nw_ref[...], wr_ref[...], n_exp)
    slot_ref[...] = slot[:, :n_exp]
    gate_ref[...] = gate[:, :n_exp]
    chosen = jnp.where(slot > 0, 1.0, 0.0)
    tri = tri_ref[...]
    grp = tri.shape[0]
    seen = cnt_ref[...]
    for r0 in range(0, chosen.shape[0], grp):
        rows = chosen[r0:r0 + grp]
        before = _dot(tri, rows.astype(BF16)) + seen
        rank_ref[r0:r0 + grp, :] = before.astype(jnp.int32)[:, :n_exp]
        seen = seen + jnp.sum(rows, axis=0, keepdims=True)
    cnt_ref[...] = seen


def router(h, nw, w_router, tm=1024):
    t, d = h.shape
    n_exp = w_router.shape[1]
    wr = jnp.pad(w_router, ((0, 0), (0, LANES - n_exp)))
    tri = jnp.asarray(np.tril(np.ones((LANES, LANES), np.float32), -1), dtype=BF16)
    per_expert = pl.BlockSpec((tm, n_exp), lambda i: (i, 0))
    return pl.pallas_call(
        functools.partial(_router_kernel, n_exp=n_exp),
        grid=(t // tm,),
        in_specs=[pl.BlockSpec((tm, d), lambda i: (i, 0)),
                  pl.BlockSpec((1, d), lambda i: (0, 0)),
                  pl.BlockSpec((d, LANES), lambda i: (0, 0)),
                  pl.BlockSpec((LANES, LANES), lambda i: (0, 0))],
        out_specs=[pl.BlockSpec((tm, d // 2), lambda i: (i, 0)), per_expert, per_expert, per_expert,
                   pl.BlockSpec((1, LANES), lambda i: (0, 0))],
        out_shape=[jax.ShapeDtypeStruct((t, d // 2), jnp.uint32),
                   jax.ShapeDtypeStruct((t, n_exp), jnp.int32),
                   jax.ShapeDtypeStruct((t, n_exp), F32),
                   jax.ShapeDtypeStruct((t, n_exp), jnp.int32),
                   jax.ShapeDtypeStruct((1, LANES), F32)],
        compiler_params=_cparams(("arbitrary",)),
        name="router",
    )(h, nw.reshape(1, d), wr, tri)


def _dispatch_plan(slot, gate, rank, totals, tm):
    t, n_exp = slot.shape
    counts = totals[0, :n_exp].astype(jnp.int32)
    tiles = (counts + tm - 1) // tm
    tile_end = jnp.cumsum(tiles)
    start = (tile_end - tiles) * tm
    pos = start[None, :] + rank
    n_rows = 2 * t + n_exp * tm
    n_tiles = n_rows // tm
    pos1 = jnp.sum(jnp.where(slot == 1, pos, 0), axis=1)
    pos2 = jnp.sum(jnp.where(slot == 2, pos, 0), axis=1)
    gate1 = jnp.sum(jnp.where(slot == 1, gate, 0.0), axis=1, keepdims=True)
    gate2 = jnp.sum(jnp.where(slot == 2, gate, 0.0), axis=1, keepdims=True)
    tile_id = jnp.arange(n_tiles, dtype=jnp.int32)
    owner = tile_id[:, None] >= tile_end[None, :]
    tile_expert = jnp.minimum(jnp.sum(owner.astype(jnp.int32), axis=1), n_exp - 1)
    mine = tile_expert[:, None] == jnp.arange(n_exp, dtype=jnp.int32)[None, :]
    group_end = jnp.sum(jnp.where(mine, (start + counts)[None, :], 0), axis=1)
    n_valid = tile_end[-1:].astype(jnp.int32)
    tile_rows = jnp.where(tile_id < n_valid[0], jnp.clip(group_end - tile_id * tm, 0, tm), 0)
    return (n_rows, pos1, pos2, gate1, gate2, tile_expert.astype(jnp.int32), n_valid,
            tile_rows.astype(jnp.int32))


def _final_norm_kernel(h_ref, w_ref, o_ref):
    o_ref[...] = _rms(h_ref[...], w_ref[...])


def final_norm(h, w, tm=1024):
    t, d = h.shape
    return pl.pallas_call(
        _final_norm_kernel,
        grid=(t // tm,),
        in_specs=[pl.BlockSpec((tm, d), lambda i: (i, 0)), pl.BlockSpec((1, d), lambda i: (0, 0))],
        out_specs=pl.BlockSpec((tm, d), lambda i: (i, 0)),
        out_shape=jax.ShapeDtypeStruct((t, d), F32),
        compiler_params=_cparams(("parallel",)),
        name="final_norm",
    )(h, w.reshape(1, d))


def kernel(x, mem, mix_norm_w, w_in, ssm_lambda_re, ssm_lambda_im, ssm_b_re, ssm_b_im, ssm_c_re, ssm_c_im, ssm_d, ssm_log_dt, ssm_w_glu, mix_out_gain, w_out, cross_norm_w, mem_norm_w, w_cross_q, w_cross_kv, w_cross_o, ffn_norm_w, w_dense_gu, w_dense_down, w_router, w_expert_gu, w_expert_down, final_norm_w):
    batch, seq, d = x.shape
    depth = w_in.shape[0]
    mem_len = mem.shape[1]
    ssm_w = ssm_d.shape[1]
    g, p = ssm_lambda_re.shape[1:]
    nh = ssm_b_re.shape[-1]
    ret_w = ssm_w
    sb_w = d - ssm_w - ret_w
    ret_dh = ret_w // RET_HEADS
    moe_tm = FFN_TM

    w_in_b = jnp.concatenate(
        [w_in[..., :ssm_w],
         _group_rotary_halves(w_in[..., ssm_w:ssm_w + ret_w], RET_HEADS),
         _group_rotary_halves(w_in[..., ssm_w + ret_w:ssm_w + 2 * ret_w], RET_HEADS),
         w_in[..., ssm_w + 2 * ret_w:]], axis=-1).astype(BF16)
    n_exp = w_expert_gu.shape[1]
    w_egu = w_expert_gu.reshape((-1,) + w_expert_gu.shape[2:])
    w_edn = w_expert_down.reshape((-1,) + w_expert_down.shape[2:])
    n_a = ssm_w + 4 * ret_w

    apow, bbc = s5_prep(ssm_lambda_re, ssm_lambda_im, ssm_log_dt, ssm_b_re, ssm_b_im,
                        nstep=S5_TIME_TILE // SUBLANES)

    mem2 = mem.reshape(batch * mem_len, d)
    h = x.reshape(batch * seq, d)
    for i in range(depth):
        p_a, p_sb = norm_proj(h, mix_norm_w[i], w_in_b, i, (n_a, 3 * sb_w), (F32, BF16))
        bb = _block_diag_in(bbc[i], g, p, nh).astype(BF16)
        cc = _block_diag_out(ssm_c_re[i], ssm_c_im[i]).astype(BF16)
        y_a = s5_mixer(p_a, bb, cc, apow[i], ssm_d[i], ssm_w_glu[i].astype(BF16), batch, seq)
        gain = mix_out_gain[i]
        y_b = retention_mixer(p_a, ssm_w, gain[ssm_w:ssm_w + ret_w], batch, seq)
        y_c = stick_breaking_mixer(p_sb, batch, seq)
        (kv,) = norm_proj(mem2, mem_norm_w, w_cross_kv, i, (2 * d,), (BF16,))
        h = mix_cross(h, y_a, y_b, y_c, gain, w_out, cross_norm_w[i], w_cross_q, kv, w_cross_o, i, batch, seq)
        if i % 2 == 0:
            h = ffn_dense(h, ffn_norm_w[i], w_dense_gu, w_dense_down, i // 2)
        else:
            xn, slot, gate, rank, totals = router(h, ffn_norm_w[i], w_router[i // 2])
            n_rows, pos1, pos2, gate1, gate2, tile_expert, n_valid, tile_rows = _dispatch_plan(
                slot, gate, rank, totals, moe_tm)
            xs = sc_scatter_rows2(xn, pos1, pos2, n_rows)
            y = ffn_grouped(xs, tile_expert, n_valid, tile_rows, w_egu, w_edn, (i // 2) * n_exp,
                            tm=moe_tm)
            yg = sc_gather_rows(y, jnp.concatenate([pos1, pos2]))
            last = i == depth - 1
            h = moe_combine(h, yg, gate1, gate2, norm_w=final_norm_w if last else None)
    if depth % 2 == 1:
        h = final_norm(h, final_norm_w)
    return h.reshape(batch, seq, d)
```

```python
import functools

import numpy as np
import jax
import jax.numpy as jnp
from jax import lax
from jax.experimental import pallas as pl
from jax.experimental.pallas import tpu as pltpu
from jax.experimental.pallas import tpu_sc as plsc

F32 = jnp.float32
BF16 = jnp.bfloat16

NORM_EPS = 1e-6
GN_EPS = 1e-6
ROPE_BASE = 10000.0
LOG2_E = 1.4426950408889634
SB_DEAD_LOG2 = -150.0

RET_HEADS = 4
SB_HEADS = 8
CROSS_HEADS = 4

LANES = 128
SUBLANES = 8


def _cparams(sem, vmem_mb=48):
    return pltpu.CompilerParams(dimension_semantics=sem, vmem_limit_bytes=vmem_mb * 1024 * 1024)


def _rms(x, w):
    ms = jnp.mean(x * x, axis=-1, keepdims=True)
    return x * lax.rsqrt(ms + NORM_EPS) * w


def _dot(a, b):
    return jnp.dot(a, b, preferred_element_type=F32)


def _dot_nt(a, b):
    return lax.dot_general(a, b, (((1,), (1,)), ((), ())), preferred_element_type=F32)


def _dot_tn(a, b):
    return lax.dot_general(a, b, (((0,), (0,)), ((), ())), preferred_element_type=F32)


def _pack_bf16_pairs(a, b):
    ua = lax.bitcast_convert_type(a.astype(BF16).astype(F32), jnp.uint32)
    ub = lax.bitcast_convert_type(b.astype(BF16).astype(F32), jnp.uint32)
    return (ub & jnp.uint32(0xFFFF0000)) | (ua >> 16)


def _unpack_bf16_pairs(p):
    lo = lax.bitcast_convert_type(p << 16, F32)
    hi = lax.bitcast_convert_type(p & jnp.uint32(0xFFFF0000), F32)
    return jnp.concatenate([lo, hi], axis=1)


def _split_dot(x, m_bf16):
    hi = x.astype(BF16)
    lo = (x - hi.astype(F32)).astype(BF16)
    return _dot(hi, m_bf16) + _dot(lo, m_bf16)


def _norm_proj_kernel(x_ref, nw_ref, w_ref, *o_refs, splits, chunk):
    xn = _rms(x_ref[...], nw_ref[...]).astype(BF16)
    col = 0
    for o_ref, width in zip(o_refs, splits):
        for c0 in range(0, width, chunk):
            r = _dot(xn, w_ref[:, col + c0:col + c0 + chunk].astype(BF16))
            o_ref[:, c0:c0 + chunk] = r.astype(o_ref.dtype)
        col += width


def norm_proj(x, nw, w, layer, splits, dtypes, tm=1024, chunk=256):
    rows, d = x.shape
    n = w.shape[2]
    assert sum(splits) == n and rows % tm == 0
    return pl.pallas_call(
        functools.partial(_norm_proj_kernel, splits=splits, chunk=chunk),
        grid=(rows // tm,),
        in_specs=[
            pl.BlockSpec((tm, d), lambda i: (i, 0)),
            pl.BlockSpec((1, d), lambda i: (0, 0)),
            pl.BlockSpec((None, d, n), lambda i: (layer, 0, 0)),
        ],
        out_specs=[pl.BlockSpec((tm, s), lambda i: (i, 0)) for s in splits],
        out_shape=[jax.ShapeDtypeStruct((rows, s), dt) for s, dt in zip(splits, dtypes)],
        compiler_params=_cparams(("parallel",)),
        name="norm_proj",
    )(x, nw.reshape(1, d), w)


def _s5_prep_kernel(lr_ref, li_ref, ldt_ref, br_ref, bi_ref, apow_ref, bb_ref):
    lr = lr_ref[0]
    li = li_ref[0]
    dt = jnp.exp(ldt_ref[0])
    mag = jnp.exp(lr * dt)
    a_re = mag * jnp.cos(li * dt)
    a_im = mag * jnp.sin(li * dt)
    denom = lr * lr + li * li
    nr = a_re - 1.0
    z_re = (nr * lr + a_im * li) / denom
    z_im = (a_im * lr - nr * li) / denom
    br = br_ref[0]
    bi = bi_ref[0]
    nh = br.shape[0]
    bb_ref[0, :nh, :] = z_re * br - z_im * bi
    bb_ref[0, nh:, :] = z_re * bi + z_im * br
    nstep = (apow_ref.shape[1] - 2 * S5_SEG_LEVELS) // 2
    pr, pi = a_re, a_im
    for j in range(nstep):
        apow_ref[0, j:j + 1, :] = pr
        apow_ref[0, nstep + j:nstep + j + 1, :] = pi
        if j + 1 < nstep:
            pr, pi = pr * a_re - pi * a_im, pr * a_im + pi * a_re
    for k in range(S5_SEG_LEVELS):
        apow_ref[0, 2 * nstep + k:2 * nstep + k + 1, :] = pr
        apow_ref[0, 2 * nstep + S5_SEG_LEVELS + k:2 * nstep + S5_SEG_LEVELS + k + 1, :] = pi
        pr, pi = pr * pr - pi * pi, 2.0 * pr * pi


S5_SEG_LEVELS = 3
S5_TIME_TILE = 1024


def s5_prep(lam_re, lam_im, log_dt, b_re, b_im, nstep):
    depth, g, p = lam_re.shape
    nh = b_re.shape[-1]
    gp = g * p
    lr = lam_re.reshape(depth, 1, gp)
    li = lam_im.reshape(depth, 1, gp)
    ldt = jnp.repeat(log_dt, p, axis=1).reshape(depth, 1, gp)
    br = b_re.transpose(0, 3, 1, 2).reshape(depth, nh, gp)
    bi = b_im.transpose(0, 3, 1, 2).reshape(depth, nh, gp)
    vec = pl.BlockSpec((1, 1, gp), lambda i: (i, 0, 0))
    mat = pl.BlockSpec((1, nh, gp), lambda i: (i, 0, 0))
    npow = 2 * (nstep + S5_SEG_LEVELS)
    return pl.pallas_call(
        _s5_prep_kernel,
        grid=(depth,),
        in_specs=[vec, vec, vec, mat, mat],
        out_specs=[pl.BlockSpec((1, npow, gp), lambda i: (i, 0, 0)),
                   pl.BlockSpec((1, 2 * nh, gp), lambda i: (i, 0, 0))],
        out_shape=[jax.ShapeDtypeStruct((depth, npow, gp), F32),
                   jax.ShapeDtypeStruct((depth, 2 * nh, gp), F32)],
        compiler_params=_cparams(("arbitrary",)),
        name="s5_prep",
    )(lr, li, ldt, br, bi)


def _s5_kernel(u_ref, bb_ref, cc_ref, apow_ref, d_ref, wglu_ref, o_ref, xs_ref, carry_ref, perm_ref,
               *, gp):
    ts = u_ref.shape[0]
    nstep = ts // SUBLANES
    rows = lambda s: slice(s * SUBLANES, (s + 1) * SUBLANES)

    @pl.when(pl.program_id(1) == 0)
    def _():
        carry_ref[...] = jnp.zeros_like(carry_ref)

    ntile = u_ref.shape[1] // LANES
    for c in range(ntile):
        perm_ref[c] = u_ref[:, c * LANES:(c + 1) * LANES]
    u = jnp.concatenate(
        [jnp.concatenate([perm_ref[c, pl.ds(s, SUBLANES, stride=nstep), :] for c in range(ntile)], axis=1)
         for s in range(nstep)], axis=0)
    xs_ref[...] = _dot(u.astype(BF16), bb_ref[...])

    ar = apow_ref[0:1, :]
    ai = apow_ref[nstep:nstep + 1, :]
    xr = xs_ref[rows(0), :gp]
    xi = xs_ref[rows(0), gp:]
    for s in range(1, nstep):
        xr, xi = (ar * xr - ai * xi + xs_ref[rows(s), :gp], ar * xi + ai * xr + xs_ref[rows(s), gp:])
        xs_ref[rows(s), :gp] = xr
        xs_ref[rows(s), gp:] = xi

    seg = lax.broadcasted_iota(jnp.int32, (SUBLANES, 1), 0)
    cr = carry_ref[0:1, :]
    ci = carry_ref[1:2, :]
    lr = apow_ref[2 * nstep:2 * nstep + 1, :]
    li = apow_ref[2 * nstep + S5_SEG_LEVELS:2 * nstep + S5_SEG_LEVELS + 1, :]
    first = seg == 0
    xr = xr + jnp.where(first, lr * cr - li * ci, 0.0)
    xi = xi + jnp.where(first, lr * ci + li * cr, 0.0)
    for k in range(S5_SEG_LEVELS):
        sh = 1 << k
        pr = apow_ref[2 * nstep + k:2 * nstep + k + 1, :]
        pi = apow_ref[2 * nstep + S5_SEG_LEVELS + k:2 * nstep + S5_SEG_LEVELS + k + 1, :]
        keep = seg >= sh
        sr = jnp.where(keep, pltpu.roll(xr, sh, axis=0), 0.0)
        si = jnp.where(keep, pltpu.roll(xi, sh, axis=0), 0.0)
        xr, xi = xr + pr * sr - pi * si, xi + pr * si + pi * sr
    carry_ref[0:1, :] = xr[SUBLANES - 1:, :]
    carry_ref[1:2, :] = xi[SUBLANES - 1:, :]
    enter_r = jnp.where(first, cr, pltpu.roll(xr, 1, axis=0))
    enter_i = jnp.where(first, ci, pltpu.roll(xi, 1, axis=0))

    for s in range(nstep):
        pr = apow_ref[s:s + 1, :]
        pi = apow_ref[nstep + s:nstep + s + 1, :]
        xs_ref[rows(s), :gp] += pr * enter_r - pi * enter_i
        xs_ref[rows(s), gp:] += pr * enter_i + pi * enter_r

    y = _dot(xs_ref[...].astype(BF16), cc_ref[...]) + d_ref[...] * u
    g = jax.nn.gelu(y)
    out = g * jax.nn.sigmoid(_dot(g.astype(BF16), wglu_ref[...]))
    for s in range(nstep):
        for c in range(ntile):
            perm_ref[c, pl.ds(s, SUBLANES, stride=nstep), :] = out[rows(s), c * LANES:(c + 1) * LANES]
    for c in range(ntile):
        o_ref[:, c * LANES:(c + 1) * LANES] = perm_ref[c]


def s5_mixer(p_a, bb, cc, apow, d_skip, w_glu, batch, seq):
    w = d_skip.shape[-1]
    gp = apow.shape[-1]
    ts = (apow.shape[0] // 2 - S5_SEG_LEVELS) * SUBLANES
    nt = seq // ts
    return pl.pallas_call(
        functools.partial(_s5_kernel, gp=gp),
        grid=(batch, nt),
        in_specs=[
            pl.BlockSpec((ts, w), lambda b, t: (b * nt + t, 0)),
            pl.BlockSpec((w, 2 * gp), lambda b, t: (0, 0)),
            pl.BlockSpec((2 * gp, w), lambda b, t: (0, 0)),
            pl.BlockSpec(apow.shape, lambda b, t: (0, 0)),
            pl.BlockSpec((1, w), lambda b, t: (0, 0)),
            pl.BlockSpec((w, w), lambda b, t: (0, 0)),
        ],
        out_specs=pl.BlockSpec((ts, w), lambda b, t: (b * nt + t, 0)),
        out_shape=jax.ShapeDtypeStruct((batch * seq, w), F32),
        scratch_shapes=[pltpu.VMEM((ts, 2 * gp), F32), pltpu.VMEM((2, gp), F32),
                        pltpu.VMEM((w // LANES, ts, LANES), F32)],
        compiler_params=_cparams(("parallel", "arbitrary")),
        name="s5_mixer",
    )(p_a, bb, cc, apow, d_skip.reshape(1, w), w_glu)


def _block_diag_in(bb, g, p, nh):
    eye = jnp.eye(g, dtype=F32)
    def one(m):
        m = m.reshape(nh, g, p)
        return jnp.einsum('hgp,kg->khgp', m, eye).reshape(g * nh, g * p)
    return jnp.concatenate([one(bb[:nh]), one(bb[nh:])], axis=1)


def _block_diag_out(c_re, c_im):
    g, nh, p = c_re.shape
    eye = jnp.eye(g, dtype=F32)
    def one(m):
        return jnp.einsum('ghp,gk->gpkh', m, eye).reshape(g * p, g * nh)
    return jnp.concatenate([one(c_re), one(-c_im)], axis=0)


def _ret_tables(seq, chunk, heads, dh):
    half = dh // 2
    w = heads * dh
    hw = w // 2
    inv_freq = ROPE_BASE ** (-np.arange(half, dtype=np.float64) / half)
    ang = np.arange(seq, dtype=np.float64)[:, None] * np.tile(inv_freq, heads)[None, :]
    cos = np.cos(ang).astype(np.float32)
    sin = np.sin(ang).astype(np.float32)
    log_gamma = np.log1p(-(2.0 ** (-5.0 - np.arange(heads, dtype=np.float64))))
    idx = np.arange(chunk, dtype=np.float64)
    rel = idx[:, None] - idx[None, :]
    intra = np.where(rel >= 0, np.exp(log_gamma[:, None, None] * np.maximum(rel, 0.0)), 0.0)
    head_nat = np.arange(w) // dh
    head_rot = (np.arange(w) % hw) // half
    xi = np.exp(log_gamma[head_nat][None, :] * (idx[:, None] + 1.0))
    zeta = np.exp(log_gamma[head_rot][None, :] * (chunk - 1.0 - idx[:, None]))
    decay = np.exp(log_gamma[head_nat] * chunk)[None, :]
    same = (head_rot[:, None] == head_nat[None, :]).astype(np.float32)
    gavg = (head_nat[:, None] == head_nat[None, :]).astype(np.float32) / dh
    f = lambda a: jnp.asarray(a, dtype=F32)
    return dict(cos=f(cos), sin=f(sin), intra=f(intra), xi=f(xi), zeta=f(zeta), decay=f(decay),
                same=f(same), gavg=jnp.asarray(gavg, dtype=BF16))


def _ret_kernel(q_ref, k_ref, v_ref, g_ref, cos_ref, sin_ref, intra_ref, xi_ref, zeta_ref,
                decay_ref, same_ref, gavg_ref, gnw_ref, o_ref, state_ref, *, heads, dh):
    w = heads * dh
    hw = w // 2
    half = dh // 2

    @pl.when(pl.program_id(1) == 0)
    def _():
        state_ref[...] = jnp.zeros_like(state_ref)

    cos = cos_ref[...]
    sin = sin_ref[...]

    def rot(x):
        x1, x2 = x[:, :hw], x[:, hw:]
        return jnp.concatenate([x1 * cos - x2 * sin, x1 * sin + x2 * cos], axis=1)

    lane = lax.broadcasted_iota(jnp.int32, (1, w), 1)
    head_rot = (lane % hw) // half
    head_nat = lane // dh
    gavg = gavg_ref[...]

    for b in range(q_ref.shape[0]):
        q = rot(q_ref[b])
        k = rot(k_ref[b]) * (dh ** -0.5)
        qb = q.astype(BF16)
        kb = k.astype(BF16)
        vb = v_ref[b].astype(BF16)

        o = _dot(qb, state_ref[b].astype(BF16)) * xi_ref[...]
        for h in range(heads):
            qh = jnp.where(head_rot == h, qb, jnp.zeros_like(qb))
            vh = jnp.where(head_nat == h, vb, jnp.zeros_like(vb))
            s = _dot_nt(qh, kb) * intra_ref[h]
            o = o + _dot(s.astype(BF16), vh)
        kz = (k * zeta_ref[...]).astype(BF16)
        state_ref[b] = decay_ref[...] * state_ref[b] + _dot_tn(kz, vb) * same_ref[...]

        mu = _split_dot(o, gavg)
        d = o - mu
        var = _split_dot(d * d, gavg)
        on = d * lax.rsqrt(var + GN_EPS) * gnw_ref[...]
        o_ref[b] = jax.nn.silu(g_ref[b]) * on


def retention_mixer(p_a, col0, gn_w, batch, seq, heads=RET_HEADS, chunk=256, nb=4):
    w = gn_w.shape[-1]
    dh = w // heads
    nc = seq // chunk
    tb = _ret_tables(seq, chunk, heads, dh)
    cb = col0 // w
    assert col0 % w == 0 and batch % nb == 0
    p3 = p_a.reshape(batch, seq, p_a.shape[1])
    row = lambda j: pl.BlockSpec((nb, chunk, w), lambda b, c, j=j: (b, c, cb + j))
    const = lambda shape: pl.BlockSpec(shape, lambda b, c: (0,) * len(shape))
    return pl.pallas_call(
        functools.partial(_ret_kernel, heads=heads, dh=dh),
        grid=(batch // nb, nc),
        in_specs=[row(0), row(1), row(2), row(3),
                  pl.BlockSpec((chunk, w // 2), lambda b, c: (c, 0)),
                  pl.BlockSpec((chunk, w // 2), lambda b, c: (c, 0)),
                  const((heads, chunk, chunk)), const((chunk, w)), const((chunk, w)),
                  const((1, w)), const((w, w)), const((w, w)), const((1, w))],
        out_specs=pl.BlockSpec((nb, chunk, w), lambda b, c: (b, c, 0)),
        out_shape=jax.ShapeDtypeStruct((batch, seq, w), F32),
        scratch_shapes=[pltpu.VMEM((nb, w, w), F32)],
        compiler_params=_cparams(("parallel", "arbitrary")),
        name="retention_mixer",
    )(p3, p3, p3, p3, tb["cos"], tb["sin"], tb["intra"], tb["xi"], tb["zeta"], tb["decay"],
      tb["same"], tb["gavg"], gn_w.reshape(1, w)).reshape(batch * seq, w)


def _group_rotary_halves(w, heads):
    lead = w.shape[:-1]
    dh = w.shape[-1] // heads
    return w.reshape(lead + (heads, 2, dh // 2)).swapaxes(-3, -2).reshape(lead + (heads * dh,))


def _sb_kernel(q_ref, k_ref, v_ref, o_ref, *, dh, blk):
    lane = lax.broadcasted_iota(jnp.int32, (1, 2 * dh), 1)
    r_id = lax.broadcasted_iota(jnp.int32, (blk, blk), 0)
    c_id = lax.broadcasted_iota(jnp.int32, (blk, blk), 1)
    neg_upper = jnp.where(r_id > c_id, -1.0, 0.0).astype(BF16)
    causal = c_id < r_id
    zscale = (dh ** -0.5) * LOG2_E
    hsel = [(lane // dh) == hh for hh in range(2)]

    def pair(qm, j, runs, diag):
        off = j * blk if isinstance(j, int) else pl.multiple_of(j * blk, blk)
        kb = k_ref[pl.ds(off, blk), :]
        vb = v_ref[pl.ds(off, blk), :]
        wgts, vms, new_runs = [], [], []
        for hh in range(2):
            zs = _dot_nt(qm[hh], kb) * zscale
            nfail = jnp.maximum(zs, 0.0) + jnp.log2(1.0 + jnp.exp2(jnp.minimum(zs, -zs)))
            if diag:
                nfail = jnp.where(causal, nfail, 0.0)
            after = _dot(nfail.astype(BF16), neg_upper) + runs[hh]
            wgt = jnp.exp2((zs - nfail) + after)
            if diag:
                wgt = jnp.where(causal, wgt, 0.0)
            wgts.append(wgt.astype(BF16))
            vms.append(jnp.where(hsel[hh], vb, jnp.zeros_like(vb)))
            new_runs.append(runs[hh] - jnp.sum(nfail, axis=1, keepdims=True))
        contrib = _dot(jnp.concatenate(wgts, axis=1), jnp.concatenate(vms, axis=0))
        return contrib, new_runs

    zero = jnp.zeros((blk, 1), F32)
    started = []
    for i in range(q_ref.shape[0] // blk):
        q2 = q_ref[i * blk:(i + 1) * blk, :]
        qm = [jnp.where(hs, q2, jnp.zeros_like(q2)) for hs in hsel]
        acc, runs = pair(qm, i, [zero, zero], True)
        if i >= 1:
            acc1, runs = pair(qm, i - 1, runs, False)
            acc = acc + acc1
        started.append((qm, acc, runs))

    for i, (qm, acc, (ra, rb)) in enumerate(started):
        if i >= 2:
            def cond(carry, i=i):
                jj, _, ra, rb = carry
                alive = jnp.maximum(jnp.max(ra), jnp.max(rb)) > SB_DEAD_LOG2
                return jnp.logical_and(jj <= i, alive)

            def body(carry, i=i, qm=qm):
                jj, acc, ra, rb = carry
                c, (ra, rb) = pair(qm, i - jj, [ra, rb], False)
                return jj + 1, acc + c, ra, rb

            _, acc, _, _ = lax.while_loop(cond, body, (jnp.int32(2), acc, ra, rb))
        o_ref[i * blk:(i + 1) * blk, :] = acc


def stick_breaking_mixer(p_sb, batch, seq, heads=SB_HEADS, blk=256):
    w = p_sb.shape[1] // 3
    dh = w // heads
    pw = 2 * dh
    assert pw == LANES and seq % blk == 0
    npair = heads // 2
    return pl.pallas_call(
        functools.partial(_sb_kernel, dh=dh, blk=blk),
        grid=(batch, npair),
        in_specs=[
            pl.BlockSpec((seq, pw), lambda b, p: (b, p)),
            pl.BlockSpec((seq, pw), lambda b, p: (b, npair + p)),
            pl.BlockSpec((seq, pw), lambda b, p: (b, 2 * npair + p)),
        ],
        out_specs=pl.BlockSpec((seq, pw), lambda b, p: (b, p)),
        out_shape=jax.ShapeDtypeStruct((batch * seq, w), F32),
        compiler_params=_cparams(("parallel", "parallel")),
        name="stick_breaking_mixer",
    )(p_sb, p_sb, p_sb)


MIX_CROSS_VMEM_MB = 56


def _mix_cross_kernel(h_ref, ya_ref, yb_ref, yc_ref, gain_ref, wout_ref, cnw_ref, wq_ref, kv_ref,
                      wo_ref, o_ref, *, heads):
    wa = ya_ref.shape[1]
    wb = yb_ref.shape[1]
    d = h_ref.shape[1]
    gain = gain_ref[...]
    ya = _rms(ya_ref[...], gain[:, :wa]).astype(BF16)
    yb = yb_ref[...].astype(BF16)
    yc = _rms(yc_ref[...], gain[:, wa + wb:]).astype(BF16)
    wout = lambda r0, r1: wout_ref[r0:r1, :].astype(BF16)
    h1 = (h_ref[...] + _dot(ya, wout(0, wa)) + _dot(yb, wout(wa, wa + wb)) + _dot(yc, wout(wa + wb, d)))
    q = _dot(_rms(h1, cnw_ref[...]).astype(BF16), wq_ref[...].astype(BF16))
    dh = d // heads
    outs = []
    for hd in range(heads):
        qh = q[:, hd * dh:(hd + 1) * dh].astype(BF16)
        kh = kv_ref[:, hd * dh:(hd + 1) * dh]
        vh = kv_ref[:, d + hd * dh:d + (hd + 1) * dh]
        s = _dot_nt(qh, kh) * (dh ** -0.5)
        s = s - jnp.max(s, axis=-1, keepdims=True)
        e = jnp.exp(s)
        p = e / jnp.sum(e, axis=-1, keepdims=True)
        outs.append(_dot(p.astype(BF16), vh).astype(BF16))
    o_ref[...] = h1 + _dot(jnp.concatenate(outs, axis=1), wo_ref[...].astype(BF16))


def mix_cross(h, ya, yb, yc, gain, w_out, cnw, wq, kv, wo, layer, batch, seq, heads=CROSS_HEADS, tm=1024):
    t, d = h.shape
    m = kv.shape[0] // batch
    nt = seq // tm
    rows = lambda wd: pl.BlockSpec((tm, wd), lambda i: (i, 0))
    const = lambda shape: pl.BlockSpec(shape, lambda i: (0, 0))
    stack = pl.BlockSpec((None, d, d), lambda i: (layer, 0, 0))
    return pl.pallas_call(
        functools.partial(_mix_cross_kernel, heads=heads),
        grid=(t // tm,),
        in_specs=[rows(d), rows(ya.shape[1]), rows(yb.shape[1]), rows(yc.shape[1]),
                  const((1, d)), stack, const((1, d)), stack,
                  pl.BlockSpec((m, 2 * d), lambda i: (i // nt, 0)),
                  stack],
        out_specs=rows(d),
        out_shape=jax.ShapeDtypeStruct((t, d), F32),
        compiler_params=_cparams(("parallel",), MIX_CROSS_VMEM_MB),
        name="mix_cross",
    )(h, ya, yb, yc, gain.reshape(1, d), w_out, cnw.reshape(1, d), wq, kv, wo)


FFN_TM = 1024
FFN_TF = 1408
FFN_CHUNK = 256
FFN_VMEM_MB = 60


def _swiglu_accumulate(x, wg_ref, wu_ref, wd_ref, o_ref):
    tf = wd_ref.shape[0]
    for c0 in range(0, tf, FFN_CHUNK):
        c1 = min(c0 + FFN_CHUNK, tf)
        wg = wg_ref[:, c0:c1].astype(BF16)
        wu = wu_ref[:, c0:c1].astype(BF16)
        act = jax.nn.silu(_dot(x, wg)) * _dot(x, wu)
        o_ref[...] += _dot(act.astype(BF16), wd_ref[c0:c1, :].astype(BF16))


def _ffn_dense_kernel(h_ref, nw_ref, wg_ref, wu_ref, wd_ref, o_ref, xn_ref):
    f = pl.program_id(1)

    @pl.when(f == 0)
    def _():
        h = h_ref[...]
        xn_ref[...] = _rms(h, nw_ref[...]).astype(BF16)
        o_ref[...] = h

    _swiglu_accumulate(xn_ref[...], wg_ref, wu_ref, wd_ref, o_ref)


def ffn_dense(h, nw, w_gu, w_down, layer, tm=FFN_TM, tf=FFN_TF):
    t, d = h.shape
    ff = w_down.shape[1]
    nf = ff // tf
    return pl.pallas_call(
        _ffn_dense_kernel,
        grid=(t // tm, nf),
        in_specs=[
            pl.BlockSpec((tm, d), lambda i, f: (i, 0)),
            pl.BlockSpec((1, d), lambda i, f: (0, 0)),
            pl.BlockSpec((None, d, tf), lambda i, f: (layer, 0, f)),
            pl.BlockSpec((None, d, tf), lambda i, f: (layer, 0, nf + f)),
            pl.BlockSpec((None, tf, d), lambda i, f: (layer, f, 0)),
        ],
        out_specs=pl.BlockSpec((tm, d), lambda i, f: (i, 0)),
        out_shape=jax.ShapeDtypeStruct((t, d), F32),
        scratch_shapes=[pltpu.VMEM((tm, d), BF16)],
        compiler_params=_cparams(("parallel", "arbitrary"), FFN_VMEM_MB),
        name="ffn_dense",
    )(h, nw.reshape(1, d), w_gu, w_gu, w_down)


def _ffn_group_kernel(te_ref, nv_ref, rows_ref, x_ref, wg_ref, wu_ref, wd_ref, y_ref, xb_ref, acc_ref):
    i = pl.program_id(0)
    f = pl.program_id(1)
    valid = i < nv_ref[0]

    @pl.when(f == 0)
    def _():
        row = lax.broadcasted_iota(jnp.int32, (x_ref.shape[0], 1), 0)
        xb_ref[...] = jnp.where(row < rows_ref[i], _unpack_bf16_pairs(x_ref[...]), 0.0).astype(BF16)
        acc_ref[...] = jnp.zeros_like(acc_ref)

    @pl.when(valid)
    def _():
        _swiglu_accumulate(xb_ref[...], wg_ref.at[0], wu_ref.at[0], wd_ref.at[0], acc_ref)

    @pl.when(f == pl.num_programs(1) - 1)
    def _():
        half = acc_ref.shape[1] // 2
        y_ref[...] = _pack_bf16_pairs(acc_ref[:, :half], acc_ref[:, half:])


def ffn_grouped(xs, tile_expert, n_valid, tile_rows, w_gu, w_down, expert_base, tm=FFN_TM, tf=FFN_TF):
    p = xs.shape[0]
    d = 2 * xs.shape[1]
    ff = w_down.shape[1]
    nf = ff // tf

    def live(i, f, nv):
        ok = i < nv[0]
        return jnp.where(ok, i, nv[0] - 1), jnp.where(ok, f, nf - 1)

    def x_map(i, f, te, nv, rows):
        ii, _ = live(i, f, nv)
        return ii, 0

    def wg_map(i, f, te, nv, rows):
        ii, ff_ = live(i, f, nv)
        return expert_base + te[ii], 0, ff_

    def wu_map(i, f, te, nv, rows):
        ii, ff_ = live(i, f, nv)
        return expert_base + te[ii], 0, nf + ff_

    def wd_map(i, f, te, nv, rows):
        ii, ff_ = live(i, f, nv)
        return expert_base + te[ii], ff_, 0

    grid_spec = pltpu.PrefetchScalarGridSpec(
        num_scalar_prefetch=3,
        grid=(p // tm, nf),
        in_specs=[
            pl.BlockSpec((tm, d // 2), x_map),
            pl.BlockSpec((1, d, tf), wg_map),
            pl.BlockSpec((1, d, tf), wu_map),
            pl.BlockSpec((1, tf, d), wd_map),
        ],
        out_specs=pl.BlockSpec((tm, d // 2), lambda i, f, te, nv, rows: (i, 0)),
        scratch_shapes=[pltpu.VMEM((tm, d), BF16), pltpu.VMEM((tm, d), F32)],
    )
    return pl.pallas_call(
        _ffn_group_kernel,
        grid_spec=grid_spec,
        out_shape=jax.ShapeDtypeStruct((p, d // 2), jnp.uint32),
        compiler_params=_cparams(("arbitrary", "arbitrary"), FFN_VMEM_MB),
        name="ffn_grouped",
    )(tile_expert, n_valid, tile_rows, xs, w_gu, w_gu, w_down)


V7X_SC_CORES = 2
V7X_SC_SUBCORES = 16
V7X_SC_WORKERS = V7X_SC_CORES * V7X_SC_SUBCORES


def _sc_mesh():
    return plsc.VectorSubcoreMesh(core_axis_name="c", subcore_axis_name="s")


def sc_scatter_rows2(x, idx1, idx2, n_out, ch=128):
    t, d = x.shape
    assert t % (V7X_SC_WORKERS * ch) == 0
    t_per_w = t // V7X_SC_WORKERS
    n_ch = t_per_w // ch

    @functools.partial(
        pl.kernel, mesh=_sc_mesh(), out_type=jax.ShapeDtypeStruct((n_out, d), x.dtype),
        scratch_types=[pltpu.VMEM((ch,), jnp.int32), pltpu.VMEM((ch,), jnp.int32),
                       pltpu.VMEM((ch, d), x.dtype), pltpu.SemaphoreType.DMA])
    def k(x_hbm, i1_hbm, i2_hbm, out_hbm, i1_v, i2_v, rows_v, sem):
        wid = lax.axis_index("s") * V7X_SC_CORES + lax.axis_index("c")
        base = wid * t_per_w

        @pl.loop(0, n_ch)
        def _(j):
            off = pl.multiple_of(base + j * ch, 8)
            pltpu.sync_copy(i1_hbm.at[pl.ds(off, ch)], i1_v)
            pltpu.sync_copy(i2_hbm.at[pl.ds(off, ch)], i2_v)
            pltpu.sync_copy(x_hbm.at[pl.ds(off, ch)], rows_v)
            pltpu.async_copy(rows_v, out_hbm.at[i1_v], sem).wait()
            pltpu.async_copy(rows_v, out_hbm.at[i2_v], sem).wait()

    return k(x, idx1, idx2)


def sc_gather_rows(table, idx, ch=128):
    _, d = table.shape
    b = idx.shape[0]
    assert b % (V7X_SC_WORKERS * ch) == 0
    b_per_w = b // V7X_SC_WORKERS
    n_ch = b_per_w // ch

    @functools.partial(
        pl.kernel, mesh=_sc_mesh(), out_type=jax.ShapeDtypeStruct((b, d), table.dtype),
        scratch_types=[pltpu.VMEM((ch,), jnp.int32), pltpu.VMEM((ch, d), table.dtype),
                       pltpu.SemaphoreType.DMA])
    def k(table_hbm, idx_hbm, out_hbm, idx_v, rows_v, sem):
        wid = lax.axis_index("s") * V7X_SC_CORES + lax.axis_index("c")
        base = wid * b_per_w

        @pl.loop(0, n_ch)
        def _(j):
            off = pl.multiple_of(base + j * ch, 8)
            pltpu.sync_copy(idx_hbm.at[pl.ds(off, ch)], idx_v)
            pltpu.async_copy(table_hbm.at[idx_v], rows_v, sem).wait()
            pltpu.sync_copy(rows_v, out_hbm.at[pl.ds(off, ch)])

    return k(table, idx)


def _moe_combine_kernel(h_ref, y1_ref, y2_ref, g1_ref, g2_ref, nw_ref, o_ref, *, normed):
    out = (h_ref[...] + g1_ref[...] * _unpack_bf16_pairs(y1_ref[...])
           + g2_ref[...] * _unpack_bf16_pairs(y2_ref[...]))
    o_ref[...] = _rms(out, nw_ref[...]) if normed else out


def moe_combine(h, yg, gate1, gate2, norm_w=None, tm=1024):
    t, d = h.shape
    nt = t // tm
    row = pl.BlockSpec((tm, d), lambda i: (i, 0))
    col = pl.BlockSpec((tm, 1), lambda i: (i, 0))
    nw = jnp.ones((d,), F32) if norm_w is None else norm_w
    return pl.pallas_call(
        functools.partial(_moe_combine_kernel, normed=norm_w is not None),
        grid=(nt,),
        in_specs=[row, pl.BlockSpec((tm, d // 2), lambda i: (i, 0)),
                  pl.BlockSpec((tm, d // 2), lambda i: (nt + i, 0)), col, col,
                  pl.BlockSpec((1, d), lambda i: (0, 0))],
        out_specs=row,
        out_shape=jax.ShapeDtypeStruct((t, d), F32),
        compiler_params=_cparams(("parallel",)),
        name="moe_combine",
    )(h, yg, yg, gate1, gate2, nw.reshape(1, d))


def _route(h, nw, wr, n_exp):
    xn = _rms(h, nw)
    half = xn.shape[1] // 2
    packed = _pack_bf16_pairs(xn[:, :half], xn[:, half:])
    xh = xn.astype(BF16)
    xl = (xn - xh.astype(F32)).astype(BF16)
    wh = wr.astype(BF16)
    wl = (wr - wh.astype(F32)).astype(BF16)
    logits = _dot(xh, wh) + _dot(xh, wl) + _dot(xl, wh)
    lane = lax.broadcasted_iota(jnp.int32, logits.shape, 1)
    neg = -jnp.inf
    lg = jnp.where(lane < n_exp, logits, neg)
    m1 = jnp.max(lg, axis=-1, keepdims=True)
    i1 = jnp.min(jnp.where(lg == m1, lane, LANES), axis=-1, keepdims=True)
    lg2 = jnp.where(lane == i1, neg, lg)
    m2 = jnp.max(lg2, axis=-1, keepdims=True)
    i2 = jnp.min(jnp.where(lg2 == m2, lane, LANES), axis=-1, keepdims=True)
    e2 = jnp.exp(m2 - m1)
    g1 = 1.0 / (1.0 + e2)
    g2 = e2 / (1.0 + e2)
    slot = jnp.where(lane == i1, 1, jnp.where(lane == i2, 2, 0))
    gate = jnp.where(lane == i1, g1, jnp.where(lane == i2, g2, 0.0))
    return packed, slot, gate


def _router_kernel(h_ref, nw_ref, wr_ref, tri_ref, xn_ref, slot_ref, gate_ref, rank_ref, cnt_ref, *, n_exp):
    @pl.when(pl.program_id(0) == 0)
    def _():
        cnt_ref[...] = jnp.zeros_like(cnt_ref)

    xn_ref[...], slot, gate = _route(h_ref[...], nw_ref[...], wr_ref[...], n_exp)
    slot_ref[...] = slot[:, :n_exp]
    gate_ref[...] = gate[:, :n_exp]
    chosen = jnp.where(slot > 0, 1.0, 0.0)
    tri = tri_ref[...]
    grp = tri.shape[0]
    seen = cnt_ref[...]
    for r0 in range(0, chosen.shape[0], grp):
        rows = chosen[r0:r0 + grp]
        before = _dot(tri, rows.astype(BF16)) + seen
        rank_ref[r0:r0 + grp, :] = before.astype(jnp.int32)[:, :n_exp]
        seen = seen + jnp.sum(rows, axis=0, keepdims=True)
    cnt_ref[...] = seen


def router(h, nw, w_router, tm=1024):
    t, d = h.shape
    n_exp = w_router.shape[1]
    wr = jnp.pad(w_router, ((0, 0), (0, LANES - n_exp)))
    tri = jnp.asarray(np.tril(np.ones((LANES, LANES), np.float32), -1), dtype=BF16)
    per_expert = pl.BlockSpec((tm, n_exp), lambda i: (i, 0))
    return pl.pallas_call(
        functools.partial(_router_kernel, n_exp=n_exp),
        grid=(t // tm,),
        in_specs=[pl.BlockSpec((tm, d), lambda i: (i, 0)),
                  pl.BlockSpec((1, d), lambda i: (0, 0)),
                  pl.BlockSpec((d, LANES), lambda i: (0, 0)),
                  pl.BlockSpec((LANES, LANES), lambda i: (0, 0))],
        out_specs=[pl.BlockSpec((tm, d // 2), lambda i: (i, 0)), per_expert, per_expert, per_expert,
                   pl.BlockSpec((1, LANES), lambda i: (0, 0))],
        out_shape=[jax.ShapeDtypeStruct((t, d // 2), jnp.uint32),
                   jax.ShapeDtypeStruct((t, n_exp), jnp.int32),
                   jax.ShapeDtypeStruct((t, n_exp), F32),
                   jax.ShapeDtypeStruct((t, n_exp), jnp.int32),
                   jax.ShapeDtypeStruct((1, LANES), F32)],
        compiler_params=_cparams(("arbitrary",)),
        name="router",
    )(h, nw.reshape(1, d), wr, tri)


def _dispatch_plan(slot, gate, rank, totals, tm):
    t, n_exp = slot.shape
    counts = totals[0, :n_exp].astype(jnp.int32)
    tiles = (counts + tm - 1) // tm
    tile_end = jnp.cumsum(tiles)
    start = (tile_end - tiles) * tm
    pos = start[None, :] + rank
    n_rows = 2 * t + n_exp * tm
    n_tiles = n_rows // tm
    pos1 = jnp.sum(jnp.where(slot == 1, pos, 0), axis=1)
    pos2 = jnp.sum(jnp.where(slot == 2, pos, 0), axis=1)
    gate1 = jnp.sum(jnp.where(slot == 1, gate, 0.0), axis=1, keepdims=True)
    gate2 = jnp.sum(jnp.where(slot == 2, gate, 0.0), axis=1, keepdims=True)
    tile_id = jnp.arange(n_tiles, dtype=jnp.int32)
    owner = tile_id[:, None] >= tile_end[None, :]
    tile_expert = jnp.minimum(jnp.sum(owner.astype(jnp.int32), axis=1), n_exp - 1)
    mine = tile_expert[:, None] == jnp.arange(n_exp, dtype=jnp.int32)[None, :]
    group_end = jnp.sum(jnp.where(mine, (start + counts)[None, :], 0), axis=1)
    n_valid = tile_end[-1:].astype(jnp.int32)
    tile_rows = jnp.where(tile_id < n_valid[0], jnp.clip(group_end - tile_id * tm, 0, tm), 0)
    return (n_rows, pos1, pos2, gate1, gate2, tile_expert.astype(jnp.int32), n_valid,
            tile_rows.astype(jnp.int32))


def _final_norm_kernel(h_ref, w_ref, o_ref):
    o_ref[...] = _rms(h_ref[...], w_ref[...])


def final_norm(h, w, tm=1024):
    t, d = h.shape
    return pl.pallas_call(
        _final_norm_kernel,
        grid=(t // tm,),
        in_specs=[pl.BlockSpec((tm, d), lambda i: (i, 0)), pl.BlockSpec((1, d), lambda i: (0, 0))],
        out_specs=pl.BlockSpec((tm, d), lambda i: (i, 0)),
        out_shape=jax.ShapeDtypeStruct((t, d), F32),
        compiler_params=_cparams(("parallel",)),
        name="final_norm",
    )(h, w.reshape(1, d))


def kernel(x, mem, mix_norm_w, w_in, ssm_lambda_re, ssm_lambda_im, ssm_b_re, ssm_b_im, ssm_c_re, ssm_c_im, ssm_d, ssm_log_dt, ssm_w_glu, mix_out_gain, w_out, cross_norm_w, mem_norm_w, w_cross_q, w_cross_kv, w_cross_o, ffn_norm_w, w_dense_gu, w_dense_down, w_router, w_expert_gu, w_expert_down, final_norm_w):
    batch, seq, d = x.shape
    depth = w_in.shape[0]
    mem_len = mem.shape[1]
    ssm_w = ssm_d.shape[1]
    g, p = ssm_lambda_re.shape[1:]
    nh = ssm_b_re.shape[-1]
    ret_w = ssm_w
    sb_w = d - ssm_w - ret_w
    ret_dh = ret_w // RET_HEADS
    moe_tm = FFN_TM

    w_in_b = jnp.concatenate(
        [w_in[..., :ssm_w],
         _group_rotary_halves(w_in[..., ssm_w:ssm_w + ret_w], RET_HEADS),
         _group_rotary_halves(w_in[..., ssm_w + ret_w:ssm_w + 2 * ret_w], RET_HEADS),
         w_in[..., ssm_w + 2 * ret_w:]], axis=-1).astype(BF16)
    n_exp = w_expert_gu.shape[1]
    w_egu = w_expert_gu.reshape((-1,) + w_expert_gu.shape[2:])
    w_edn = w_expert_down.reshape((-1,) + w_expert_down.shape[2:])
    n_a = ssm_w + 4 * ret_w

    apow, bbc = s5_prep(ssm_lambda_re, ssm_lambda_im, ssm_log_dt, ssm_b_re, ssm_b_im,
                        nstep=S5_TIME_TILE // SUBLANES)

    mem2 = mem.reshape(batch * mem_len, d)
    h = x.reshape(batch * seq, d)
    for i in range(depth):
        p_a, p_sb = norm_proj(h, mix_norm_w[i], w_in_b, i, (n_a, 3 * sb_w), (F32, BF16))
        bb = _block_diag_in(bbc[i], g, p, nh).astype(BF16)
        cc = _block_diag_out(ssm_c_re[i], ssm_c_im[i]).astype(BF16)
        y_a = s5_mixer(p_a, bb, cc, apow[i], ssm_d[i], ssm_w_glu[i].astype(BF16), batch, seq)
        gain = mix_out_gain[i]
        y_b = retention_mixer(p_a, ssm_w, gain[ssm_w:ssm_w + ret_w], batch, seq)
        y_c = stick_breaking_mixer(p_sb, batch, seq)
        (kv,) = norm_proj(mem2, mem_norm_w, w_cross_kv, i, (2 * d,), (BF16,))
        h = mix_cross(h, y_a, y_b, y_c, gain, w_out, cross_norm_w[i], w_cross_q, kv, w_cross_o, i, batch, seq)
        if i % 2 == 0:
            h = ffn_dense(h, ffn_norm_w[i], w_dense_gu, w_dense_down, i // 2)
        else:
            xn, slot, gate, rank, totals = router(h, ffn_norm_w[i], w_router[i // 2])
            n_rows, pos1, pos2, gate1, gate2, tile_expert, n_valid, tile_rows = _dispatch_plan(
                slot, gate, rank, totals, moe_tm)
            xs = sc_scatter_rows2(xn, pos1, pos2, n_rows)
            y = ffn_grouped(xs, tile_expert, n_valid, tile_rows, w_egu, w_edn, (i // 2) * n_exp,
                            tm=moe_tm)
            yg = sc_gather_rows(y, jnp.concatenate([pos1, pos2]))
            last = i == depth - 1
            h = moe_combine(h, yg, gate1, gate2, norm_w=final_norm_w if last else None)
    if depth % 2 == 1:
        h = final_norm(h, final_norm_w)
    return h.reshape(batch, seq, d)
```

```python
import functools

import numpy as np
import jax
import jax.numpy as jnp
from jax import lax
from jax.experimental import pallas as pl
from jax.experimental.pallas import tpu as pltpu
from jax.experimental.pallas import tpu_sc as plsc

F32 = jnp.float32
BF16 = jnp.bfloat16

NORM_EPS = 1e-6
GN_EPS = 1e-6
ROPE_BASE = 10000.0
LOG2_E = 1.4426950408889634
SB_DEAD_LOG2 = -150.0

RET_HEADS = 4
SB_HEADS = 8
CROSS_HEADS = 4

LANES = 128
SUBLANES = 8


def _cparams(sem, vmem_mb=48):
    return pltpu.CompilerParams(dimension_semantics=sem, vmem_limit_bytes=vmem_mb * 1024 * 1024)


def _rms(x, w):
    ms = jnp.mean(x * x, axis=-1, keepdims=True)
    return x * lax.rsqrt(ms + NORM_EPS) * w


def _dot(a, b):
    return jnp.dot(a, b, preferred_element_type=F32)


def _dot_nt(a, b):
    return lax.dot_general(a, b, (((1,), (1,)), ((), ())), preferred_element_type=F32)


def _dot_tn(a, b):
    return lax.dot_general(a, b, (((0,), (0,)), ((), ())), preferred_element_type=F32)


def _pack_bf16_pairs(a, b):
    ua = lax.bitcast_convert_type(a.astype(BF16).astype(F32), jnp.uint32)
    ub = lax.bitcast_convert_type(b.astype(BF16).astype(F32), jnp.uint32)
    return (ub & jnp.uint32(0xFFFF0000)) | (ua >> 16)


def _unpack_bf16_pairs(p):
    lo = lax.bitcast_convert_type(p << 16, F32)
    hi = lax.bitcast_convert_type(p & jnp.uint32(0xFFFF0000), F32)
    return jnp.concatenate([lo, hi], axis=1)


def _split_dot(x, m_bf16):
    hi = x.astype(BF16)
    lo = (x - hi.astype(F32)).astype(BF16)
    return _dot(hi, m_bf16) + _dot(lo, m_bf16)


def _norm_proj_kernel(x_ref, nw_ref, w_ref, *o_refs, splits, chunk):
    xn = _rms(x_ref[...], nw_ref[...]).astype(BF16)
    col = 0
    for o_ref, width in zip(o_refs, splits):
        for c0 in range(0, width, chunk):
            r = _dot(xn, w_ref[:, col + c0:col + c0 + chunk].astype(BF16))
            o_ref[:, c0:c0 + chunk] = r.astype(o_ref.dtype)
        col += width


def norm_proj(x, nw, w, layer, splits, dtypes, tm=1024, chunk=256):
    rows, d = x.shape
    n = w.shape[2]
    assert sum(splits) == n and rows % tm == 0
    return pl.pallas_call(
        functools.partial(_norm_proj_kernel, splits=splits, chunk=chunk),
        grid=(rows // tm,),
        in_specs=[
            pl.BlockSpec((tm, d), lambda i: (i, 0)),
            pl.BlockSpec((1, d), lambda i: (0, 0)),
            pl.BlockSpec((None, d, n), lambda i: (layer, 0, 0)),
        ],
        out_specs=[pl.BlockSpec((tm, s), lambda i: (i, 0)) for s in splits],
        out_shape=[jax.ShapeDtypeStruct((rows, s), dt) for s, dt in zip(splits, dtypes)],
        compiler_params=_cparams(("parallel",)),
        name="norm_proj",
    )(x, nw.reshape(1, d), w)


def _s5_prep_kernel(lr_ref, li_ref, ldt_ref, br_ref, bi_ref, apow_ref, bb_ref):
    lr = lr_ref[0]
    li = li_ref[0]
    dt = jnp.exp(ldt_ref[0])
    mag = jnp.exp(lr * dt)
    a_re = mag * jnp.cos(li * dt)
    a_im = mag * jnp.sin(li * dt)
    denom = lr * lr + li * li
    nr = a_re - 1.0
    z_re = (nr * lr + a_im * li) / denom
    z_im = (a_im * lr - nr * li) / denom
    br = br_ref[0]
    bi = bi_ref[0]
    nh = br.shape[0]
    bb_ref[0, :nh, :] = z_re * br - z_im * bi
    bb_ref[0, nh:, :] = z_re * bi + z_im * br
    nstep = (apow_ref.shape[1] - 2 * S5_SEG_LEVELS) // 2
    pr, pi = a_re, a_im
    for j in range(nstep):
        apow_ref[0, j:j + 1, :] = pr
        apow_ref[0, nstep + j:nstep + j + 1, :] = pi
        if j + 1 < nstep:
            pr, pi = pr * a_re - pi * a_im, pr * a_im + pi * a_re
    for k in range(S5_SEG_LEVELS):
        apow_ref[0, 2 * nstep + k:2 * nstep + k + 1, :] = pr
        apow_ref[0, 2 * nstep + S5_SEG_LEVELS + k:2 * nstep + S5_SEG_LEVELS + k + 1, :] = pi
        pr, pi = pr * pr - pi * pi, 2.0 * pr * pi


S5_SEG_LEVELS = 3
S5_TIME_TILE = 1024


def s5_prep(lam_re, lam_im, log_dt, b_re, b_im, nstep):
    depth, g, p = lam_re.shape
    nh = b_re.shape[-1]
    gp = g * p
    lr = lam_re.reshape(depth, 1, gp)
    li = lam_im.reshape(depth, 1, gp)
    ldt = jnp.repeat(log_dt, p, axis=1).reshape(depth, 1, gp)
    br = b_re.transpose(0, 3, 1, 2).reshape(depth, nh, gp)
    bi = b_im.transpose(0, 3, 1, 2).reshape(depth, nh, gp)
    vec = pl.BlockSpec((1, 1, gp), lambda i: (i, 0, 0))
    mat = pl.BlockSpec((1, nh, gp), lambda i: (i, 0, 0))
    npow = 2 * (nstep + S5_SEG_LEVELS)
    return pl.pallas_call(
        _s5_prep_kernel,
        grid=(depth,),
        in_specs=[vec, vec, vec, mat, mat],
        out_specs=[pl.BlockSpec((1, npow, gp), lambda i: (i, 0, 0)),
                   pl.BlockSpec((1, 2 * nh, gp), lambda i: (i, 0, 0))],
        out_shape=[jax.ShapeDtypeStruct((depth, npow, gp), F32),
                   jax.ShapeDtypeStruct((depth, 2 * nh, gp), F32)],
        compiler_params=_cparams(("arbitrary",)),
        name="s5_prep",
    )(lr, li, ldt, br, bi)


def _s5_kernel(u_ref, bb_ref, cc_ref, apow_ref, d_ref, wglu_ref, o_ref, xs_ref, carry_ref, perm_ref,
               *, gp):
    ts = u_ref.shape[0]
    nstep = ts // SUBLANES
    rows = lambda s: slice(s * SUBLANES, (s + 1) * SUBLANES)

    @pl.when(pl.program_id(1) == 0)
    def _():
        carry_ref[...] = jnp.zeros_like(carry_ref)

    ntile = u_ref.shape[1] // LANES
    for c in range(ntile):
        perm_ref[c] = u_ref[:, c * LANES:(c + 1) * LANES]
    u = jnp.concatenate(
        [jnp.concatenate([perm_ref[c, pl.ds(s, SUBLANES, stride=nstep), :] for c in range(ntile)], axis=1)
         for s in range(nstep)], axis=0)
    xs_ref[...] = _dot(u.astype(BF16), bb_ref[...])

    ar = apow_ref[0:1, :]
    ai = apow_ref[nstep:nstep + 1, :]
    xr = xs_ref[rows(0), :gp]
    xi = xs_ref[rows(0), gp:]
    for s in range(1, nstep):
        xr, xi = (ar * xr - ai * xi + xs_ref[rows(s), :gp], ar * xi + ai * xr + xs_ref[rows(s), gp:])
        xs_ref[rows(s), :gp] = xr
        xs_ref[rows(s), gp:] = xi

    seg = lax.broadcasted_iota(jnp.int32, (SUBLANES, 1), 0)
    cr = carry_ref[0:1, :]
    ci = carry_ref[1:2, :]
    lr = apow_ref[2 * nstep:2 * nstep + 1, :]
    li = apow_ref[2 * nstep + S5_SEG_LEVELS:2 * nstep + S5_SEG_LEVELS + 1, :]
    first = seg == 0
    xr = xr + jnp.where(first, lr * cr - li * ci, 0.0)
    xi = xi + jnp.where(first, lr * ci + li * cr, 0.0)
    for k in range(S5_SEG_LEVELS):
        sh = 1 << k
        pr = apow_ref[2 * nstep + k:2 * nstep + k + 1, :]
        pi = apow_ref[2 * nstep + S5_SEG_LEVELS + k:2 * nstep + S5_SEG_LEVELS + k + 1, :]
        keep = seg >= sh
        sr = jnp.where(keep, pltpu.roll(xr, sh, axis=0), 0.0)
        si = jnp.where(keep, pltpu.roll(xi, sh, axis=0), 0.0)
        xr, xi = xr + pr * sr - pi * si, xi + pr * si + pi * sr
    carry_ref[0:1, :] = xr[SUBLANES - 1:, :]
    carry_ref[1:2, :] = xi[SUBLANES - 1:, :]
    enter_r = jnp.where(first, cr, pltpu.roll(xr, 1, axis=0))
    enter_i = jnp.where(first, ci, pltpu.roll(xi, 1, axis=0))

    for s in range(nstep):
        pr = apow_ref[s:s + 1, :]
        pi = apow_ref[nstep + s:nstep + s + 1, :]
        xs_ref[rows(s), :gp] += pr * enter_r - pi * enter_i
        xs_ref[rows(s), gp:] += pr * enter_i + pi * enter_r

    y = _dot(xs_ref[...].astype(BF16), cc_ref[...]) + d_ref[...] * u
    g = jax.nn.gelu(y)
    out = g * jax.nn.sigmoid(_dot(g.astype(BF16), wglu_ref[...]))
    for s in range(nstep):
        for c in range(ntile):
            perm_ref[c, pl.ds(s, SUBLANES, stride=nstep), :] = out[rows(s), c * LANES:(c + 1) * LANES]
    for c in range(ntile):
        o_ref[:, c * LANES:(c + 1) * LANES] = perm_ref[c]


def s5_mixer(p_a, bb, cc, apow, d_skip, w_glu, batch, seq):
    w = d_skip.shape[-1]
    gp = apow.shape[-1]
    ts = (apow.shape[0] // 2 - S5_SEG_LEVELS) * SUBLANES
    nt = seq // ts
    return pl.pallas_call(
        functools.partial(_s5_kernel, gp=gp),
        grid=(batch, nt),
        in_specs=[
            pl.BlockSpec((ts, w), lambda b, t: (b * nt + t, 0)),
            pl.BlockSpec((w, 2 * gp), lambda b, t: (0, 0)),
            pl.BlockSpec((2 * gp, w), lambda b, t: (0, 0)),
            pl.BlockSpec(apow.shape, lambda b, t: (0, 0)),
            pl.BlockSpec((1, w), lambda b, t: (0, 0)),
            pl.BlockSpec((w, w), lambda b, t: (0, 0)),
        ],
        out_specs=pl.BlockSpec((ts, w), lambda b, t: (b * nt + t, 0)),
        out_shape=jax.ShapeDtypeStruct((batch * seq, w), F32),
        scratch_shapes=[pltpu.VMEM((ts, 2 * gp), F32), pltpu.VMEM((2, gp), F32),
                        pltpu.VMEM((w // LANES, ts, LANES), F32)],
        compiler_params=_cparams(("parallel", "arbitrary")),
        name="s5_mixer",
    )(p_a, bb, cc, apow, d_skip.reshape(1, w), w_glu)


def _block_diag_in(bb, g, p, nh):
    eye = jnp.eye(g, dtype=F32)
    def one(m):
        m = m.reshape(nh, g, p)
        return jnp.einsum('hgp,kg->khgp', m, eye).reshape(g * nh, g * p)
    return jnp.concatenate([one(bb[:nh]), one(bb[nh:])], axis=1)


def _block_diag_out(c_re, c_im):
    g, nh, p = c_re.shape
    eye = jnp.eye(g, dtype=F32)
    def one(m):
        return jnp.einsum('ghp,gk->gpkh', m, eye).reshape(g * p, g * nh)
    return jnp.concatenate([one(c_re), one(-c_im)], axis=0)


def _ret_tables(seq, chunk, heads, dh):
    half = dh // 2
    w = heads * dh
    hw = w // 2
    inv_freq = ROPE_BASE ** (-np.arange(half, dtype=np.float64) / half)
    ang = np.arange(seq, dtype=np.float64)[:, None] * np.tile(inv_freq, heads)[None, :]
    cos = np.cos(ang).astype(np.float32)
    sin = np.sin(ang).astype(np.float32)
    log_gamma = np.log1p(-(2.0 ** (-5.0 - np.arange(heads, dtype=np.float64))))
    idx = np.arange(chunk, dtype=np.float64)
    rel = idx[:, None] - idx[None, :]
    intra = np.where(rel >= 0, np.exp(log_gamma[:, None, None] * np.maximum(rel, 0.0)), 0.0)
    head_nat = np.arange(w) // dh
    head_rot = (np.arange(w) % hw) // half
    xi = np.exp(log_gamma[head_nat][None, :] * (idx[:, None] + 1.0))
    zeta = np.exp(log_gamma[head_rot][None, :] * (chunk - 1.0 - idx[:, None]))
    decay = np.exp(log_gamma[head_nat] * chunk)[None, :]
    same = (head_rot[:, None] == head_nat[None, :]).astype(np.float32)
    gavg = (head_nat[:, None] == head_nat[None, :]).astype(np.float32) / dh
    f = lambda a: jnp.asarray(a, dtype=F32)
    return dict(cos=f(cos), sin=f(sin), intra=f(intra), xi=f(xi), zeta=f(zeta), decay=f(decay),
                same=f(same), gavg=jnp.asarray(gavg, dtype=BF16))


def _ret_kernel(q_ref, k_ref, v_ref, g_ref, cos_ref, sin_ref, intra_ref, xi_ref, zeta_ref,
                decay_ref, same_ref, gavg_ref, gnw_ref, o_ref, state_ref, *, heads, dh):
    w = heads * dh
    hw = w // 2
    half = dh // 2

    @pl.when(pl.program_id(1) == 0)
    def _():
        state_ref[...] = jnp.zeros_like(state_ref)

    cos = cos_ref[...]
    sin = sin_ref[...]

    def rot(x):
        x1, x2 = x[:, :hw], x[:, hw:]
        return jnp.concatenate([x1 * cos - x2 * sin, x1 * sin + x2 * cos], axis=1)

    lane = lax.broadcasted_iota(jnp.int32, (1, w), 1)
    head_rot = (lane % hw) // half
    head_nat = lane // dh
    gavg = gavg_ref[...]

    for b in range(q_ref.shape[0]):
        q = rot(q_ref[b])
        k = rot(k_ref[b]) * (dh ** -0.5)
        qb = q.astype(BF16)
        kb = k.astype(BF16)
        vb = v_ref[b].astype(BF16)

        o = _dot(qb, state_ref[b].astype(BF16)) * xi_ref[...]
        for h in range(heads):
            qh = jnp.where(head_rot == h, qb, jnp.zeros_like(qb))
            vh = jnp.where(head_nat == h, vb, jnp.zeros_like(vb))
            s = _dot_nt(qh, kb) * intra_ref[h]
            o = o + _dot(s.astype(BF16), vh)
        kz = (k * zeta_ref[...]).astype(BF16)
        state_ref[b] = decay_ref[...] * state_ref[b] + _dot_tn(kz, vb) * same_ref[...]

        mu = _split_dot(o, gavg)
        d = o - mu
        var = _split_dot(d * d, gavg)
        on = d * lax.rsqrt(var + GN_EPS) * gnw_ref[...]
        o_ref[b] = jax.nn.silu(g_ref[b]) * on


def retention_mixer(p_a, col0, gn_w, batch, seq, heads=RET_HEADS, chunk=256, nb=4):
    w = gn_w.shape[-1]
    dh = w // heads
    nc = seq // chunk
    tb = _ret_tables(seq, chunk, heads, dh)
    cb = col0 // w
    assert col0 % w == 0 and batch % nb == 0
    p3 = p_a.reshape(batch, seq, p_a.shape[1])
    row = lambda j: pl.BlockSpec((nb, chunk, w), lambda b, c, j=j: (b, c, cb + j))
    const = lambda shape: pl.BlockSpec(shape, lambda b, c: (0,) * len(shape))
    return pl.pallas_call(
        functools.partial(_ret_kernel, heads=heads, dh=dh),
        grid=(batch // nb, nc),
        in_specs=[row(0), row(1), row(2), row(3),
                  pl.BlockSpec((chunk, w // 2), lambda b, c: (c, 0)),
                  pl.BlockSpec((chunk, w // 2), lambda b, c: (c, 0)),
                  const((heads, chunk, chunk)), const((chunk, w)), const((chunk, w)),
                  const((1, w)), const((w, w)), const((w, w)), const((1, w))],
        out_specs=pl.BlockSpec((nb, chunk, w), lambda b, c: (b, c, 0)),
        out_shape=jax.ShapeDtypeStruct((batch, seq, w), F32),
        scratch_shapes=[pltpu.VMEM((nb, w, w), F32)],
        compiler_params=_cparams(("parallel", "arbitrary")),
        name="retention_mixer",
    )(p3, p3, p3, p3, tb["cos"], tb["sin"], tb["intra"], tb["xi"], tb["zeta"], tb["decay"],
      tb["same"], tb["gavg"], gn_w.reshape(1, w)).reshape(batch * seq, w)


def _group_rotary_halves(w, heads):
    lead = w.shape[:-1]
    dh = w.shape[-1] // heads
    return w.reshape(lead + (heads, 2, dh // 2)).swapaxes(-3, -2).reshape(lead + (heads * dh,))


def _sb_kernel(q_ref, k_ref, v_ref, o_ref, *, dh, blk):
    lane = lax.broadcasted_iota(jnp.int32, (1, 2 * dh), 1)
    r_id = lax.broadcasted_iota(jnp.int32, (blk, blk), 0)
    c_id = lax.broadcasted_iota(jnp.int32, (blk, blk), 1)
    neg_upper = jnp.where(r_id > c_id, -1.0, 0.0).astype(BF16)
    causal = c_id < r_id
    zscale = (dh ** -0.5) * LOG2_E
    hsel = [(lane // dh) == hh for hh in range(2)]

    def pair(qm, j, runs, diag):
        off = j * blk if isinstance(j, int) else pl.multiple_of(j * blk, blk)
        kb = k_ref[pl.ds(off, blk), :]
        vb = v_ref[pl.ds(off, blk), :]
        wgts, vms, new_runs = [], [], []
        for hh in range(2):
            zs = _dot_nt(qm[hh], kb) * zscale
            nfail = jnp.maximum(zs, 0.0) + jnp.log2(1.0 + jnp.exp2(jnp.minimum(zs, -zs)))
            if diag:
                nfail = jnp.where(causal, nfail, 0.0)
            after = _dot(nfail.astype(BF16), neg_upper) + runs[hh]
            wgt = jnp.exp2((zs - nfail) + after)
            if diag:
                wgt = jnp.where(causal, wgt, 0.0)
            wgts.append(wgt.astype(BF16))
            vms.append(jnp.where(hsel[hh], vb, jnp.zeros_like(vb)))
            new_runs.append(runs[hh] - jnp.sum(nfail, axis=1, keepdims=True))
        contrib = _dot(jnp.concatenate(wgts, axis=1), jnp.concatenate(vms, axis=0))
        return contrib, new_runs

    zero = jnp.zeros((blk, 1), F32)
    started = []
    for i in range(q_ref.shape[0] // blk):
        q2 = q_ref[i * blk:(i + 1) * blk, :]
        qm = [jnp.where(hs, q2, jnp.zeros_like(q2)) for hs in hsel]
        acc, runs = pair(qm, i, [zero, zero], True)
        if i >= 1:
            acc1, runs = pair(qm, i - 1, runs, False)
            acc = acc + acc1
        started.append((qm, acc, runs))

    for i, (qm, acc, (ra, rb)) in enumerate(started):
        if i >= 2:
            def cond(carry, i=i):
                jj, _, ra, rb = carry
                alive = jnp.maximum(jnp.max(ra), jnp.max(rb)) > SB_DEAD_LOG2
                return jnp.logical_and(jj <= i, alive)

            def body(carry, i=i, qm=qm):
                jj, acc, ra, rb = carry
                c, (ra, rb) = pair(qm, i - jj, [ra, rb], False)
                return jj + 1, acc + c, ra, rb

            _, acc, _, _ = lax.while_loop(cond, body, (jnp.int32(2), acc, ra, rb))
        o_ref[i * blk:(i + 1) * blk, :] = acc


def stick_breaking_mixer(p_sb, batch, seq, heads=SB_HEADS, blk=256):
    w = p_sb.shape[1] // 3
    dh = w // heads
    pw = 2 * dh
    assert pw == LANES and seq % blk == 0
    npair = heads // 2
    return pl.pallas_call(
        functools.partial(_sb_kernel, dh=dh, blk=blk),
        grid=(batch, npair),
        in_specs=[
            pl.BlockSpec((seq, pw), lambda b, p: (b, p)),
            pl.BlockSpec((seq, pw), lambda b, p: (b, npair + p)),
            pl.BlockSpec((seq, pw), lambda b, p: (b, 2 * npair + p)),
        ],
        out_specs=pl.BlockSpec((seq, pw), lambda b, p: (b, p)),
        out_shape=jax.ShapeDtypeStruct((batch * seq, w), F32),
        compiler_params=_cparams(("parallel", "parallel")),
        name="stick_breaking_mixer",
    )(p_sb, p_sb, p_sb)


MIX_CROSS_VMEM_MB = 56


def _mix_cross_kernel(h_ref, ya_ref, yb_ref, yc_ref, gain_ref, wout_ref, cnw_ref, wq_ref, kv_ref,
                      wo_ref, o_ref, *, heads):
    wa = ya_ref.shape[1]
    wb = yb_ref.shape[1]
    d = h_ref.shape[1]
    gain = gain_ref[...]
    ya = _rms(ya_ref[...], gain[:, :wa]).astype(BF16)
    yb = yb_ref[...].astype(BF16)
    yc = _rms(yc_ref[...], gain[:, wa + wb:]).astype(BF16)
    wout = lambda r0, r1: wout_ref[r0:r1, :].astype(BF16)
    h1 = (h_ref[...] + _dot(ya, wout(0, wa)) + _dot(yb, wout(wa, wa + wb)) + _dot(yc, wout(wa + wb, d)))
    q = _dot(_rms(h1, cnw_ref[...]).astype(BF16), wq_ref[...].astype(BF16))
    dh = d // heads
    outs = []
    for hd in range(heads):
        qh = q[:, hd * dh:(hd + 1) * dh].astype(BF16)
        kh = kv_ref[:, hd * dh:(hd + 1) * dh]
        vh = kv_ref[:, d + hd * dh:d + (hd + 1) * dh]
        s = _dot_nt(qh, kh) * (dh ** -0.5)
        s = s - jnp.max(s, axis=-1, keepdims=True)
        e = jnp.exp(s)
        p = e / jnp.sum(e, axis=-1, keepdims=True)
        outs.append(_dot(p.astype(BF16), vh).astype(BF16))
    o_ref[...] = h1 + _dot(jnp.concatenate(outs, axis=1), wo_ref[...].astype(BF16))


def mix_cross(h, ya, yb, yc, gain, w_out, cnw, wq, kv, wo, layer, batch, seq, heads=CROSS_HEADS, tm=1024):
    t, d = h.shape
    m = kv.shape[0] // batch
    nt = seq // tm
    rows = lambda wd: pl.BlockSpec((tm, wd), lambda i: (i, 0))
    const = lambda shape: pl.BlockSpec(shape, lambda i: (0, 0))
    stack = pl.BlockSpec((None, d, d), lambda i: (layer, 0, 0))
    return pl.pallas_call(
        functools.partial(_mix_cross_kernel, heads=heads),
        grid=(t // tm,),
        in_specs=[rows(d), rows(ya.shape[1]), rows(yb.shape[1]), rows(yc.shape[1]),
                  const((1, d)), stack, const((1, d)), stack,
                  pl.BlockSpec((m, 2 * d), lambda i: (i // nt, 0)),
                  stack],
        out_specs=rows(d),
        out_shape=jax.ShapeDtypeStruct((t, d), F32),
        compiler_params=_cparams(("parallel",), MIX_CROSS_VMEM_MB),
        name="mix_cross",
    )(h, ya, yb, yc, gain.reshape(1, d), w_out, cnw.reshape(1, d), wq, kv, wo)


FFN_TM = 1024
FFN_TF = 1408
FFN_CHUNK = 256
FFN_VMEM_MB = 60


def _swiglu_accumulate(x, wg_ref, wu_ref, wd_ref, o_ref):
    tf = wd_ref.shape[0]
    for c0 in range(0, tf, FFN_CHUNK):
        c1 = min(c0 + FFN_CHUNK, tf)
        wg = wg_ref[:, c0:c1].astype(BF16)
        wu = wu_ref[:, c0:c1].astype(BF16)
        act = jax.nn.silu(_dot(x, wg)) * _dot(x, wu)
        o_ref[...] += _dot(act.astype(BF16), wd_ref[c0:c1, :].astype(BF16))


def _ffn_dense_kernel(h_ref, nw_ref, wg_ref, wu_ref, wd_ref, o_ref, xn_ref):
    f = pl.program_id(1)

    @pl.when(f == 0)
    def _():
        h = h_ref[...]
        xn_ref[...] = _rms(h, nw_ref[...]).astype(BF16)
        o_ref[...] = h

    _swiglu_accumulate(xn_ref[...], wg_ref, wu_ref, wd_ref, o_ref)


def ffn_dense(h, nw, w_gu, w_down, layer, tm=FFN_TM, tf=FFN_TF):
    t, d = h.shape
    ff = w_down.shape[1]
    nf = ff // tf
    return pl.pallas_call(
        _ffn_dense_kernel,
        grid=(t // tm, nf),
        in_specs=[
            pl.BlockSpec((tm, d), lambda i, f: (i, 0)),
            pl.BlockSpec((1, d), lambda i, f: (0, 0)),
            pl.BlockSpec((None, d, tf), lambda i, f: (layer, 0, f)),
            pl.BlockSpec((None, d, tf), lambda i, f: (layer, 0, nf + f)),
            pl.BlockSpec((None, tf, d), lambda i, f: (layer, f, 0)),
        ],
        out_specs=pl.BlockSpec((tm, d), lambda i, f: (i, 0)),
        out_shape=jax.ShapeDtypeStruct((t, d), F32),
        scratch_shapes=[pltpu.VMEM((tm, d), BF16)],
        compiler_params=_cparams(("parallel", "arbitrary"), FFN_VMEM_MB),
        name="ffn_dense",
    )(h, nw.reshape(1, d), w_gu, w_gu, w_down)


def _ffn_group_kernel(te_ref, nv_ref, rows_ref, x_ref, wg_ref, wu_ref, wd_ref, y_ref, xb_ref, acc_ref):
    i = pl.program_id(0)
    f = pl.program_id(1)
    valid = i < nv_ref[0]

    @pl.when(f == 0)
    def _():
        row = lax.broadcasted_iota(jnp.int32, (x_ref.shape[0], 1), 0)
        xb_ref[...] = jnp.where(row < rows_ref[i], _unpack_bf16_pairs(x_ref[...]), 0.0).astype(BF16)
        acc_ref[...] = jnp.zeros_like(acc_ref)

    @pl.when(valid)
    def _():
        _swiglu_accumulate(xb_ref[...], wg_ref.at[0], wu_ref.at[0], wd_ref.at[0], acc_ref)

    @pl.when(f == pl.num_programs(1) - 1)
    def _():
        half = acc_ref.shape[1] // 2
        y_ref[...] = _pack_bf16_pairs(acc_ref[:, :half], acc_ref[:, half:])


def ffn_grouped(xs, tile_expert, n_valid, tile_rows, w_gu, w_down, expert_base, tm=FFN_TM, tf=FFN_TF):
    p = xs.shape[0]
    d = 2 * xs.shape[1]
    ff = w_down.shape[1]
    nf = ff // tf

    def live(i, f, nv):
        ok = i < nv[0]
        return jnp.where(ok, i, nv[0] - 1), jnp.where(ok, f, nf - 1)

    def x_map(i, f, te, nv, rows):
        ii, _ = live(i, f, nv)
        return ii, 0

    def wg_map(i, f, te, nv, rows):
        ii, ff_ = live(i, f, nv)
        return expert_base + te[ii], 0, ff_

    def wu_map(i, f, te, nv, rows):
        ii, ff_ = live(i, f, nv)
        return expert_base + te[ii], 0, nf + ff_

    def wd_map(i, f, te, nv, rows):
        ii, ff_ = live(i, f, nv)
        return expert_base + te[ii], ff_, 0

    grid_spec = pltpu.PrefetchScalarGridSpec(
        num_scalar_prefetch=3,
        grid=(p // tm, nf),
        in_specs=[
            pl.BlockSpec((tm, d // 2), x_map),
            pl.BlockSpec((1, d, tf), wg_map),
            pl.BlockSpec((1, d, tf), wu_map),
            pl.BlockSpec((1, tf, d), wd_map),
        ],
        out_specs=pl.BlockSpec((tm, d // 2), lambda i, f, te, nv, rows: (i, 0)),
        scratch_shapes=[pltpu.VMEM((tm, d), BF16), pltpu.VMEM((tm, d), F32)],
    )
    return pl.pallas_call(
        _ffn_group_kernel,
        grid_spec=grid_spec,
        out_shape=jax.ShapeDtypeStruct((p, d // 2), jnp.uint32),
        compiler_params=_cparams(("arbitrary", "arbitrary"), FFN_VMEM_MB),
        name="ffn_grouped",
    )(tile_expert, n_valid, tile_rows, xs, w_gu, w_gu, w_down)


V7X_SC_CORES = 2
V7X_SC_SUBCORES = 16
V7X_SC_WORKERS = V7X_SC_CORES * V7X_SC_SUBCORES


def _sc_mesh():
    return plsc.VectorSubcoreMesh(core_axis_name="c", subcore_axis_name="s")


def sc_scatter_rows2(x, idx1, idx2, n_out, ch=128):
    t, d = x.shape
    assert t % (V7X_SC_WORKERS * ch) == 0
    t_per_w = t // V7X_SC_WORKERS
    n_ch = t_per_w // ch

    @functools.partial(
        pl.kernel, mesh=_sc_mesh(), out_type=jax.ShapeDtypeStruct((n_out, d), x.dtype),
        scratch_types=[pltpu.VMEM((ch,), jnp.int32), pltpu.VMEM((ch,), jnp.int32),
                       pltpu.VMEM((ch, d), x.dtype), pltpu.SemaphoreType.DMA])
    def k(x_hbm, i1_hbm, i2_hbm, out_hbm, i1_v, i2_v, rows_v, sem):
        wid = lax.axis_index("s") * V7X_SC_CORES + lax.axis_index("c")
        base = wid * t_per_w

        @pl.loop(0, n_ch)
        def _(j):
            off = pl.multiple_of(base + j * ch, 8)
            loads = [pltpu.async_copy(i1_hbm.at[pl.ds(off, ch)], i1_v, sem),
                     pltpu.async_copy(i2_hbm.at[pl.ds(off, ch)], i2_v, sem),
                     pltpu.async_copy(x_hbm.at[pl.ds(off, ch)], rows_v, sem)]
            for cp in loads:
                cp.wait()
            stores = [pltpu.async_copy(rows_v, out_hbm.at[i1_v], sem),
                      pltpu.async_copy(rows_v, out_hbm.at[i2_v], sem)]
            for cp in stores:
                cp.wait()

    return k(x, idx1, idx2)


def sc_gather_rows(table, idx, ch=128):
    _, d = table.shape
    b = idx.shape[0]
    assert b % (V7X_SC_WORKERS * ch) == 0
    b_per_w = b // V7X_SC_WORKERS
    n_ch = b_per_w // ch

    @functools.partial(
        pl.kernel, mesh=_sc_mesh(), out_type=jax.ShapeDtypeStruct((b, d), table.dtype),
        scratch_types=[pltpu.VMEM((ch,), jnp.int32), pltpu.VMEM((ch, d), table.dtype),
                       pltpu.SemaphoreType.DMA])
    def k(table_hbm, idx_hbm, out_hbm, idx_v, rows_v, sem):
        wid = lax.axis_index("s") * V7X_SC_CORES + lax.axis_index("c")
        base = wid * b_per_w

        @pl.loop(0, n_ch)
        def _(j):
            off = pl.multiple_of(base + j * ch, 8)
            pltpu.sync_copy(idx_hbm.at[pl.ds(off, ch)], idx_v)
            pltpu.async_copy(table_hbm.at[idx_v], rows_v, sem).wait()
            pltpu.sync_copy(rows_v, out_hbm.at[pl.ds(off, ch)])

    return k(table, idx)


def _moe_combine_kernel(h_ref, y1_ref, y2_ref, g1_ref, g2_ref, nw_ref, o_ref, *, normed):
    out = (h_ref[...] + g1_ref[...] * _unpack_bf16_pairs(y1_ref[...])
           + g2_ref[...] * _unpack_bf16_pairs(y2_ref[...]))
    o_ref[...] = _rms(out, nw_ref[...]) if normed else out


def moe_combine(h, yg, gate1, gate2, norm_w=None, tm=1024):
    t, d = h.shape
    nt = t // tm
    row = pl.BlockSpec((tm, d), lambda i: (i, 0))
    col = pl.BlockSpec((tm, 1), lambda i: (i, 0))
    nw = jnp.ones((d,), F32) if norm_w is None else norm_w
    return pl.pallas_call(
        functools.partial(_moe_combine_kernel, normed=norm_w is not None),
        grid=(nt,),
        in_specs=[row, pl.BlockSpec((tm, d // 2), lambda i: (i, 0)),
                  pl.BlockSpec((tm, d // 2), lambda i: (nt + i, 0)), col, col,
                  pl.BlockSpec((1, d), lambda i: (0, 0))],
        out_specs=row,
        out_shape=jax.ShapeDtypeStruct((t, d), F32),
        compiler_params=_cparams(("parallel",)),
        name="moe_combine",
    )(h, yg, yg, gate1, gate2, nw.reshape(1, d))


def _route(h, nw, wr, n_exp):
    xn = _rms(h, nw)
    half = xn.shape[1] // 2
    packed = _pack_bf16_pairs(xn[:, :half], xn[:, half:])
    xh = xn.astype(BF16)
    xl = (xn - xh.astype(F32)).astype(BF16)
    wh = wr.astype(BF16)
    wl = (wr - wh.astype(F32)).astype(BF16)
    logits = _dot(xh, wh) + _dot(xh, wl) + _dot(xl, wh)
    lane = lax.broadcasted_iota(jnp.int32, logits.shape, 1)
    neg = -jnp.inf
    lg = jnp.where(lane < n_exp, logits, neg)
    m1 = jnp.max(lg, axis=-1, keepdims=True)
    i1 = jnp.min(jnp.where(lg == m1, lane, LANES), axis=-1, keepdims=True)
    lg2 = jnp.where(lane == i1, neg, lg)
    m2 = jnp.max(lg2, axis=-1, keepdims=True)
    i2 = jnp.min(jnp.where(lg2 == m2, lane, LANES), axis=-1, keepdims=True)
    e2 = jnp.exp(m2 - m1)
    g1 = 1.0 / (1.0 + e2)
    g2 = e2 / (1.0 + e2)
    slot = jnp.where(lane == i1, 1, jnp.where(lane == i2, 2, 0))
    gate = jnp.where(lane == i1, g1, jnp.where(lane == i2, g2, 0.0))
    return packed, slot, gate


def _router_kernel(h_ref, nw_ref, wr_ref, tri_ref, xn_ref, slot_ref, gate_ref, rank_ref, cnt_ref, *, n_exp):
    @pl.when(pl.program_id(0) == 0)
    def _():
        cnt_ref[...] = jnp.zeros_like(cnt_ref)

    xn_ref[...], slot, gate = _route(h_ref[...], nw_ref[...], wr_ref[...], n_exp)
    slot_ref[...] = slot[:, :n_exp]
    gate_ref[...] = gate[:, :n_exp]
    chosen = jnp.where(slot > 0, 1.0, 0.0)
    tri = tri_ref[...]
    grp = tri.shape[0]
    seen = cnt_ref[...]
    for r0 in range(0, chosen.shape[0], grp):
        rows = chosen[r0:r0 + grp]
        before = _dot(tri, rows.astype(BF16)) + seen
        rank_ref[r0:r0 + grp, :] = before.astype(jnp.int32)[:, :n_exp]
        seen = seen + jnp.sum(rows, axis=0, keepdims=True)
    cnt_ref[...] = seen


def router(h, nw, w_router, tm=1024):
    t, d = h.shape
    n_exp = w_router.shape[1]
    wr = jnp.pad(w_router, ((0, 0), (0, LANES - n_exp)))
    tri = jnp.asarray(np.tril(np.ones((LANES, LANES), np.float32), -1), dtype=BF16)
    per_expert = pl.BlockSpec((tm, n_exp), lambda i: (i, 0))
    return pl.pallas_call(
        functools.partial(_router_kernel, n_exp=n_exp),
        grid=(t // tm,),
        in_specs=[pl.BlockSpec((tm, d), lambda i: (i, 0)),
                  pl.BlockSpec((1, d), lambda i: (0, 0)),
                  pl.BlockSpec((d, LANES), lambda i: (0, 0)),
                  pl.BlockSpec((LANES, LANES), lambda i: (0, 0))],
        out_specs=[pl.BlockSpec((tm, d // 2), lambda i: (i, 0)), per_expert, per_expert, per_expert,
                   pl.BlockSpec((1, LANES), lambda i: (0, 0))],
        out_shape=[jax.ShapeDtypeStruct((t, d // 2), jnp.uint32),
                   jax.ShapeDtypeStruct((t, n_exp), jnp.int32),
                   jax.ShapeDtypeStruct((t, n_exp), F32),
                   jax.ShapeDtypeStruct((t, n_exp), jnp.int32),
                   jax.ShapeDtypeStruct((1, LANES), F32)],
        compiler_params=_cparams(("arbitrary",)),
        name="router",
    )(h, nw.reshape(1, d), wr, tri)


def _dispatch_plan(slot, gate, rank, totals, tm):
    t, n_exp = slot.shape
    counts = totals[0, :n_exp].astype(jnp.int32)
    tiles = (counts + tm - 1) // tm
    tile_end = jnp.cumsum(tiles)
    start = (tile_end - tiles) * tm
    pos = start[None, :] + rank
    n_rows = 2 * t + n_exp * tm
    n_tiles = n_rows // tm
    pos1 = jnp.sum(jnp.where(slot == 1, pos, 0), axis=1)
    pos2 = jnp.sum(jnp.where(slot == 2, pos, 0), axis=1)
    gate1 = jnp.sum(jnp.where(slot == 1, gate, 0.0), axis=1, keepdims=True)
    gate2 = jnp.sum(jnp.where(slot == 2, gate, 0.0), axis=1, keepdims=True)
    tile_id = jnp.arange(n_tiles, dtype=jnp.int32)
    owner = tile_id[:, None] >= tile_end[None, :]
    tile_expert = jnp.minimum(jnp.sum(owner.astype(jnp.int32), axis=1), n_exp - 1)
    mine = tile_expert[:, None] == jnp.arange(n_exp, dtype=jnp.int32)[None, :]
    group_end = jnp.sum(jnp.where(mine, (start + counts)[None, :], 0), axis=1)
    n_valid = tile_end[-1:].astype(jnp.int32)
    tile_rows = jnp.where(tile_id < n_valid[0], jnp.clip(group_end - tile_id * tm, 0, tm), 0)
    return (n_rows, pos1, pos2, gate1, gate2, tile_expert.astype(jnp.int32), n_valid,
            tile_rows.astype(jnp.int32))


def _final_norm_kernel(h_ref, w_ref, o_ref):
    o_ref[...] = _rms(h_ref[...], w_ref[...])


def final_norm(h, w, tm=1024):
    t, d = h.shape
    return pl.pallas_call(
        _final_norm_kernel,
        grid=(t // tm,),
        in_specs=[pl.BlockSpec((tm, d), lambda i: (i, 0)), pl.BlockSpec((1, d), lambda i: (0, 0))],
        out_specs=pl.BlockSpec((tm, d), lambda i: (i, 0)),
        out_shape=jax.ShapeDtypeStruct((t, d), F32),
        compiler_params=_cparams(("parallel",)),
        name="final_norm",
    )(h, w.reshape(1, d))


def kernel(x, mem, mix_norm_w, w_in, ssm_lambda_re, ssm_lambda_im, ssm_b_re, ssm_b_im, ssm_c_re, ssm_c_im, ssm_d, ssm_log_dt, ssm_w_glu, mix_out_gain, w_out, cross_norm_w, mem_norm_w, w_cross_q, w_cross_kv, w_cross_o, ffn_norm_w, w_dense_gu, w_dense_down, w_router, w_expert_gu, w_expert_down, final_norm_w):
    batch, seq, d = x.shape
    depth = w_in.shape[0]
    mem_len = mem.shape[1]
    ssm_w = ssm_d.shape[1]
    g, p = ssm_lambda_re.shape[1:]
    nh = ssm_b_re.shape[-1]
    ret_w = ssm_w
    sb_w = d - ssm_w - ret_w
    ret_dh = ret_w // RET_HEADS
    moe_tm = FFN_TM

    w_in_b = jnp.concatenate(
        [w_in[..., :ssm_w],
         _group_rotary_halves(w_in[..., ssm_w:ssm_w + ret_w], RET_HEADS),
         _group_rotary_halves(w_in[..., ssm_w + ret_w:ssm_w + 2 * ret_w], RET_HEADS),
         w_in[..., ssm_w + 2 * ret_w:]], axis=-1).astype(BF16)
    n_exp = w_expert_gu.shape[1]
    w_egu = w_expert_gu.reshape((-1,) + w_expert_gu.shape[2:])
    w_edn = w_expert_down.reshape((-1,) + w_expert_down.shape[2:])
    n_a = ssm_w + 4 * ret_w

    apow, bbc = s5_prep(ssm_lambda_re, ssm_lambda_im, ssm_log_dt, ssm_b_re, ssm_b_im,
                        nstep=S5_TIME_TILE // SUBLANES)

    mem2 = mem.reshape(batch * mem_len, d)
    h = x.reshape(batch * seq, d)
    for i in range(depth):
        p_a, p_sb = norm_proj(h, mix_norm_w[i], w_in_b, i, (n_a, 3 * sb_w), (F32, BF16))
        bb = _block_diag_in(bbc[i], g, p, nh).astype(BF16)
        cc = _block_diag_out(ssm_c_re[i], ssm_c_im[i]).astype(BF16)
        y_a = s5_mixer(p_a, bb, cc, apow[i], ssm_d[i], ssm_w_glu[i].astype(BF16), batch, seq)
        gain = mix_out_gain[i]
        y_b = retention_mixer(p_a, ssm_w, gain[ssm_w:ssm_w + ret_w], batch, seq)
        y_c = stick_breaking_mixer(p_sb, batch, seq)
        (kv,) = norm_proj(mem2, mem_norm_w, w_cross_kv, i, (2 * d,), (BF16,))
        h = mix_cross(h, y_a, y_b, y_c, gain, w_out, cross_norm_w[i], w_cross_q, kv, w_cross_o, i, batch, seq)
        if i % 2 == 0:
            h = ffn_dense(h, ffn_norm_w[i], w_dense_gu, w_dense_down, i // 2)
        else:
            xn, slot, gate, rank, totals = router(h, ffn_norm_w[i], w_router[i // 2])
            n_rows, pos1, pos2, gate1, gate2, tile_expert, n_valid, tile_rows = _dispatch_plan(
                slot, gate, rank, totals, moe_tm)
            xs = sc_scatter_rows2(xn, pos1, pos2, n_rows)
            y = ffn_grouped(xs, tile_expert, n_valid, tile_rows, w_egu, w_edn, (i // 2) * n_exp,
                            tm=moe_tm)
            yg = sc_gather_rows(y, jnp.concatenate([pos1, pos2]))
            last = i == depth - 1
            h = moe_combine(h, yg, gate1, gate2, norm_w=final_norm_w if last else None)
    if depth % 2 == 1:
        h = final_norm(h, final_norm_w)
    return h.reshape(batch, seq, d)
```
